```python
import math
import jax, jax.numpy as jnp
from jax import lax
import numpy as np

D_MODEL = 1024
BATCH = 8
SEQ = 4096
DEPTH = 1

CHUNK = 64
Q_BLOCK = 128
MEM_LEN = 256
MIX_WIDTH = D_MODEL
FOX_WIDTH = MIX_WIDTH // 2
FOX_HEADS = 8
FOX_HEAD_DIM = FOX_WIDTH // FOX_HEADS
GLA_WIDTH = MIX_WIDTH - FOX_WIDTH
GLA_HEADS = 4
GLA_VAL_DIM = GLA_WIDTH // GLA_HEADS
GLA_KEY_DIM = GLA_VAL_DIM // 2
GLA_QK = GLA_HEADS * GLA_KEY_DIM
GLA_GATE_RANK = 16
GLA_GATE_NORM = 16.0
MEM_HEADS = 4
MEM_HEAD_DIM = D_MODEL // MEM_HEADS
D_FF = 128 * (-(-8 * D_MODEL // (3 * 128)))
RMS_EPS = 1e-6

OFF_FQ = 0
OFF_FK = OFF_FQ + FOX_WIDTH
OFF_FV = OFF_FK + FOX_WIDTH
OFF_FF = OFF_FV + FOX_WIDTH
OFF_GQ = OFF_FF + FOX_HEADS
OFF_GK = OFF_GQ + GLA_QK
OFF_GV = OFF_GK + GLA_QK
OFF_GG = OFF_GV + GLA_WIDTH
OFF_GR = OFF_GG + GLA_GATE_RANK
IN_WIDTH = OFF_GR + GLA_WIDTH

kernel_name = 'hybrid_fox_gla_macaron_memory_layer'

F32 = jnp.float32


def rms_norm(x, g):
    xf = x.astype(F32)
    y = xf * lax.rsqrt(jnp.mean(xf * xf, axis=-1, keepdims=True) + RMS_EPS)
    return (y * g.astype(F32)).astype(x.dtype)


def head_rms_norm(o, g):
    B, S, H, d = o.shape
    y = o * lax.rsqrt(jnp.mean(o * o, axis=-1, keepdims=True) + RMS_EPS)
    return (y * g.astype(F32).reshape(H, d)).reshape(B, S, H * d)


def swiglu(h, w_gate, w_up, w_down):
    return (jax.nn.silu(h @ w_gate) * (h @ w_up)) @ w_down


def forgetting_attention(q, k, v, log_f):
    B, S, H, dh = q.shape
    qf = (q.astype(F32) * (dh ** -0.5)).transpose(0, 2, 1, 3)
    kf = k.astype(F32).transpose(0, 2, 1, 3)
    vf = v.astype(F32).transpose(0, 2, 1, 3)
    c = jnp.cumsum(log_f, axis=1).transpose(0, 2, 1)
    outs = []
    for i in range(S // Q_BLOCK):
        q0 = i * Q_BLOCK
        L = q0 + Q_BLOCK
        s = jnp.einsum('bhqd,bhkd->bhqk', qf[:, :, q0:L], kf[:, :, :L])
        s = s + c[:, :, q0:L, None] - c[:, :, None, :L]
        mask = jnp.arange(L)[None, :] <= (q0 + jnp.arange(Q_BLOCK))[:, None]
        p = jax.nn.softmax(jnp.where(mask, s, -jnp.inf), axis=-1)
        outs.append(jnp.einsum('bhqk,bhkd->bhqd', p, vf[:, :, :L]))
    return jnp.concatenate(outs, axis=2).transpose(0, 2, 1, 3)


def gla_chunked(q, k, v, log_a):
    B, S, H, dk = q.shape
    dv = v.shape[-1]
    N = S // CHUNK

    def to_chunks(t):
        return t.astype(F32).reshape(B, N, CHUNK, H, t.shape[-1]).transpose(0, 3, 1, 2, 4)

    qc = to_chunks(q) * (dk ** -0.5)
    kc = to_chunks(k)
    vc = to_chunks(v)
    b = jnp.cumsum(to_chunks(log_a), axis=3)
    b_last = b[:, :, :, -1:, :]
    q_dec = qc * jnp.exp(b)
    k_inv = kc * jnp.exp(-b)
    causal = jnp.tril(jnp.ones((CHUNK, CHUNK), dtype=bool))
    att = jnp.where(causal, jnp.einsum('bhncd,bhnsd->bhncs', q_dec, k_inv), 0.0)
    o_intra = jnp.einsum('bhncs,bhnse->bhnce', att, vc)
    kv = jnp.einsum('bhncd,bhnce->bhnde', kc * jnp.exp(b_last - b), vc)
    decay = jnp.exp(b_last[:, :, :, 0, :])

    def step(state, inp):
        dec, kv_n = inp
        return dec[..., None] * state + kv_n, state

    s0 = jnp.zeros((B, H, dk, dv), F32)
    _, s_prev = lax.scan(step, s0, (jnp.moveaxis(decay, 2, 0), jnp.moveaxis(kv, 2, 0)))
    s_prev = jnp.moveaxis(s_prev, 0, 2)
    o = o_intra + jnp.einsum('bhncd,bhnde->bhnce', q_dec, s_prev)
    return o.transpose(0, 2, 3, 1, 4).reshape(B, S, H, dv)


def hybrid_mixer(h, w_in, b_f, w_g2, b_g, g_fox_out, g_gla_out, w_out):
    B, S, _ = h.shape
    z = h @ w_in
    fq = z[..., OFF_FQ:OFF_FK].reshape(B, S, FOX_HEADS, FOX_HEAD_DIM)
    fk = z[..., OFF_FK:OFF_FV].reshape(B, S, FOX_HEADS, FOX_HEAD_DIM)
    fv = z[..., OFF_FV:OFF_FF].reshape(B, S, FOX_HEADS, FOX_HEAD_DIM)
    log_f = jax.nn.log_sigmoid((z[..., OFF_FF:OFF_GQ] + b_f).astype(F32))
    o_fox = head_rms_norm(forgetting_attention(fq, fk, fv, log_f), g_fox_out)
    gq = z[..., OFF_GQ:OFF_GK].reshape(B, S, GLA_HEADS, GLA_KEY_DIM)
    gk = z[..., OFF_GK:OFF_GV].reshape(B, S, GLA_HEADS, GLA_KEY_DIM)
    gv = z[..., OFF_GV:OFF_GG].reshape(B, S, GLA_HEADS, GLA_VAL_DIM)
    gate = (z[..., OFF_GG:OFF_GR] @ w_g2 + b_g).astype(F32)
    log_a = (jax.nn.log_sigmoid(gate) / GLA_GATE_NORM).reshape(B, S, GLA_HEADS, GLA_KEY_DIM)
    o_gla = head_rms_norm(gla_chunked(gq, gk, gv, log_a), g_gla_out)
    o_gla = o_gla * jax.nn.silu(z[..., OFF_GR:IN_WIDTH].astype(F32))
    o = jnp.concatenate([o_fox, o_gla], axis=-1).astype(h.dtype)
    return o @ w_out


def memory_cross_attention(h, m, w_q, w_kv, w_o):
    B, S, _ = h.shape
    M = m.shape[1]
    q = (h @ w_q).reshape(B, S, MEM_HEADS, MEM_HEAD_DIM).astype(F32)
    kv = (m @ w_kv).reshape(B, M, 2, MEM_HEADS, MEM_HEAD_DIM).astype(F32)
    s = jnp.einsum('bqhd,bkhd->bhqk', q, kv[:, :, 0]) * (MEM_HEAD_DIM ** -0.5)
    p = jax.nn.softmax(s, axis=-1)
    o = jnp.einsum('bhqk,bkhd->bqhd', p, kv[:, :, 1]).reshape(B, S, MEM_HEADS * MEM_HEAD_DIM)
    return o.astype(h.dtype) @ w_o


def setup_inputs(seed: int = 0) -> dict:
    key = jax.random.key(seed)
    ks = iter(jax.random.split(key, 32))

    def normal(shape, scale):
        return jax.random.normal(next(ks), shape, F32) * scale

    def gain(n):
        return 1.0 + normal((DEPTH, n), 0.1)

    Ld = DEPTH
    return {
        'x': normal((BATCH, SEQ, D_MODEL), 1.0),
        'mem': normal((BATCH, MEM_LEN, D_MODEL), 1.0),
        'g_ff1_pre': gain(D_MODEL),
        'w_ff1_gate': normal((Ld, D_MODEL, D_FF), D_MODEL ** -0.5),
        'w_ff1_up': normal((Ld, D_MODEL, D_FF), D_MODEL ** -0.5),
        'w_ff1_down': normal((Ld, D_FF, D_MODEL), D_FF ** -0.5),
        'g_ff1_post': gain(D_MODEL),
        'g_mix_pre': gain(D_MODEL),
        'w_mix_in': normal((Ld, D_MODEL, IN_WIDTH), D_MODEL ** -0.5),
        'b_fox_f': 2.0 + normal((Ld, FOX_HEADS), 0.5),
        'w_gla_g2': normal((Ld, GLA_GATE_RANK, GLA_QK), GLA_GATE_RANK ** -0.5),
        'b_gla_g': normal((Ld, GLA_QK), 0.1),
        'g_fox_out': gain(FOX_WIDTH),
        'g_gla_out': gain(GLA_WIDTH),
        'w_mix_out': normal((Ld, MIX_WIDTH, D_MODEL), MIX_WIDTH ** -0.5),
        'g_mix_post': gain(D_MODEL),
        'g_mem_pre': gain(D_MODEL),
        'g_mem_src': gain(D_MODEL),
        'w_mem_q': normal((Ld, D_MODEL, MEM_HEADS * MEM_HEAD_DIM), D_MODEL ** -0.5),
        'w_mem_kv': normal((Ld, D_MODEL, 2 * MEM_HEADS * MEM_HEAD_DIM), D_MODEL ** -0.5),
        'w_mem_o': normal((Ld, MEM_HEADS * MEM_HEAD_DIM, D_MODEL), (MEM_HEADS * MEM_HEAD_DIM) ** -0.5),
        'g_mem_post': gain(D_MODEL),
        'g_ff2_pre': gain(D_MODEL),
        'w_ff2_gate': normal((Ld, D_MODEL, D_FF), D_MODEL ** -0.5),
        'w_ff2_up': normal((Ld, D_MODEL, D_FF), D_MODEL ** -0.5),
        'w_ff2_down': normal((Ld, D_FF, D_MODEL), D_FF ** -0.5),
        'g_ff2_post': gain(D_MODEL),
        'g_final': 1.0 + normal((D_MODEL,), 0.1),
    }


def reference(x, mem, g_ff1_pre, w_ff1_gate, w_ff1_up, w_ff1_down, g_ff1_post,
              g_mix_pre, w_mix_in, b_fox_f, w_gla_g2, b_gla_g, g_fox_out, g_gla_out,
              w_mix_out, g_mix_post, g_mem_pre, g_mem_src, w_mem_q, w_mem_kv, w_mem_o,
              g_mem_post, g_ff2_pre, w_ff2_gate, w_ff2_up, w_ff2_down, g_ff2_post, g_final):
    for l in range(DEPTH):
        y = swiglu(rms_norm(x, g_ff1_pre[l]), w_ff1_gate[l], w_ff1_up[l], w_ff1_down[l])
        x = x + 0.5 * rms_norm(y, g_ff1_post[l])
        y = hybrid_mixer(rms_norm(x, g_mix_pre[l]), w_mix_in[l], b_fox_f[l], w_gla_g2[l],
                         b_gla_g[l], g_fox_out[l], g_gla_out[l], w_mix_out[l])
        x = x + rms_norm(y, g_mix_post[l])
        y = memory_cross_attention(rms_norm(x, g_mem_pre[l]), rms_norm(mem, g_mem_src[l]),
                                   w_mem_q[l], w_mem_kv[l], w_mem_o[l])
        x = x + rms_norm(y, g_mem_post[l])
        y = swiglu(rms_norm(x, g_ff2_pre[l]), w_ff2_gate[l], w_ff2_up[l], w_ff2_down[l])
        x = x + 0.5 * rms_norm(y, g_ff2_post[l])
    return rms_norm(x, g_final)
```

```python
import functools

import jax
import jax.numpy as jnp
from jax import lax
from jax.experimental import pallas as pl
from jax.experimental.pallas import tpu as pltpu

F32 = jnp.float32
BF16 = jnp.bfloat16

D_MODEL = 1024
CHUNK = 64
FOX_WIDTH = 512
FOX_HEADS = 8
FOX_HEAD_DIM = 64
GLA_WIDTH = 512
GLA_HEADS = 4
GLA_VAL_DIM = 128
GLA_KEY_DIM = 64
GLA_QK = 256
GLA_GATE_RANK = 16
GLA_GATE_NORM = 16.0
MEM_HEADS = 4
MEM_HEAD_DIM = 256
D_FF = 2816
RMS_EPS = 1e-6

OFF_FQ = 0
OFF_FK = OFF_FQ + FOX_WIDTH
OFF_FV = OFF_FK + FOX_WIDTH
OFF_FF = OFF_FV + FOX_WIDTH
OFF_GQ = OFF_FF + FOX_HEADS
OFF_GK = OFF_GQ + GLA_QK
OFF_GV = OFF_GK + GLA_QK
OFF_GG = OFF_GV + GLA_WIDTH
OFF_GR = OFF_GG + GLA_GATE_RANK
IN_WIDTH = OFF_GR + GLA_WIDTH

LANES = 128
VMEM_LIMIT = 56 * 1024 * 1024

TM = 512
TQ = 256
TG = 512


def _rms(x, g):
    return x * lax.rsqrt(jnp.mean(x * x, axis=-1, keepdims=True) + RMS_EPS) * g


def _dot(a, b):
    return jnp.dot(a, b, preferred_element_type=F32)


def _dot_nt(a, b):
    return lax.dot_general(a, b, (((1,), (1,)), ((), ())), preferred_element_type=F32)


def _dot_tn(a, b):
    return lax.dot_general(a, b, (((0,), (0,)), ((), ())), preferred_element_type=F32)


def _split3(x):
    hi = x.astype(BF16)
    r1 = x - hi.astype(F32)
    mid = r1.astype(BF16)
    lo = (r1 - mid.astype(F32)).astype(BF16)
    return hi, mid, lo


def _tri_cumsum(tri_bf16, x):
    hi, mid, lo = _split3(x)
    return _dot(tri_bf16, hi) + _dot(tri_bf16, mid) + _dot(tri_bf16, lo)


def _log_sigmoid(x):
    return jnp.minimum(x, 0.0) - jnp.log1p(jnp.exp(-jnp.abs(x)))


def _silu(x):
    return x * jax.nn.sigmoid(x)


def _const_spec(shape):
    return pl.BlockSpec(shape, lambda *_: (0,) * len(shape), pipeline_mode=pl.Buffered(1))


def _ffn_kernel(x_ref, gpre_ref, wg_ref, wu_ref, wd_ref, gpost_ref, gfin_ref, o_ref, *, final):
    x = x_ref[...]
    h = _rms(x, gpre_ref[...]).astype(BF16)
    a = (_silu(_dot(h, wg_ref[...])) * _dot(h, wu_ref[...])).astype(BF16)
    y = _dot(a, wd_ref[...])
    out = x + 0.5 * _rms(y, gpost_ref[...])
    if final:
        out = _rms(out, gfin_ref[...])
    o_ref[...] = out


def _ffn(x, gpre, wg, wu, wd, gpost, gfin, final):
    t = x.shape[0]
    return pl.pallas_call(
        functools.partial(_ffn_kernel, final=final),
        grid=(t // TM,),
        in_specs=[
            pl.BlockSpec((TM, D_MODEL), lambda i: (i, 0)),
            _const_spec((1, D_MODEL)),
            _const_spec((D_MODEL, D_FF)),
            _const_spec((D_MODEL, D_FF)),
            _const_spec((D_FF, D_MODEL)),
            _const_spec((1, D_MODEL)),
            _const_spec((1, D_MODEL)),
        ],
        out_specs=pl.BlockSpec((TM, D_MODEL), lambda i: (i, 0)),
        out_shape=jax.ShapeDtypeStruct((t, D_MODEL), F32),
        compiler_params=pltpu.CompilerParams(
            dimension_semantics=("arbitrary",), vmem_limit_bytes=VMEM_LIMIT),
        name="ffn_final" if final else "ffn",
    )(x, gpre, wg, wu, wd, gpost, gfin)


def _mix_in_kernel(x_ref, g_ref, wfox_ref, wgla_ref, wgr_ref, wff_ref, wgg_ref, bf_ref,
                   wg2_ref, bg_ref,
                   fqkv_ref, c_ref, gqk_ref, gv_ref, la_ref, sr_ref,
                   carry_ref, *, tiles_per_seq):
    i = pl.program_id(0)

    @pl.when(i % tiles_per_seq == 0)
    def _():
        carry_ref[...] = jnp.zeros_like(carry_ref)

    h = _rms(x_ref[...], g_ref[...]).astype(BF16)
    fqkv_ref[...] = _dot(h, wfox_ref[...]).astype(BF16)
    gl = _dot(h, wgla_ref[...])
    gqk_ref[...] = gl[:, :2 * GLA_QK].astype(BF16)
    gv_ref[...] = gl[:, 2 * GLA_QK:].astype(BF16)
    sr_ref[...] = _silu(_dot(h, wgr_ref[...])).astype(BF16)

    lf = _log_sigmoid(_dot(h, wff_ref[...]) + bf_ref[...])
    row = lax.broadcasted_iota(jnp.int32, (TM, TM), 0)
    col = lax.broadcasted_iota(jnp.int32, (TM, TM), 1)
    tri = jnp.where(col <= row, 1.0, 0.0).astype(BF16)
    c = _tri_cumsum(tri, lf) + carry_ref[...]
    carry_ref[...] = c[TM - 1:TM, :]
    c_ref[...] = c[:, :FOX_HEADS]

    gg = _dot(h, wgg_ref[...])
    g_hi, g_mid, _ = _split3(gg)
    w2 = wg2_ref[...]
    w_hi, w_mid, _ = _split3(w2)
    gate = _dot(g_hi, w_hi) + _dot(g_mid, w_hi) + _dot(g_hi, w_mid) + bg_ref[...]
    la_ref[...] = _log_sigmoid(gate) / GLA_GATE_NORM


def _mix_in(x, g, wfox, wgla, wgr, wff, wgg, bf, wg2, bg, seq):
    t = x.shape[0]
    row = lambda w: pl.BlockSpec((TM, w), lambda i: (i, 0))
    return pl.pallas_call(
        functools.partial(_mix_in_kernel, tiles_per_seq=seq // TM),
        grid=(t // TM,),
        in_specs=[
            row(D_MODEL),
            _const_spec((1, D_MODEL)),
            _const_spec((D_MODEL, 3 * FOX_WIDTH)),
            _const_spec((D_MODEL, 2 * GLA_QK + GLA_WIDTH)),
            _const_spec((D_MODEL, GLA_WIDTH)),
            _const_spec((D_MODEL, LANES)),
            _const_spec((D_MODEL, LANES)),
            _const_spec((1, LANES)),
            _const_spec((LANES, GLA_QK)),
            _const_spec((1, GLA_QK)),
        ],
        out_specs=[row(3 * FOX_WIDTH), row(FOX_HEADS), row(2 * GLA_QK), row(GLA_WIDTH),
                   row(GLA_QK), row(GLA_WIDTH)],
        out_shape=[
            jax.ShapeDtypeStruct((t, 3 * FOX_WIDTH), BF16),
            jax.ShapeDtypeStruct((t, FOX_HEADS), F32),
            jax.ShapeDtypeStruct((t, 2 * GLA_QK), BF16),
            jax.ShapeDtypeStruct((t, GLA_WIDTH), BF16),
            jax.ShapeDtypeStruct((t, GLA_QK), F32),
            jax.ShapeDtypeStruct((t, GLA_WIDTH), BF16),
        ],
        scratch_shapes=[pltpu.VMEM((1, LANES), F32)],
        compiler_params=pltpu.CompilerParams(
            dimension_semantics=("arbitrary",), vmem_limit_bytes=VMEM_LIMIT),
        name="mix_in",
    )(x, g, wfox, wgla, wgr, wff, wgg, bf, wg2, bg)


def _fox_kernel(q_ref, k_ref, v_ref, ccol_ref, crow_ref, g_ref, o_ref):
    i = pl.program_id(2)
    lane = lax.broadcasted_iota(jnp.int32, (1, LANES), 1)
    first = lane < FOX_HEAD_DIM
    q = q_ref[...] * jnp.asarray(FOX_HEAD_DIM ** -0.5, BF16)
    zero = jnp.zeros_like(q)
    qs = (jnp.where(first, q, zero), jnp.where(first, zero, q))
    cqs = (ccol_ref[0, 0, :, 0:1], ccol_ref[0, 0, :, 1:2])
    row = lax.broadcasted_iota(jnp.int32, (TQ, TQ), 0)
    col = lax.broadcasted_iota(jnp.int32, (TQ, TQ), 1)
    causal = col <= row

    def step(j, carry, diagonal):
        start = pl.multiple_of(j * TQ, TQ)
        kj = k_ref[pl.ds(start, TQ), :]
        vj = v_ref[pl.ds(start, TQ), :]
        ck = crow_ref[0, 0, j]
        new = []
        for a in range(2):
            m, l, acc = carry[a]
            s = _dot_nt(qs[a], kj) + (cqs[a] - ck[a:a + 1, :])
            if diagonal:
                s = jnp.where(causal, s, -jnp.inf)
            m_new = jnp.maximum(m, jnp.max(s, axis=-1, keepdims=True))
            alpha = jnp.exp(m - m_new)
            p = jnp.exp(s - m_new)
            l = alpha * l + jnp.sum(p, axis=-1, keepdims=True)
            acc = alpha * acc + _dot(p.astype(BF16), vj)
            new.append((m_new, l, acc))
        return tuple(new)

    init = tuple((jnp.full((TQ, 1), -jnp.inf, F32), jnp.zeros((TQ, 1), F32),
                  jnp.zeros((TQ, LANES), F32)) for _ in range(2))
    carry = lax.fori_loop(0, i, lambda j, c: step(j, c, False), init)
    (_, l0, acc0), (_, l1, acc1) = step(i, carry, True)
    o = jnp.where(first, acc0 / l0, acc1 / l1)
    o2 = o * o
    ms0 = jnp.sum(jnp.where(first, o2, 0.0), axis=-1, keepdims=True)
    ms1 = jnp.sum(jnp.where(first, 0.0, o2), axis=-1, keepdims=True)
    ms = jnp.where(first, ms0, ms1) * (1.0 / FOX_HEAD_DIM)
    o_ref[...] = (o * lax.rsqrt(ms + RMS_EPS) * g_ref[...]).astype(BF16)


def _fox(fqkv, ccol, crow, g, batch, seq):
    nq = seq // TQ
    pairs = FOX_WIDTH // LANES
    return pl.pallas_call(
        _fox_kernel,
        grid=(batch, pairs, nq),
        in_specs=[
            pl.BlockSpec((TQ, LANES), lambda b, p, i: (b * nq + i, p)),
            pl.BlockSpec((seq, LANES), lambda b, p, i: (b, pairs + p)),
            pl.BlockSpec((seq, LANES), lambda b, p, i: (b, 2 * pairs + p)),
            pl.BlockSpec((1, 1, TQ, 2), lambda b, p, i: (b, p, i, 0)),
            pl.BlockSpec((1, 1, nq, 2, TQ), lambda b, p, i: (b, p, 0, 0, 0)),
            pl.BlockSpec((1, LANES), lambda b, p, i: (0, p)),
        ],
        out_specs=pl.BlockSpec((TQ, LANES), lambda b, p, i: (b * nq + i, p)),
        out_shape=jax.ShapeDtypeStruct((batch * seq, FOX_WIDTH), BF16),
        compiler_params=pltpu.CompilerParams(
            dimension_semantics=("arbitrary", "arbitrary", "arbitrary"),
            vmem_limit_bytes=VMEM_LIMIT),
        name="fox",
    )(fqkv, fqkv, fqkv, ccol, crow, g)


def _gla_kernel(qk_ref, v_ref, la_ref, sr_ref, g_ref, o_ref, st_ref):
    @pl.when(pl.program_id(1) == 0)
    def _():
        st_ref[...] = jnp.zeros_like(st_ref)

    lane = lax.broadcasted_iota(jnp.int32, (1, LANES), 1)
    first = lane < GLA_KEY_DIM
    row = lax.broadcasted_iota(jnp.int32, (CHUNK, CHUNK), 0)
    col = lax.broadcasted_iota(jnp.int32, (CHUNK, CHUNK), 1)
    causal = col <= row
    tri = jnp.where(causal, 1.0, 0.0).astype(BF16)
    scale = GLA_KEY_DIM ** -0.5

    def chunk(n, _):
        r0 = pl.multiple_of(n * CHUNK, CHUNK)
        rows = pl.ds(r0, CHUNK)
        b = _tri_cumsum(tri, la_ref[rows, :])
        b_last = b[CHUNK - 1:CHUNK, :]
        q = qk_ref[rows, 0:GLA_QK].astype(F32)
        k = qk_ref[rows, GLA_QK:2 * GLA_QK].astype(F32)
        q_dec = (q * scale * jnp.exp(b)).astype(BF16)
        k_inv = (k * jnp.exp(-b)).astype(BF16)
        k_rem = (k * jnp.exp(b_last - b)).astype(BF16)
        decay = jnp.exp(b_last)
        zero = jnp.zeros((CHUNK, LANES), BF16)
        for h in range(GLA_HEADS):
            pr = slice((h // 2) * LANES, (h // 2 + 1) * LANES)
            hmask = first if h % 2 == 0 else jnp.logical_not(first)
            qd = jnp.where(hmask, q_dec[:, pr], zero)
            kr = jnp.where(hmask, k_rem[:, pr], zero)
            vh = v_ref[rows, h * GLA_VAL_DIM:(h + 1) * GLA_VAL_DIM]
            att = jnp.where(causal, _dot_nt(qd, k_inv[:, pr]), 0.0)
            st = st_ref[h]
            o = _dot(att.astype(BF16), vh) + _dot_nt(qd, st.astype(BF16))
            st_ref[h] = st * decay[:, pr] + _dot_tn(vh, kr)
            ms = jnp.mean(o * o, axis=-1, keepdims=True)
            cols = slice(h * GLA_VAL_DIM, (h + 1) * GLA_VAL_DIM)
            y = o * lax.rsqrt(ms + RMS_EPS) * g_ref[:, cols] * sr_ref[rows, cols].astype(F32)
            o_ref[rows, cols] = y.astype(BF16)
        return 0

    lax.fori_loop(0, TG // CHUNK, chunk, 0)


def _gla(gqk, gv, la, sr, g, batch, seq):
    nt = seq // TG
    row = lambda w: pl.BlockSpec((TG, w), lambda b, i: (b * nt + i, 0))
    return pl.pallas_call(
        _gla_kernel,
        grid=(batch, nt),
        in_specs=[row(2 * GLA_QK), row(GLA_WIDTH), row(GLA_QK), row(GLA_WIDTH),
                  pl.BlockSpec((1, GLA_WIDTH), lambda b, i: (0, 0))],
        out_specs=row(GLA_WIDTH),
        out_shape=jax.ShapeDtypeStruct((batch * seq, GLA_WIDTH), BF16),
        scratch_shapes=[pltpu.VMEM((GLA_HEADS, GLA_VAL_DIM, LANES), F32)],
        compiler_params=pltpu.CompilerParams(
            dimension_semantics=("arbitrary", "arbitrary"), vmem_limit_bytes=VMEM_LIMIT),
        name="gla",
    )(gqk, gv, la, sr, g)


def _mem_kv_kernel(m_ref, g_ref, w_ref, o_ref):
    h = _rms(m_ref[...], g_ref[...]).astype(BF16)
    o_ref[...] = _dot(h, w_ref[...]).astype(BF16)


def _mem_kv(mem, g, w, mem_len):
    t = mem.shape[0]
    width = w.shape[1]
    return pl.pallas_call(
        _mem_kv_kernel,
        grid=(t // mem_len,),
        in_specs=[pl.BlockSpec((mem_len, D_MODEL), lambda i: (i, 0)),
                  _const_spec((1, D_MODEL)), _const_spec((D_MODEL, width))],
        out_specs=pl.BlockSpec((mem_len, width), lambda i: (i, 0)),
        out_shape=jax.ShapeDtypeStruct((t, width), BF16),
        compiler_params=pltpu.CompilerParams(
            dimension_semantics=("arbitrary",), vmem_limit_bytes=VMEM_LIMIT),
        name="mem_kv",
    )(mem, g, w)


def _mix_out_mem_kernel(x_ref, of_ref, og_ref, kv_ref, wout_ref, gpost_ref, gpre_ref,
                        wq_ref, wo_ref, gmpost_ref, o_ref):
    y = _dot(of_ref[...], wout_ref[0:FOX_WIDTH, :]) + _dot(og_ref[...], wout_ref[FOX_WIDTH:, :])
    x = x_ref[...] + _rms(y, gpost_ref[...])
    h = _rms(x, gpre_ref[...]).astype(BF16)
    q = (_dot(h, wq_ref[...]) * (MEM_HEAD_DIM ** -0.5)).astype(BF16)
    width = MEM_HEADS * MEM_HEAD_DIM
    outs = []
    for hd in range(MEM_HEADS):
        cols = slice(hd * MEM_HEAD_DIM, (hd + 1) * MEM_HEAD_DIM)
        s = _dot_nt(q[:, cols], kv_ref[:, cols])
        m = jnp.max(s, axis=-1, keepdims=True)
        p = jnp.exp(s - m)
        p = p / jnp.sum(p, axis=-1, keepdims=True)
        vcols = slice(width + hd * MEM_HEAD_DIM, width + (hd + 1) * MEM_HEAD_DIM)
        outs.append(_dot(p.astype(BF16), kv_ref[:, vcols]).astype(BF16))
    o = jnp.concatenate(outs, axis=-1)
    y2 = _dot(o, wo_ref[...])
    o_ref[...] = x + _rms(y2, gmpost_ref[...])


def _mix_out_mem(x, o_fox, o_gla, kv, wout, gpost, gpre, wq, wo, gmpost, seq, mem_len):
    t = x.shape[0]
    tiles_per_seq = seq // TM
    row = lambda w: pl.BlockSpec((TM, w), lambda i: (i, 0))
    return pl.pallas_call(
        _mix_out_mem_kernel,
        grid=(t // TM,),
        in_specs=[
            row(D_MODEL), row(FOX_WIDTH), row(GLA_WIDTH),
            pl.BlockSpec((mem_len, kv.shape[1]), lambda i: (i // tiles_per_seq, 0)),
            _const_spec((D_MODEL, D_MODEL)), _const_spec((1, D_MODEL)), _const_spec((1, D_MODEL)),
            _const_spec((D_MODEL, D_MODEL)), _const_spec((D_MODEL, D_MODEL)),
            _const_spec((1, D_MODEL)),
        ],
        out_specs=row(D_MODEL),
        out_shape=jax.ShapeDtypeStruct((t, D_MODEL), F32),
        compiler_params=pltpu.CompilerParams(
            dimension_semantics=("arbitrary",), vmem_limit_bytes=VMEM_LIMIT),
        name="mix_out_mem",
    )(x, o_fox, o_gla, kv, wout, gpost, gpre, wq, wo, gmpost)


def _pad_cols(w, width):
    return jnp.pad(w, ((0, 0), (0, width - w.shape[1])))


def kernel(x, mem, g_ff1_pre, w_ff1_gate, w_ff1_up, w_ff1_down, g_ff1_post, g_mix_pre, w_mix_in, b_fox_f, w_gla_g2, b_gla_g, g_fox_out, g_gla_out, w_mix_out, g_mix_post, g_mem_pre, g_mem_src, w_mem_q, w_mem_kv, w_mem_o, g_mem_post, g_ff2_pre, w_ff2_gate, w_ff2_up, w_ff2_down, g_ff2_post, g_final):
    batch, seq, d = x.shape
    mem_len = mem.shape[1]
    depth = w_ff1_gate.shape[0]
    xt = x.reshape(batch * seq, d)
    memt = mem.reshape(batch * mem_len, d)
    gfin = g_final.reshape(1, d)
    vec = lambda v: v.reshape(1, -1).astype(F32)
    for l in range(depth):
        last = l == depth - 1
        xt = _ffn(xt, vec(g_ff1_pre[l]), w_ff1_gate[l].astype(BF16), w_ff1_up[l].astype(BF16),
                  w_ff1_down[l].astype(BF16), vec(g_ff1_post[l]), gfin, False)

        w_in = w_mix_in[l]
        wfox = w_in[:, OFF_FQ:OFF_FF].astype(BF16)
        wgla = w_in[:, OFF_GQ:OFF_GG].astype(BF16)
        wgr = w_in[:, OFF_GR:IN_WIDTH].astype(BF16)
        wff = _pad_cols(w_in[:, OFF_FF:OFF_GQ], LANES).astype(BF16)
        wgg = _pad_cols(w_in[:, OFF_GG:OFF_GR], LANES).astype(BF16)
        bf = _pad_cols(vec(b_fox_f[l]), LANES)
        wg2 = jnp.pad(w_gla_g2[l].astype(F32), ((0, LANES - GLA_GATE_RANK), (0, 0)))
        fqkv, c, gqk, gv, la, sr = _mix_in(xt, vec(g_mix_pre[l]), wfox, wgla, wgr, wff, wgg, bf,
                                           wg2, vec(b_gla_g[l]), seq)

        pairs = FOX_WIDTH // LANES
        c4 = c.reshape(batch, seq, pairs, 2)
        ccol = c4.transpose(0, 2, 1, 3)
        crow = c4.reshape(batch, seq // TQ, TQ, pairs, 2).transpose(0, 3, 1, 4, 2)
        o_fox = _fox(fqkv, ccol, crow, vec(g_fox_out[l]), batch, seq)
        o_gla = _gla(gqk, gv, la, sr, vec(g_gla_out[l]), batch, seq)

        kv = _mem_kv(memt, vec(g_mem_src[l]), w_mem_kv[l].astype(BF16), mem_len)
        xt = _mix_out_mem(xt, o_fox, o_gla, kv, w_mix_out[l].astype(BF16), vec(g_mix_post[l]),
                          vec(g_mem_pre[l]), w_mem_q[l].astype(BF16), w_mem_o[l].astype(BF16),
                          vec(g_mem_post[l]), seq, mem_len)

        xt = _ffn(xt, vec(g_ff2_pre[l]), w_ff2_gate[l].astype(BF16), w_ff2_up[l].astype(BF16),
                  w_ff2_down[l].astype(BF16), vec(g_ff2_post[l]), gfin, last)
    if depth == 0:
        xt = _rms(xt, gfin)
    return xt.reshape(batch, seq, d)
```

```python
import functools

import numpy as np
import jax
import jax.numpy as jnp
from jax import lax
from jax.experimental import pallas as pl
from jax.experimental.pallas import tpu as pltpu

F32 = jnp.float32
BF16 = jnp.bfloat16

D_MODEL = 1024
CHUNK = 64
FOX_WIDTH = 512
FOX_HEADS = 8
FOX_HEAD_DIM = 64
GLA_WIDTH = 512
GLA_HEADS = 4
GLA_VAL_DIM = 128
GLA_KEY_DIM = 64
GLA_QK = 256
GLA_GATE_RANK = 16
GLA_GATE_NORM = 16.0
MEM_HEADS = 4
MEM_HEAD_DIM = 256
D_FF = 2816
RMS_EPS = 1e-6

OFF_FQ = 0
OFF_FK = OFF_FQ + FOX_WIDTH
OFF_FV = OFF_FK + FOX_WIDTH
OFF_FF = OFF_FV + FOX_WIDTH
OFF_GQ = OFF_FF + FOX_HEADS
OFF_GK = OFF_GQ + GLA_QK
OFF_GV = OFF_GK + GLA_QK
OFF_GG = OFF_GV + GLA_WIDTH
OFF_GR = OFF_GG + GLA_GATE_RANK
IN_WIDTH = OFF_GR + GLA_WIDTH

LANES = 128
VMEM_LIMIT = 56 * 1024 * 1024

TM = 512
TQ = 256
TG = 512

FOX_PAD = FOX_HEADS * LANES
C_TERMS = 3
ONE_LANE = C_TERMS * FOX_HEADS
AUG0 = FOX_HEAD_DIM
AUG1 = FOX_HEAD_DIM + C_TERMS
V_ONE_ROW = FOX_HEAD_DIM


def _rms(x, g):
    return x * lax.rsqrt(jnp.mean(x * x, axis=-1, keepdims=True) + RMS_EPS) * g


def _dot(a, b):
    return jnp.dot(a, b, preferred_element_type=F32)


def _dot_nt(a, b):
    return lax.dot_general(a, b, (((1,), (1,)), ((), ())), preferred_element_type=F32)


def _dot_tn(a, b):
    return lax.dot_general(a, b, (((0,), (0,)), ((), ())), preferred_element_type=F32)


def _split3(x):
    hi = x.astype(BF16)
    r1 = x - hi.astype(F32)
    mid = r1.astype(BF16)
    lo = (r1 - mid.astype(F32)).astype(BF16)
    return hi, mid, lo


def _tri_cumsum(tri_bf16, x):
    hi, mid, lo = _split3(x)
    return _dot(tri_bf16, hi) + _dot(tri_bf16, mid) + _dot(tri_bf16, lo)


def _log_sigmoid(x):
    return jnp.minimum(x, 0.0) - jnp.log1p(jnp.exp(-jnp.abs(x)))


def _silu(x):
    return x * jax.nn.sigmoid(x)


def _const_spec(shape):
    return pl.BlockSpec(shape, lambda *_: (0,) * len(shape), pipeline_mode=pl.Buffered(1))


def _ffn_kernel(x_ref, gpre_ref, wg_ref, wu_ref, wd_ref, gpost_ref, gfin_ref, o_ref, *, final):
    x = x_ref[...]
    h = _rms(x, gpre_ref[...]).astype(BF16)
    a = (_silu(_dot(h, wg_ref[...])) * _dot(h, wu_ref[...])).astype(BF16)
    y = _dot(a, wd_ref[...])
    out = x + 0.5 * _rms(y, gpost_ref[...])
    if final:
        out = _rms(out, gfin_ref[...])
    o_ref[...] = out


def _ffn(x, gpre, wg, wu, wd, gpost, gfin, final):
    t = x.shape[0]
    return pl.pallas_call(
        functools.partial(_ffn_kernel, final=final),
        grid=(t // TM,),
        in_specs=[
            pl.BlockSpec((TM, D_MODEL), lambda i: (i, 0)),
            _const_spec((1, D_MODEL)),
            _const_spec((D_MODEL, D_FF)),
            _const_spec((D_MODEL, D_FF)),
            _const_spec((D_FF, D_MODEL)),
            _const_spec((1, D_MODEL)),
            _const_spec((1, D_MODEL)),
        ],
        out_specs=pl.BlockSpec((TM, D_MODEL), lambda i: (i, 0)),
        out_shape=jax.ShapeDtypeStruct((t, D_MODEL), F32),
        compiler_params=pltpu.CompilerParams(
            dimension_semantics=("arbitrary",), vmem_limit_bytes=VMEM_LIMIT),
        name="ffn_final" if final else "ffn",
    )(x, gpre, wg, wu, wd, gpost, gfin)


def _mix_in_kernel(x_ref, g_ref, wk_ref, wqt_ref, wvt_ref, wgla_ref, wgr_ref, wff_ref, wgg_ref,
                   bf_ref, pk_ref, pqt_ref, wg2_ref, bg_ref,
                   k_ref, qt_ref, vt_ref, gqk_ref, gv_ref, la_ref, sr_ref,
                   carry_ref, *, tiles_per_seq):
    i = pl.program_id(0)

    @pl.when(i % tiles_per_seq == 0)
    def _():
        carry_ref[...] = jnp.zeros_like(carry_ref)

    h = _rms(x_ref[...], g_ref[...]).astype(BF16)
    gl = _dot(h, wgla_ref[...])
    gqk_ref[...] = gl[:, :2 * GLA_QK].astype(BF16)
    gv_ref[...] = gl[:, 2 * GLA_QK:].astype(BF16)
    sr_ref[...] = _silu(_dot(h, wgr_ref[...])).astype(BF16)

    lane = lax.broadcasted_iota(jnp.int32, (1, LANES), 1)
    lf = jnp.where(lane < FOX_HEADS, _log_sigmoid(_dot(h, wff_ref[...]) + bf_ref[...]), 0.0)
    row = lax.broadcasted_iota(jnp.int32, (TM, TM), 0)
    col = lax.broadcasted_iota(jnp.int32, (TM, TM), 1)
    tri = jnp.where(col <= row, 1.0, 0.0).astype(BF16)
    c = _tri_cumsum(tri, lf) + carry_ref[...]
    carry_ref[...] = c[TM - 1:TM, :]
    c_hi, c_mid, c_lo = _split3(c)
    packed = (c_hi.astype(F32) + pltpu.roll(c_mid.astype(F32), FOX_HEADS, 1)
              + pltpu.roll(c_lo.astype(F32), 2 * FOX_HEADS, 1)
              + jnp.where(lane == ONE_LANE, 1.0, 0.0)).astype(BF16)

    k_ref[...] = (_dot(h, wk_ref[...]) + _dot(packed, pk_ref[...])).astype(BF16)
    qt = (_dot_nt(wqt_ref[...], h) + _dot_nt(pqt_ref[...], packed)).astype(BF16)
    vrow = lax.broadcasted_iota(jnp.int32, (FOX_PAD, 1), 0) % LANES
    vt = jnp.where(vrow == V_ONE_ROW, 1.0, _dot_nt(wvt_ref[...], h)).astype(BF16)
    for n in range(TM // TQ):
        qt_ref[n] = qt[:, n * TQ:(n + 1) * TQ]
        vt_ref[n] = vt[:, n * TQ:(n + 1) * TQ]

    gg = _dot(h, wgg_ref[...])
    g_hi, g_mid, _ = _split3(gg)
    w_hi, w_mid, _ = _split3(wg2_ref[...])
    gate = _dot(g_hi, w_hi) + _dot(g_mid, w_hi) + _dot(g_hi, w_mid) + bg_ref[...]
    la_ref[...] = _log_sigmoid(gate) / GLA_GATE_NORM


def _mix_in(x, g, wk, wqt, wvt, wgla, wgr, wff, wgg, bf, pk, pqt, wg2, bg, seq):
    t = x.shape[0]
    row = lambda w: pl.BlockSpec((TM, w), lambda i: (i, 0))
    tblk = pl.BlockSpec((TM // TQ, FOX_PAD, TQ), lambda i: (i, 0, 0))
    return pl.pallas_call(
        functools.partial(_mix_in_kernel, tiles_per_seq=seq // TM),
        grid=(t // TM,),
        in_specs=[
            row(D_MODEL),
            _const_spec((1, D_MODEL)),
            _const_spec((D_MODEL, FOX_PAD)),
            _const_spec((FOX_PAD, D_MODEL)),
            _const_spec((FOX_PAD, D_MODEL)),
            _const_spec((D_MODEL, 2 * GLA_QK + GLA_WIDTH)),
            _const_spec((D_MODEL, GLA_WIDTH)),
            _const_spec((D_MODEL, LANES)),
            _const_spec((D_MODEL, LANES)),
            _const_spec((1, LANES)),
            _const_spec((LANES, FOX_PAD)),
            _const_spec((FOX_PAD, LANES)),
            _const_spec((LANES, GLA_QK)),
            _const_spec((1, GLA_QK)),
        ],
        out_specs=[row(FOX_PAD), tblk, tblk, row(2 * GLA_QK), row(GLA_WIDTH),
                   row(GLA_QK), row(GLA_WIDTH)],
        out_shape=[
            jax.ShapeDtypeStruct((t, FOX_PAD), BF16),
            jax.ShapeDtypeStruct((t // TQ, FOX_PAD, TQ), BF16),
            jax.ShapeDtypeStruct((t // TQ, FOX_PAD, TQ), BF16),
            jax.ShapeDtypeStruct((t, 2 * GLA_QK), BF16),
            jax.ShapeDtypeStruct((t, GLA_WIDTH), BF16),
            jax.ShapeDtypeStruct((t, GLA_QK), F32),
            jax.ShapeDtypeStruct((t, GLA_WIDTH), BF16),
        ],
        scratch_shapes=[pltpu.VMEM((1, LANES), F32)],
        compiler_params=pltpu.CompilerParams(
            dimension_semantics=("arbitrary",), vmem_limit_bytes=VMEM_LIMIT),
        name="mix_in",
    )(x, g, wk, wqt, wvt, wgla, wgr, wff, wgg, bf, pk, pqt, wg2, bg)


def _fox_kernel(qt_ref, k_ref, vt_ref, g_ref, o_ref, m_ref, acc_ref):
    i = pl.program_id(1)
    m_ref[...] = jnp.full(m_ref.shape, -jnp.inf, F32)
    acc_ref[...] = jnp.zeros(acc_ref.shape, F32)
    key = lax.broadcasted_iota(jnp.int32, (TQ, TQ), 0)
    qry = lax.broadcasted_iota(jnp.int32, (TQ, TQ), 1)
    causal = key <= qry

    def block(j, diagonal):
        start = pl.multiple_of(j * TQ, TQ)
        slots = [slice(h * LANES, (h + 1) * LANES) for h in range(FOX_HEADS)]
        scores = [_dot(k_ref[pl.ds(start, TQ), slots[h]], qt_ref[0, slots[h], :])
                  for h in range(FOX_HEADS)]
        probs, alphas = [], []
        for h in range(FOX_HEADS):
            s = jnp.where(causal, scores[h], -jnp.inf) if diagonal else scores[h]
            m_old = m_ref[h:h + 1, :]
            m_new = jnp.maximum(m_old, jnp.max(s, axis=0, keepdims=True))
            alphas.append(jnp.exp(m_old - m_new))
            probs.append(jnp.exp(s - m_new).astype(BF16))
            m_ref[h:h + 1, :] = m_new
        for h in range(FOX_HEADS):
            acc_ref[h] = alphas[h] * acc_ref[h] + _dot(vt_ref[j, slots[h], :], probs[h])

    def body(j, carry):
        block(j, False)
        return carry

    lax.fori_loop(0, i, body, 0)
    block(i, True)

    for h in range(FOX_HEADS):
        acc = acc_ref[h]
        o = acc[:FOX_HEAD_DIM, :] / acc[V_ONE_ROW:V_ONE_ROW + 1, :]
        ms = jnp.mean(o * o, axis=0, keepdims=True)
        rows = slice(h * FOX_HEAD_DIM, (h + 1) * FOX_HEAD_DIM)
        o_ref[rows, :] = (o * lax.rsqrt(ms + RMS_EPS) * g_ref[rows, :]).astype(BF16)


def _fox(qt, k, vt, g, batch, seq):
    nq = seq // TQ
    return pl.pallas_call(
        _fox_kernel,
        grid=(batch, nq),
        in_specs=[
            pl.BlockSpec((1, FOX_PAD, TQ), lambda b, i: (b * nq + i, 0, 0)),
            pl.BlockSpec((seq, FOX_PAD), lambda b, i: (b, 0), pipeline_mode=pl.Buffered(1)),
            pl.BlockSpec((nq, FOX_PAD, TQ), lambda b, i: (b, 0, 0), pipeline_mode=pl.Buffered(1)),
            _const_spec((FOX_WIDTH, TQ)),
        ],
        out_specs=pl.BlockSpec((FOX_WIDTH, TQ), lambda b, i: (0, b * nq + i)),
        out_shape=jax.ShapeDtypeStruct((FOX_WIDTH, batch * seq), BF16),
        scratch_shapes=[pltpu.VMEM((FOX_HEADS, TQ), F32),
                        pltpu.VMEM((FOX_HEADS, LANES, TQ), F32)],
        compiler_params=pltpu.CompilerParams(
            dimension_semantics=("arbitrary", "arbitrary"), vmem_limit_bytes=VMEM_LIMIT),
        name="fox",
    )(qt, k, vt, g)


def _gla_kernel(qk_ref, v_ref, la_ref, sr_ref, g_ref, o_ref, st_ref):
    @pl.when(pl.program_id(1) == 0)
    def _():
        st_ref[...] = jnp.zeros_like(st_ref)

    lane = lax.broadcasted_iota(jnp.int32, (1, LANES), 1)
    first = lane < GLA_KEY_DIM
    row = lax.broadcasted_iota(jnp.int32, (CHUNK, CHUNK), 0)
    col = lax.broadcasted_iota(jnp.int32, (CHUNK, CHUNK), 1)
    causal = col <= row
    tri = jnp.where(causal, 1.0, 0.0).astype(BF16)
    scale = GLA_KEY_DIM ** -0.5

    def chunk(n, _):
        r0 = pl.multiple_of(n * CHUNK, CHUNK)
        rows = pl.ds(r0, CHUNK)
        b = _tri_cumsum(tri, la_ref[rows, :])
        b_last = b[CHUNK - 1:CHUNK, :]
        q = qk_ref[rows, 0:GLA_QK].astype(F32)
        k = qk_ref[rows, GLA_QK:2 * GLA_QK].astype(F32)
        q_dec = (q * scale * jnp.exp(b)).astype(BF16)
        k_inv = (k * jnp.exp(-b)).astype(BF16)
        k_rem = (k * jnp.exp(b_last - b)).astype(BF16)
        decay = jnp.exp(b_last)
        zero = jnp.zeros((CHUNK, LANES), BF16)
        for h in range(GLA_HEADS):
            pr = slice((h // 2) * LANES, (h // 2 + 1) * LANES)
            hmask = first if h % 2 == 0 else jnp.logical_not(first)
            qd = jnp.where(hmask, q_dec[:, pr], zero)
            kr = jnp.where(hmask, k_rem[:, pr], zero)
            vh = v_ref[rows, h * GLA_VAL_DIM:(h + 1) * GLA_VAL_DIM]
            att = jnp.where(causal, _dot_nt(qd, k_inv[:, pr]), 0.0)
            st = st_ref[h]
            o = _dot(att.astype(BF16), vh) + _dot_nt(qd, st.astype(BF16))
            st_ref[h] = st * decay[:, pr] + _dot_tn(vh, kr)
            ms = jnp.mean(o * o, axis=-1, keepdims=True)
            cols = slice(h * GLA_VAL_DIM, (h + 1) * GLA_VAL_DIM)
            y = o * lax.rsqrt(ms + RMS_EPS) * g_ref[:, cols] * sr_ref[rows, cols].astype(F32)
            o_ref[rows, cols] = y.astype(BF16)
        return 0

    lax.fori_loop(0, TG // CHUNK, chunk, 0)


def _gla(gqk, gv, la, sr, g, batch, seq):
    nt = seq // TG
    row = lambda w: pl.BlockSpec((TG, w), lambda b, i: (b * nt + i, 0))
    return pl.pallas_call(
        _gla_kernel,
        grid=(batch, nt),
        in_specs=[row(2 * GLA_QK), row(GLA_WIDTH), row(GLA_QK), row(GLA_WIDTH),
                  pl.BlockSpec((1, GLA_WIDTH), lambda b, i: (0, 0))],
        out_specs=row(GLA_WIDTH),
        out_shape=jax.ShapeDtypeStruct((batch * seq, GLA_WIDTH), BF16),
        scratch_shapes=[pltpu.VMEM((GLA_HEADS, GLA_VAL_DIM, LANES), F32)],
        compiler_params=pltpu.CompilerParams(
            dimension_semantics=("arbitrary", "arbitrary"), vmem_limit_bytes=VMEM_LIMIT),
        name="gla",
    )(gqk, gv, la, sr, g)


def _mem_kv_kernel(m_ref, g_ref, w_ref, o_ref):
    h = _rms(m_ref[...], g_ref[...]).astype(BF16)
    o_ref[...] = _dot(h, w_ref[...]).astype(BF16)


def _mem_kv(mem, g, w, mem_len):
    t = mem.shape[0]
    width = w.shape[1]
    return pl.pallas_call(
        _mem_kv_kernel,
        grid=(t // mem_len,),
        in_specs=[pl.BlockSpec((mem_len, D_MODEL), lambda i: (i, 0)),
                  _const_spec((1, D_MODEL)), _const_spec((D_MODEL, width))],
        out_specs=pl.BlockSpec((mem_len, width), lambda i: (i, 0)),
        out_shape=jax.ShapeDtypeStruct((t, width), BF16),
        compiler_params=pltpu.CompilerParams(
            dimension_semantics=("arbitrary",), vmem_limit_bytes=VMEM_LIMIT),
        name="mem_kv",
    )(mem, g, w)


def _mix_out_mem_kernel(x_ref, oft_ref, og_ref, kv_ref, wout_ref, gpost_ref, gpre_ref,
                        wq_ref, wo_ref, gmpost_ref, o_ref):
    y = _dot_tn(oft_ref[...], wout_ref[0:FOX_WIDTH, :]) + _dot(og_ref[...], wout_ref[FOX_WIDTH:, :])
    x = x_ref[...] + _rms(y, gpost_ref[...])
    h = _rms(x, gpre_ref[...]).astype(BF16)
    q = (_dot(h, wq_ref[...]) * (MEM_HEAD_DIM ** -0.5)).astype(BF16)
    width = MEM_HEADS * MEM_HEAD_DIM
    outs = []
    for hd in range(MEM_HEADS):
        cols = slice(hd * MEM_HEAD_DIM, (hd + 1) * MEM_HEAD_DIM)
        s = _dot_nt(q[:, cols], kv_ref[:, cols])
        m = jnp.max(s, axis=-1, keepdims=True)
        p = jnp.exp(s - m)
        p = p / jnp.sum(p, axis=-1, keepdims=True)
        vcols = slice(width + hd * MEM_HEAD_DIM, width + (hd + 1) * MEM_HEAD_DIM)
        outs.append(_dot(p.astype(BF16), kv_ref[:, vcols]).astype(BF16))
    o = jnp.concatenate(outs, axis=-1)
    y2 = _dot(o, wo_ref[...])
    o_ref[...] = x + _rms(y2, gmpost_ref[...])


def _mix_out_mem(x, o_fox_t, o_gla, kv, wout, gpost, gpre, wq, wo, gmpost, seq, mem_len):
    t = x.shape[0]
    tiles_per_seq = seq // TM
    row = lambda w: pl.BlockSpec((TM, w), lambda i: (i, 0))
    return pl.pallas_call(
        _mix_out_mem_kernel,
        grid=(t // TM,),
        in_specs=[
            row(D_MODEL),
            pl.BlockSpec((FOX_WIDTH, TM), lambda i: (0, i)),
            row(GLA_WIDTH),
            pl.BlockSpec((mem_len, kv.shape[1]), lambda i: (i // tiles_per_seq, 0)),
            _const_spec((D_MODEL, D_MODEL)), _const_spec((1, D_MODEL)), _const_spec((1, D_MODEL)),
            _const_spec((D_MODEL, D_MODEL)), _const_spec((D_MODEL, D_MODEL)),
            _const_spec((1, D_MODEL)),
        ],
        out_specs=row(D_MODEL),
        out_shape=jax.ShapeDtypeStruct((t, D_MODEL), F32),
        compiler_params=pltpu.CompilerParams(
            dimension_semantics=("arbitrary",), vmem_limit_bytes=VMEM_LIMIT),
        name="mix_out_mem",
    )(x, o_fox_t, o_gla, kv, wout, gpost, gpre, wq, wo, gmpost)


def _pad_cols(w, width):
    return jnp.pad(w, ((0, 0), (0, width - w.shape[1])))


def _head_slots(w):
    w = w.reshape(w.shape[0], FOX_HEADS, FOX_HEAD_DIM)
    w = jnp.pad(w, ((0, 0), (0, 0), (0, LANES - FOX_HEAD_DIM)))
    return w.reshape(w.shape[0], FOX_PAD)


def _placement():
    pk = np.zeros((LANES, FOX_PAD), np.float32)
    pq = np.zeros((LANES, FOX_PAD), np.float32)
    for h in range(FOX_HEADS):
        for t in range(C_TERMS):
            src = t * FOX_HEADS + h
            pq[src, h * LANES + AUG0 + t] = 1.0
            pk[ONE_LANE, h * LANES + AUG0 + t] = 1.0
            pq[ONE_LANE, h * LANES + AUG1 + t] = 1.0
            pk[src, h * LANES + AUG1 + t] = -1.0
    return jnp.asarray(pk, BF16), jnp.asarray(pq.T, BF16)


def kernel(x, mem, g_ff1_pre, w_ff1_gate, w_ff1_up, w_ff1_down, g_ff1_post, g_mix_pre, w_mix_in, b_fox_f, w_gla_g2, b_gla_g, g_fox_out, g_gla_out, w_mix_out, g_mix_post, g_mem_pre, g_mem_src, w_mem_q, w_mem_kv, w_mem_o, g_mem_post, g_ff2_pre, w_ff2_gate, w_ff2_up, w_ff2_down, g_ff2_post, g_final):
    batch, seq, d = x.shape
    mem_len = mem.shape[1]
    depth = w_ff1_gate.shape[0]
    xt = x.reshape(batch * seq, d)
    memt = mem.reshape(batch * mem_len, d)
    gfin = g_final.reshape(1, d)
    vec = lambda v: v.reshape(1, -1).astype(F32)
    pk, pqt = _placement()
    for l in range(depth):
        last = l == depth - 1
        xt = _ffn(xt, vec(g_ff1_pre[l]), w_ff1_gate[l].astype(BF16), w_ff1_up[l].astype(BF16),
                  w_ff1_down[l].astype(BF16), vec(g_ff1_post[l]), gfin, False)

        w_in = w_mix_in[l]
        wqt = _head_slots(w_in[:, OFF_FQ:OFF_FK] * (FOX_HEAD_DIM ** -0.5)).T.astype(BF16)
        wk = _head_slots(w_in[:, OFF_FK:OFF_FV]).astype(BF16)
        wvt = _head_slots(w_in[:, OFF_FV:OFF_FF]).T.astype(BF16)
        wgla = w_in[:, OFF_GQ:OFF_GG].astype(BF16)
        wgr = w_in[:, OFF_GR:IN_WIDTH].astype(BF16)
        wff = _pad_cols(w_in[:, OFF_FF:OFF_GQ], LANES).astype(BF16)
        wgg = _pad_cols(w_in[:, OFF_GG:OFF_GR], LANES).astype(BF16)
        bf = _pad_cols(vec(b_fox_f[l]), LANES)
        wg2 = jnp.pad(w_gla_g2[l].astype(F32), ((0, LANES - GLA_GATE_RANK), (0, 0)))
        k, qt, vt, gqk, gv, la, sr = _mix_in(xt, vec(g_mix_pre[l]), wk, wqt, wvt, wgla, wgr, wff,
                                             wgg, bf, pk, pqt, wg2, vec(b_gla_g[l]), seq)

        g_fox = jnp.broadcast_to(g_fox_out[l].astype(F32)[:, None], (FOX_WIDTH, TQ))
        o_fox_t = _fox(qt, k, vt, g_fox, batch, seq)
        o_gla = _gla(gqk, gv, la, sr, vec(g_gla_out[l]), batch, seq)

        kv = _mem_kv(memt, vec(g_mem_src[l]), w_mem_kv[l].astype(BF16), mem_len)
        xt = _mix_out_mem(xt, o_fox_t, o_gla, kv, w_mix_out[l].astype(BF16), vec(g_mix_post[l]),
                          vec(g_mem_pre[l]), w_mem_q[l].astype(BF16), w_mem_o[l].astype(BF16),
                          vec(g_mem_post[l]), seq, mem_len)

        xt = _ffn(xt, vec(g_ff2_pre[l]), w_ff2_gate[l].astype(BF16), w_ff2_up[l].astype(BF16),
                  w_ff2_down[l].astype(BF16), vec(g_ff2_post[l]), gfin, last)
    return xt.reshape(batch, seq, d)
```

```python
import functools

import numpy as np
import jax
import jax.numpy as jnp
from jax import lax
from jax.experimental import pallas as pl
from jax.experimental.pallas import tpu as pltpu

F32 = jnp.float32
BF16 = jnp.bfloat16

D_MODEL = 1024
CHUNK = 64
FOX_WIDTH = 512
FOX_HEADS = 8
FOX_HEAD_DIM = 64
GLA_WIDTH = 512
GLA_HEADS = 4
GLA_VAL_DIM = 128
GLA_KEY_DIM = 64
GLA_QK = 256
GLA_GATE_RANK = 16
GLA_GATE_NORM = 16.0
MEM_HEADS = 4
MEM_HEAD_DIM = 256
D_FF = 2816
RMS_EPS = 1e-6

OFF_FQ = 0
OFF_FK = OFF_FQ + FOX_WIDTH
OFF_FV = OFF_FK + FOX_WIDTH
OFF_FF = OFF_FV + FOX_WIDTH
OFF_GQ = OFF_FF + FOX_HEADS
OFF_GK = OFF_GQ + GLA_QK
OFF_GV = OFF_GK + GLA_QK
OFF_GG = OFF_GV + GLA_WIDTH
OFF_GR = OFF_GG + GLA_GATE_RANK
IN_WIDTH = OFF_GR + GLA_WIDTH

LANES = 128
VMEM_LIMIT = 56 * 1024 * 1024

TM = 512
TQ = 256
TG = 512

FOX_PAD = FOX_HEADS * LANES
C_TERMS = 3
ONE_LANE = C_TERMS * FOX_HEADS
AUG0 = FOX_HEAD_DIM
AUG1 = FOX_HEAD_DIM + C_TERMS
V_ONE_ROW = FOX_HEAD_DIM
V_ROWS = 80
LOG2E = 1.4426950408889634


def _rms(x, g):
    return x * lax.rsqrt(jnp.mean(x * x, axis=-1, keepdims=True) + RMS_EPS) * g


def _dot(a, b):
    return jnp.dot(a, b, preferred_element_type=F32)


def _dot_nt(a, b):
    return lax.dot_general(a, b, (((1,), (1,)), ((), ())), preferred_element_type=F32)


def _dot_tn(a, b):
    return lax.dot_general(a, b, (((0,), (0,)), ((), ())), preferred_element_type=F32)


def _split3(x):
    hi = x.astype(BF16)
    r1 = x - hi.astype(F32)
    mid = r1.astype(BF16)
    lo = (r1 - mid.astype(F32)).astype(BF16)
    return hi, mid, lo


def _tri_cumsum(tri_bf16, x):
    hi, mid, lo = _split3(x)
    return _dot(tri_bf16, hi) + _dot(tri_bf16, mid) + _dot(tri_bf16, lo)


def _log_sigmoid(x):
    return jnp.minimum(x, 0.0) - jnp.log1p(jnp.exp(-jnp.abs(x)))


def _silu(x):
    return x * jax.nn.sigmoid(x)


def _const_spec(shape):
    return pl.BlockSpec(shape, lambda *_: (0,) * len(shape), pipeline_mode=pl.Buffered(1))


def _ffn_kernel(x_ref, gpre_ref, wg_ref, wu_ref, wd_ref, gpost_ref, gfin_ref, o_ref, *, final):
    x = x_ref[...]
    h = _rms(x, gpre_ref[...]).astype(BF16)
    a = (_silu(_dot(h, wg_ref[...])) * _dot(h, wu_ref[...])).astype(BF16)
    y = _dot(a, wd_ref[...])
    out = x + 0.5 * _rms(y, gpost_ref[...])
    if final:
        out = _rms(out, gfin_ref[...])
    o_ref[...] = out


def _ffn(x, gpre, wg, wu, wd, gpost, gfin, final):
    t = x.shape[0]
    return pl.pallas_call(
        functools.partial(_ffn_kernel, final=final),
        grid=(t // TM,),
        in_specs=[
            pl.BlockSpec((TM, D_MODEL), lambda i: (i, 0)),
            _const_spec((1, D_MODEL)),
            _const_spec((D_MODEL, D_FF)),
            _const_spec((D_MODEL, D_FF)),
            _const_spec((D_FF, D_MODEL)),
            _const_spec((1, D_MODEL)),
            _const_spec((1, D_MODEL)),
        ],
        out_specs=pl.BlockSpec((TM, D_MODEL), lambda i: (i, 0)),
        out_shape=jax.ShapeDtypeStruct((t, D_MODEL), F32),
        compiler_params=pltpu.CompilerParams(
            dimension_semantics=("arbitrary",), vmem_limit_bytes=VMEM_LIMIT),
        name="ffn_final" if final else "ffn",
    )(x, gpre, wg, wu, wd, gpost, gfin)


def _mix_in_kernel(x_ref, g_ref, wk_ref, wqt_ref, wvt_ref, wgla_ref, wgr_ref, wff_ref, wgg_ref,
                   bf_ref, pk_ref, pqt_ref, wg2_ref, bg_ref,
                   k_ref, qt_ref, vt_ref, gqk_ref, gv_ref, la_ref, sr_ref,
                   carry_ref, *, tiles_per_seq):
    i = pl.program_id(0)

    @pl.when(i % tiles_per_seq == 0)
    def _():
        carry_ref[...] = jnp.zeros_like(carry_ref)

    h = _rms(x_ref[...], g_ref[...]).astype(BF16)
    gl = _dot(h, wgla_ref[...])
    gqk_ref[...] = gl[:, :2 * GLA_QK].astype(BF16)
    gv_ref[...] = gl[:, 2 * GLA_QK:].astype(BF16)
    sr_ref[...] = _silu(_dot(h, wgr_ref[...])).astype(BF16)

    lane = lax.broadcasted_iota(jnp.int32, (1, LANES), 1)
    lf = jnp.where(lane < FOX_HEADS, _log_sigmoid(_dot(h, wff_ref[...]) + bf_ref[...]), 0.0)
    row = lax.broadcasted_iota(jnp.int32, (TM, TM), 0)
    col = lax.broadcasted_iota(jnp.int32, (TM, TM), 1)
    tri = jnp.where(col <= row, 1.0, 0.0).astype(BF16)
    c = _tri_cumsum(tri, lf) + carry_ref[...]
    carry_ref[...] = c[TM - 1:TM, :]
    c_hi, c_mid, c_lo = _split3(c * LOG2E)
    packed = (c_hi.astype(F32) + pltpu.roll(c_mid.astype(F32), FOX_HEADS, 1)
              + pltpu.roll(c_lo.astype(F32), 2 * FOX_HEADS, 1)
              + jnp.where(lane == ONE_LANE, 1.0, 0.0)).astype(BF16)

    k_ref[...] = (_dot(h, wk_ref[...]) + _dot(packed, pk_ref[...])).astype(BF16)
    qt = (_dot_nt(wqt_ref[...], h) + _dot_nt(pqt_ref[...], packed)).astype(BF16)
    vrow = lax.broadcasted_iota(jnp.int32, (FOX_PAD, 1), 0) % LANES
    vt = jnp.where(vrow == V_ONE_ROW, 1.0, _dot_nt(wvt_ref[...], h)).astype(BF16)
    for n in range(TM // TQ):
        qt_ref[n] = qt[:, n * TQ:(n + 1) * TQ]
        vt_ref[n] = vt[:, n * TQ:(n + 1) * TQ]

    gg = _dot(h, wgg_ref[...])
    g_hi, g_mid, _ = _split3(gg)
    w_hi, w_mid, _ = _split3(wg2_ref[...])
    gate = _dot(g_hi, w_hi) + _dot(g_mid, w_hi) + _dot(g_hi, w_mid) + bg_ref[...]
    la_ref[...] = _log_sigmoid(gate) / GLA_GATE_NORM


def _mix_in(x, g, wk, wqt, wvt, wgla, wgr, wff, wgg, bf, pk, pqt, wg2, bg, seq):
    t = x.shape[0]
    row = lambda w: pl.BlockSpec((TM, w), lambda i: (i, 0))
    tblk = pl.BlockSpec((TM // TQ, FOX_PAD, TQ), lambda i: (i, 0, 0))
    return pl.pallas_call(
        functools.partial(_mix_in_kernel, tiles_per_seq=seq // TM),
        grid=(t // TM,),
        in_specs=[
            row(D_MODEL),
            _const_spec((1, D_MODEL)),
            _const_spec((D_MODEL, FOX_PAD)),
            _const_spec((FOX_PAD, D_MODEL)),
            _const_spec((FOX_PAD, D_MODEL)),
            _const_spec((D_MODEL, 2 * GLA_QK + GLA_WIDTH)),
            _const_spec((D_MODEL, GLA_WIDTH)),
            _const_spec((D_MODEL, LANES)),
            _const_spec((D_MODEL, LANES)),
            _const_spec((1, LANES)),
            _const_spec((LANES, FOX_PAD)),
            _const_spec((FOX_PAD, LANES)),
            _const_spec((LANES, GLA_QK)),
            _const_spec((1, GLA_QK)),
        ],
        out_specs=[row(FOX_PAD), tblk, tblk, row(2 * GLA_QK), row(GLA_WIDTH),
                   row(GLA_QK), row(GLA_WIDTH)],
        out_shape=[
            jax.ShapeDtypeStruct((t, FOX_PAD), BF16),
            jax.ShapeDtypeStruct((t // TQ, FOX_PAD, TQ), BF16),
            jax.ShapeDtypeStruct((t // TQ, FOX_PAD, TQ), BF16),
            jax.ShapeDtypeStruct((t, 2 * GLA_QK), BF16),
            jax.ShapeDtypeStruct((t, GLA_WIDTH), BF16),
            jax.ShapeDtypeStruct((t, GLA_QK), F32),
            jax.ShapeDtypeStruct((t, GLA_WIDTH), BF16),
        ],
        scratch_shapes=[pltpu.VMEM((1, LANES), F32)],
        compiler_params=pltpu.CompilerParams(
            dimension_semantics=("arbitrary",), vmem_limit_bytes=VMEM_LIMIT),
        name="mix_in",
    )(x, g, wk, wqt, wvt, wgla, wgr, wff, wgg, bf, pk, pqt, wg2, bg)


def _fox_kernel(qt_ref, k_ref, vt_ref, g_ref, o_ref, m_ref, acc_ref, sa_ref, sb_ref):
    i = pl.program_id(1)
    m_ref[...] = jnp.full(m_ref.shape, -jnp.inf, F32)
    acc_ref[...] = jnp.zeros(acc_ref.shape, F32)
    key = lax.broadcasted_iota(jnp.int32, (TQ, TQ), 0)
    qry = lax.broadcasted_iota(jnp.int32, (TQ, TQ), 1)
    causal = key <= qry
    slots = [slice(h * LANES, (h + 1) * LANES) for h in range(FOX_HEADS)]
    vrows = [slice(h * LANES, h * LANES + V_ROWS) for h in range(FOX_HEADS)]

    def scores(j, s_ref):
        start = pl.multiple_of(j * TQ, TQ)
        for h in range(FOX_HEADS):
            s_ref[h] = _dot(k_ref[pl.ds(start, TQ), slots[h]], qt_ref[0, slots[h], :])

    def accumulate(j, s_ref, diagonal):
        probs, alphas = [], []
        for h in range(FOX_HEADS):
            s = s_ref[h]
            if diagonal:
                s = jnp.where(causal, s, -jnp.inf)
            m_old = m_ref[h:h + 1, :]
            m_new = jnp.maximum(m_old, jnp.max(s, axis=0, keepdims=True))
            alphas.append(jnp.exp2(m_old - m_new))
            probs.append(jnp.exp2(s - m_new).astype(BF16))
            m_ref[h:h + 1, :] = m_new
        for h in range(FOX_HEADS):
            acc_ref[h] = alphas[h] * acc_ref[h] + _dot(vt_ref[j, vrows[h], :], probs[h])

    scores(0, sa_ref)

    def pair(t, carry):
        j = 2 * t
        scores(j + 1, sb_ref)
        accumulate(j, sa_ref, False)
        scores(j + 2, sa_ref)
        accumulate(j + 1, sb_ref, False)
        return carry

    lax.fori_loop(0, i // 2, pair, 0)

    @pl.when(i % 2 == 0)
    def _():
        accumulate(i, sa_ref, True)

    @pl.when(i % 2 == 1)
    def _():
        scores(i, sb_ref)
        accumulate(i - 1, sa_ref, False)
        accumulate(i, sb_ref, True)

    for h in range(FOX_HEADS):
        acc = acc_ref[h]
        o = acc[:FOX_HEAD_DIM, :] / acc[V_ONE_ROW:V_ONE_ROW + 1, :]
        ms = jnp.mean(o * o, axis=0, keepdims=True)
        rows = slice(h * FOX_HEAD_DIM, (h + 1) * FOX_HEAD_DIM)
        o_ref[rows, :] = (o * lax.rsqrt(ms + RMS_EPS) * g_ref[rows, :]).astype(BF16)


def _fox(qt, k, vt, g, batch, seq):
    nq = seq // TQ
    return pl.pallas_call(
        _fox_kernel,
        grid=(batch, nq),
        in_specs=[
            pl.BlockSpec((1, FOX_PAD, TQ), lambda b, i: (b * nq + i, 0, 0)),
            pl.BlockSpec((seq, FOX_PAD), lambda b, i: (b, 0), pipeline_mode=pl.Buffered(1)),
            pl.BlockSpec((nq, FOX_PAD, TQ), lambda b, i: (b, 0, 0), pipeline_mode=pl.Buffered(1)),
            _const_spec((FOX_WIDTH, TQ)),
        ],
        out_specs=pl.BlockSpec((FOX_WIDTH, TQ), lambda b, i: (0, b * nq + i)),
        out_shape=jax.ShapeDtypeStruct((FOX_WIDTH, batch * seq), BF16),
        scratch_shapes=[pltpu.VMEM((FOX_HEADS, TQ), F32),
                        pltpu.VMEM((FOX_HEADS, V_ROWS, TQ), F32),
                        pltpu.VMEM((FOX_HEADS, TQ, TQ), F32),
                        pltpu.VMEM((FOX_HEADS, TQ, TQ), F32)],
        compiler_params=pltpu.CompilerParams(
            dimension_semantics=("arbitrary", "arbitrary"), vmem_limit_bytes=VMEM_LIMIT),
        name="fox",
    )(qt, k, vt, g)


def _gla_kernel(qk_ref, v_ref, la_ref, sr_ref, g_ref, o_ref, st_ref):
    @pl.when(pl.program_id(1) == 0)
    def _():
        st_ref[...] = jnp.zeros_like(st_ref)

    lane = lax.broadcasted_iota(jnp.int32, (1, LANES), 1)
    first = lane < GLA_KEY_DIM
    row = lax.broadcasted_iota(jnp.int32, (CHUNK, CHUNK), 0)
    col = lax.broadcasted_iota(jnp.int32, (CHUNK, CHUNK), 1)
    causal = col <= row
    tri = jnp.where(causal, 1.0, 0.0).astype(BF16)
    scale = GLA_KEY_DIM ** -0.5

    def chunk(n, _):
        r0 = pl.multiple_of(n * CHUNK, CHUNK)
        rows = pl.ds(r0, CHUNK)
        b = _tri_cumsum(tri, la_ref[rows, :])
        b_last = b[CHUNK - 1:CHUNK, :]
        q = qk_ref[rows, 0:GLA_QK].astype(F32)
        k = qk_ref[rows, GLA_QK:2 * GLA_QK].astype(F32)
        q_dec = (q * scale * jnp.exp(b)).astype(BF16)
        k_inv = (k * jnp.exp(-b)).astype(BF16)
        k_rem = (k * jnp.exp(b_last - b)).astype(BF16)
        decay = jnp.exp(b_last)
        zero = jnp.zeros((CHUNK, LANES), BF16)
        for h in range(GLA_HEADS):
            pr = slice((h // 2) * LANES, (h // 2 + 1) * LANES)
            hmask = first if h % 2 == 0 else jnp.logical_not(first)
            qd = jnp.where(hmask, q_dec[:, pr], zero)
            kr = jnp.where(hmask, k_rem[:, pr], zero)
            vh = v_ref[rows, h * GLA_VAL_DIM:(h + 1) * GLA_VAL_DIM]
            att = jnp.where(causal, _dot_nt(qd, k_inv[:, pr]), 0.0)
            st = st_ref[h]
            o = _dot(att.astype(BF16), vh) + _dot_nt(qd, st.astype(BF16))
            st_ref[h] = st * decay[:, pr] + _dot_tn(vh, kr)
            ms = jnp.mean(o * o, axis=-1, keepdims=True)
            cols = slice(h * GLA_VAL_DIM, (h + 1) * GLA_VAL_DIM)
            y = o * lax.rsqrt(ms + RMS_EPS) * g_ref[:, cols] * sr_ref[rows, cols].astype(F32)
            o_ref[rows, cols] = y.astype(BF16)
        return 0

    lax.fori_loop(0, TG // CHUNK, chunk, 0)


def _gla(gqk, gv, la, sr, g, batch, seq):
    nt = seq // TG
    row = lambda w: pl.BlockSpec((TG, w), lambda b, i: (b * nt + i, 0))
    return pl.pallas_call(
        _gla_kernel,
        grid=(batch, nt),
        in_specs=[row(2 * GLA_QK), row(GLA_WIDTH), row(GLA_QK), row(GLA_WIDTH),
                  pl.BlockSpec((1, GLA_WIDTH), lambda b, i: (0, 0))],
        out_specs=row(GLA_WIDTH),
        out_shape=jax.ShapeDtypeStruct((batch * seq, GLA_WIDTH), BF16),
        scratch_shapes=[pltpu.VMEM((GLA_HEADS, GLA_VAL_DIM, LANES), F32)],
        compiler_params=pltpu.CompilerParams(
            dimension_semantics=("arbitrary", "arbitrary"), vmem_limit_bytes=VMEM_LIMIT),
        name="gla",
    )(gqk, gv, la, sr, g)


def _mem_kv_kernel(m_ref, g_ref, w_ref, o_ref):
    h = _rms(m_ref[...], g_ref[...]).astype(BF16)
    o_ref[...] = _dot(h, w_ref[...]).astype(BF16)


def _mem_kv(mem, g, w, mem_len):
    t = mem.shape[0]
    width = w.shape[1]
    return pl.pallas_call(
        _mem_kv_kernel,
        grid=(t // mem_len,),
        in_specs=[pl.BlockSpec((mem_len, D_MODEL), lambda i: (i, 0)),
                  _const_spec((1, D_MODEL)), _const_spec((D_MODEL, width))],
        out_specs=pl.BlockSpec((mem_len, width), lambda i: (i, 0)),
        out_shape=jax.ShapeDtypeStruct((t, width), BF16),
        compiler_params=pltpu.CompilerParams(
            dimension_semantics=("arbitrary",), vmem_limit_bytes=VMEM_LIMIT),
        name="mem_kv",
    )(mem, g, w)


def _mix_out_mem_kernel(x_ref, oft_ref, og_ref, kv_ref, wout_ref, gpost_ref, gpre_ref,
                        wq_ref, wo_ref, gmpost_ref, o_ref):
    y = _dot_tn(oft_ref[...], wout_ref[0:FOX_WIDTH, :]) + _dot(og_ref[...], wout_ref[FOX_WIDTH:, :])
    x = x_ref[...] + _rms(y, gpost_ref[...])
    h = _rms(x, gpre_ref[...]).astype(BF16)
    q = (_dot(h, wq_ref[...]) * (MEM_HEAD_DIM ** -0.5)).astype(BF16)
    width = MEM_HEADS * MEM_HEAD_DIM
    outs = []
    for hd in range(MEM_HEADS):
        cols = slice(hd * MEM_HEAD_DIM, (hd + 1) * MEM_HEAD_DIM)
        s = _dot_nt(q[:, cols], kv_ref[:, cols])
        m = jnp.max(s, axis=-1, keepdims=True)
        p = jnp.exp(s - m)
        p = p / jnp.sum(p, axis=-1, keepdims=True)
        vcols = slice(width + hd * MEM_HEAD_DIM, width + (hd + 1) * MEM_HEAD_DIM)
        outs.append(_dot(p.astype(BF16), kv_ref[:, vcols]).astype(BF16))
    o = jnp.concatenate(outs, axis=-1)
    y2 = _dot(o, wo_ref[...])
    o_ref[...] = x + _rms(y2, gmpost_ref[...])


def _mix_out_mem(x, o_fox_t, o_gla, kv, wout, gpost, gpre, wq, wo, gmpost, seq, mem_len):
    t = x.shape[0]
    tiles_per_seq = seq // TM
    row = lambda w: pl.BlockSpec((TM, w), lambda i: (i, 0))
    return pl.pallas_call(
        _mix_out_mem_kernel,
        grid=(t // TM,),
        in_specs=[
            row(D_MODEL),
            pl.BlockSpec((FOX_WIDTH, TM), lambda i: (0, i)),
            row(GLA_WIDTH),
            pl.BlockSpec((mem_len, kv.shape[1]), lambda i: (i // tiles_per_seq, 0)),
            _const_spec((D_MODEL, D_MODEL)), _const_spec((1, D_MODEL)), _const_spec((1, D_MODEL)),
            _const_spec((D_MODEL, D_MODEL)), _const_spec((D_MODEL, D_MODEL)),
            _const_spec((1, D_MODEL)),
        ],
        out_specs=row(D_MODEL),
        out_shape=jax.ShapeDtypeStruct((t, D_MODEL), F32),
        compiler_params=pltpu.CompilerParams(
            dimension_semantics=("arbitrary",), vmem_limit_bytes=VMEM_LIMIT),
        name="mix_out_mem",
    )(x, o_fox_t, o_gla, kv, wout, gpost, gpre, wq, wo, gmpost)


def _pad_cols(w, width):
    return jnp.pad(w, ((0, 0), (0, width - w.shape[1])))


def _head_slots(w):
    w = w.reshape(w.shape[0], FOX_HEADS, FOX_HEAD_DIM)
    w = jnp.pad(w, ((0, 0), (0, 0), (0, LANES - FOX_HEAD_DIM)))
    return w.reshape(w.shape[0], FOX_PAD)


def _placement():
    pk = np.zeros((LANES, FOX_PAD), np.float32)
    pq = np.zeros((LANES, FOX_PAD), np.float32)
    for h in range(FOX_HEADS):
        for t in range(C_TERMS):
            src = t * FOX_HEADS + h
            pq[src, h * LANES + AUG0 + t] = 1.0
            pk[ONE_LANE, h * LANES + AUG0 + t] = 1.0
            pq[ONE_LANE, h * LANES + AUG1 + t] = 1.0
            pk[src, h * LANES + AUG1 + t] = -1.0
    return jnp.asarray(pk, BF16), jnp.asarray(pq.T, BF16)


def kernel(x, mem, g_ff1_pre, w_ff1_gate, w_ff1_up, w_ff1_down, g_ff1_post, g_mix_pre, w_mix_in, b_fox_f, w_gla_g2, b_gla_g, g_fox_out, g_gla_out, w_mix_out, g_mix_post, g_mem_pre, g_mem_src, w_mem_q, w_mem_kv, w_mem_o, g_mem_post, g_ff2_pre, w_ff2_gate, w_ff2_up, w_ff2_down, g_ff2_post, g_final):
    batch, seq, d = x.shape
    mem_len = mem.shape[1]
    depth = w_ff1_gate.shape[0]
    xt = x.reshape(batch * seq, d)
    memt = mem.reshape(batch * mem_len, d)
    gfin = g_final.reshape(1, d)
    vec = lambda v: v.reshape(1, -1).astype(F32)
    pk, pqt = _placement()
    for l in range(depth):
        last = l == depth - 1
        xt = _ffn(xt, vec(g_ff1_pre[l]), w_ff1_gate[l].astype(BF16), w_ff1_up[l].astype(BF16),
                  w_ff1_down[l].astype(BF16), vec(g_ff1_post[l]), gfin, False)

        w_in = w_mix_in[l]
        wqt = _head_slots(w_in[:, OFF_FQ:OFF_FK] * (LOG2E * FOX_HEAD_DIM ** -0.5)).T.astype(BF16)
        wk = _head_slots(w_in[:, OFF_FK:OFF_FV]).astype(BF16)
        wvt = _head_slots(w_in[:, OFF_FV:OFF_FF]).T.astype(BF16)
        wgla = w_in[:, OFF_GQ:OFF_GG].astype(BF16)
        wgr = w_in[:, OFF_GR:IN_WIDTH].astype(BF16)
        wff = _pad_cols(w_in[:, OFF_FF:OFF_GQ], LANES).astype(BF16)
        wgg = _pad_cols(w_in[:, OFF_GG:OFF_GR], LANES).astype(BF16)
        bf = _pad_cols(vec(b_fox_f[l]), LANES)
        wg2 = jnp.pad(w_gla_g2[l].astype(F32), ((0, LANES - GLA_GATE_RANK), (0, 0)))
        k, qt, vt, gqk, gv, la, sr = _mix_in(xt, vec(g_mix_pre[l]), wk, wqt, wvt, wgla, wgr, wff,
                                             wgg, bf, pk, pqt, wg2, vec(b_gla_g[l]), seq)

        g_fox = jnp.broadcast_to(g_fox_out[l].astype(F32)[:, None], (FOX_WIDTH, TQ))
        o_fox_t = _fox(qt, k, vt, g_fox, batch, seq)
        o_gla = _gla(gqk, gv, la, sr, vec(g_gla_out[l]), batch, seq)

        kv = _mem_kv(memt, vec(g_mem_src[l]), w_mem_kv[l].astype(BF16), mem_len)
        xt = _mix_out_mem(xt, o_fox_t, o_gla, kv, w_mix_out[l].astype(BF16), vec(g_mix_post[l]),
                          vec(g_mem_pre[l]), w_mem_q[l].astype(BF16), w_mem_o[l].astype(BF16),
                          vec(g_mem_post[l]), seq, mem_len)

        xt = _ffn(xt, vec(g_ff2_pre[l]), w_ff2_gate[l].astype(BF16), w_ff2_up[l].astype(BF16),
                  w_ff2_down[l].astype(BF16), vec(g_ff2_post[l]), gfin, last)
    return xt.reshape(batch, seq, d)
```

```python
import functools

import numpy as np
import jax
import jax.numpy as jnp
from jax import lax
from jax.experimental import pallas as pl
from jax.experimental.pallas import tpu as pltpu

F32 = jnp.float32
BF16 = jnp.bfloat16

D_MODEL = 1024
CHUNK = 64
FOX_WIDTH = 512
FOX_HEADS = 8
FOX_HEAD_DIM = 64
GLA_WIDTH = 512
GLA_HEADS = 4
GLA_VAL_DIM = 128
GLA_KEY_DIM = 64
GLA_QK = 256
GLA_GATE_RANK = 16
GLA_GATE_NORM = 16.0
MEM_HEADS = 4
MEM_HEAD_DIM = 256
D_FF = 2816
RMS_EPS = 1e-6

OFF_FQ = 0
OFF_FK = OFF_FQ + FOX_WIDTH
OFF_FV = OFF_FK + FOX_WIDTH
OFF_FF = OFF_FV + FOX_WIDTH
OFF_GQ = OFF_FF + FOX_HEADS
OFF_GK = OFF_GQ + GLA_QK
OFF_GV = OFF_GK + GLA_QK
OFF_GG = OFF_GV + GLA_WIDTH
OFF_GR = OFF_GG + GLA_GATE_RANK
IN_WIDTH = OFF_GR + GLA_WIDTH

LANES = 128
VMEM_LIMIT = 56 * 1024 * 1024

TM = 512
TQ = 256
TG = 512

FOX_PAD = FOX_HEADS * LANES
C_TERMS = 3
ONE_LANE = C_TERMS * FOX_HEADS
AUG_ROWS = 16
V_ONE_ROW = FOX_HEAD_DIM
V_ROWS = FOX_HEAD_DIM + AUG_ROWS
LOG2E = 1.4426950408889634


def _rms(x, g):
    return x * lax.rsqrt(jnp.mean(x * x, axis=-1, keepdims=True) + RMS_EPS) * g


def _dot(a, b):
    return jnp.dot(a, b, preferred_element_type=F32)


def _dot_nt(a, b):
    return lax.dot_general(a, b, (((1,), (1,)), ((), ())), preferred_element_type=F32)


def _dot_tn(a, b):
    return lax.dot_general(a, b, (((0,), (0,)), ((), ())), preferred_element_type=F32)


def _split3(x):
    hi = x.astype(BF16)
    r1 = x - hi.astype(F32)
    mid = r1.astype(BF16)
    lo = (r1 - mid.astype(F32)).astype(BF16)
    return hi, mid, lo


def _tri_cumsum(tri_bf16, x):
    hi, mid, lo = _split3(x)
    return _dot(tri_bf16, hi) + _dot(tri_bf16, mid) + _dot(tri_bf16, lo)


def _log_sigmoid(x):
    return jnp.minimum(x, 0.0) - jnp.log1p(jnp.exp(-jnp.abs(x)))


def _silu(x):
    return x * jax.nn.sigmoid(x)


def _const_spec(shape):
    return pl.BlockSpec(shape, lambda *_: (0,) * len(shape), pipeline_mode=pl.Buffered(1))


def _ffn_kernel(x_ref, gpre_ref, wg_ref, wu_ref, wd_ref, gpost_ref, gfin_ref, o_ref, *, final):
    x = x_ref[...]
    h = _rms(x, gpre_ref[...]).astype(BF16)
    a = (_silu(_dot(h, wg_ref[...])) * _dot(h, wu_ref[...])).astype(BF16)
    y = _dot(a, wd_ref[...])
    out = x + 0.5 * _rms(y, gpost_ref[...])
    if final:
        out = _rms(out, gfin_ref[...])
    o_ref[...] = out


def _ffn(x, gpre, wg, wu, wd, gpost, gfin, final):
    t = x.shape[0]
    return pl.pallas_call(
        functools.partial(_ffn_kernel, final=final),
        grid=(t // TM,),
        in_specs=[
            pl.BlockSpec((TM, D_MODEL), lambda i: (i, 0)),
            _const_spec((1, D_MODEL)),
            _const_spec((D_MODEL, D_FF)),
            _const_spec((D_MODEL, D_FF)),
            _const_spec((D_FF, D_MODEL)),
            _const_spec((1, D_MODEL)),
            _const_spec((1, D_MODEL)),
        ],
        out_specs=pl.BlockSpec((TM, D_MODEL), lambda i: (i, 0)),
        out_shape=jax.ShapeDtypeStruct((t, D_MODEL), F32),
        compiler_params=pltpu.CompilerParams(
            dimension_semantics=("arbitrary",), vmem_limit_bytes=VMEM_LIMIT),
        name="ffn_final" if final else "ffn",
    )(x, gpre, wg, wu, wd, gpost, gfin)


def _mix_in_kernel(x_ref, g_ref, wk_ref, wqt_ref, wvt_ref, wgla_ref, wgr_ref, wsm_ref,
                   bf_ref, pk_ref, pqt_ref, wg2_ref, bg_ref,
                   k_ref, qt_ref, vt_ref, gqk_ref, gv_ref, la_ref, sr_ref,
                   carry_ref, *, tiles_per_seq):
    i = pl.program_id(0)

    @pl.when(i % tiles_per_seq == 0)
    def _():
        carry_ref[...] = jnp.zeros_like(carry_ref)

    h = _rms(x_ref[...], g_ref[...]).astype(BF16)
    gl = _dot(h, wgla_ref[...])
    gqk_ref[...] = gl[:, :2 * GLA_QK].astype(BF16)
    gv_ref[...] = gl[:, 2 * GLA_QK:].astype(BF16)
    sr_ref[...] = _silu(_dot(h, wgr_ref[...])).astype(BF16)

    sm = _dot(h, wsm_ref[...])

    lane = lax.broadcasted_iota(jnp.int32, (1, LANES), 1)
    lf = jnp.where(lane < FOX_HEADS, _log_sigmoid(sm + bf_ref[...]), 0.0)
    row = lax.broadcasted_iota(jnp.int32, (TM, TM), 0)
    col = lax.broadcasted_iota(jnp.int32, (TM, TM), 1)
    tri = jnp.where(col <= row, 1.0, 0.0).astype(BF16)
    c = _tri_cumsum(tri, lf) + carry_ref[...]
    carry_ref[...] = c[TM - 1:TM, :]
    c_hi, c_mid, c_lo = _split3(c * LOG2E)
    packed = (c_hi.astype(F32) + pltpu.roll(c_mid.astype(F32), FOX_HEADS, 1)
              + pltpu.roll(c_lo.astype(F32), 2 * FOX_HEADS, 1)
              + jnp.where(lane == ONE_LANE, 1.0, 0.0)).astype(BF16)

    kp = _dot(h, wk_ref[...])
    kaug = _dot(packed, pk_ref[...])
    low = lane < FOX_HEAD_DIM
    for hd in range(FOX_HEADS):
        pair = slice((hd // 2) * LANES, (hd // 2 + 1) * LANES)
        slot = slice(hd * LANES, (hd + 1) * LANES)
        own = low if hd % 2 == 0 else jnp.logical_not(low)
        k_ref[:, slot] = jnp.where(own, kp[:, pair], kaug[:, slot]).astype(BF16)

    qt = _dot_nt(wqt_ref[...], h).astype(BF16)
    aug = _dot_nt(pqt_ref[...], packed).astype(BF16)
    vt = _dot_nt(wvt_ref[...], h).astype(BF16)
    one_rows = jnp.where(lax.broadcasted_iota(jnp.int32, (AUG_ROWS, TQ), 0) == 0,
                         1.0, 0.0).astype(BF16)
    zeros = jnp.zeros((LANES - FOX_HEAD_DIM - AUG_ROWS, TQ), BF16)
    for n in range(TM // TQ):
        cs = slice(n * TQ, (n + 1) * TQ)
        for hd in range(FOX_HEADS):
            q_h = qt[hd * FOX_HEAD_DIM:(hd + 1) * FOX_HEAD_DIM, cs]
            a_h = aug[hd * AUG_ROWS:(hd + 1) * AUG_ROWS, cs]
            base = hd * LANES
            pieces = (q_h, a_h, zeros) if hd % 2 == 0 else (a_h, zeros, q_h)
            r = base
            for piece in pieces:
                qt_ref[n, r:r + piece.shape[0], :] = piece
                r += piece.shape[0]
            vb = hd * V_ROWS
            vt_ref[n, vb:vb + FOX_HEAD_DIM, :] = vt[hd * FOX_HEAD_DIM:(hd + 1) * FOX_HEAD_DIM, cs]
            vt_ref[n, vb + FOX_HEAD_DIM:vb + V_ROWS, :] = one_rows

    g_hi, g_mid, _ = _split3(sm)
    w_hi, w_mid, _ = _split3(wg2_ref[...])
    gate = _dot(g_hi, w_hi) + _dot(g_mid, w_hi) + _dot(g_hi, w_mid) + bg_ref[...]
    la_ref[...] = _log_sigmoid(gate) / GLA_GATE_NORM


def _mix_in(x, g, wk, wqt, wvt, wgla, wgr, wsm, bf, pk, pqt, wg2, bg, seq):
    t = x.shape[0]
    row = lambda w: pl.BlockSpec((TM, w), lambda i: (i, 0))
    tblk = lambda rows: pl.BlockSpec((TM // TQ, rows, TQ), lambda i: (i, 0, 0))
    return pl.pallas_call(
        functools.partial(_mix_in_kernel, tiles_per_seq=seq // TM),
        grid=(t // TM,),
        in_specs=[
            row(D_MODEL),
            _const_spec((1, D_MODEL)),
            _const_spec((D_MODEL, FOX_WIDTH)),
            _const_spec((FOX_WIDTH, D_MODEL)),
            _const_spec((FOX_WIDTH, D_MODEL)),
            _const_spec((D_MODEL, 2 * GLA_QK + GLA_WIDTH)),
            _const_spec((D_MODEL, GLA_WIDTH)),
            _const_spec((D_MODEL, LANES)),
            _const_spec((1, LANES)),
            _const_spec((LANES, FOX_PAD)),
            _const_spec((FOX_HEADS * AUG_ROWS, LANES)),
            _const_spec((LANES, GLA_QK)),
            _const_spec((1, GLA_QK)),
        ],
        out_specs=[row(FOX_PAD), tblk(FOX_PAD), tblk(FOX_HEADS * V_ROWS), row(2 * GLA_QK),
                   row(GLA_WIDTH), row(GLA_QK), row(GLA_WIDTH)],
        out_shape=[
            jax.ShapeDtypeStruct((t, FOX_PAD), BF16),
            jax.ShapeDtypeStruct((t // TQ, FOX_PAD, TQ), BF16),
            jax.ShapeDtypeStruct((t // TQ, FOX_HEADS * V_ROWS, TQ), BF16),
            jax.ShapeDtypeStruct((t, 2 * GLA_QK), BF16),
            jax.ShapeDtypeStruct((t, GLA_WIDTH), BF16),
            jax.ShapeDtypeStruct((t, GLA_QK), F32),
            jax.ShapeDtypeStruct((t, GLA_WIDTH), BF16),
        ],
        scratch_shapes=[pltpu.VMEM((1, LANES), F32)],
        compiler_params=pltpu.CompilerParams(
            dimension_semantics=("arbitrary",), vmem_limit_bytes=VMEM_LIMIT),
        name="mix_in",
    )(x, g, wk, wqt, wvt, wgla, wgr, wsm, bf, pk, pqt, wg2, bg)


def _fox_kernel(qt_ref, k_ref, vt_ref, g_ref, o_ref, m_ref, acc_ref, sa_ref, sb_ref):
    i = pl.program_id(1)
    m_ref[...] = jnp.full(m_ref.shape, -jnp.inf, F32)
    acc_ref[...] = jnp.zeros(acc_ref.shape, F32)
    key = lax.broadcasted_iota(jnp.int32, (TQ, TQ), 0)
    qry = lax.broadcasted_iota(jnp.int32, (TQ, TQ), 1)
    causal = key <= qry
    slots = [slice(h * LANES, (h + 1) * LANES) for h in range(FOX_HEADS)]
    vrows = [slice(h * V_ROWS, (h + 1) * V_ROWS) for h in range(FOX_HEADS)]

    def scores(j, s_ref):
        start = pl.multiple_of(j * TQ, TQ)
        for h in range(FOX_HEADS):
            s_ref[h] = _dot(k_ref[pl.ds(start, TQ), slots[h]], qt_ref[0, slots[h], :])

    def accumulate(j, s_ref, diagonal):
        probs, alphas = [], []
        for h in range(FOX_HEADS):
            s = s_ref[h]
            if diagonal:
                s = jnp.where(causal, s, -jnp.inf)
            m_old = m_ref[h:h + 1, :]
            m_new = jnp.maximum(m_old, jnp.max(s, axis=0, keepdims=True))
            alphas.append(jnp.exp2(m_old - m_new))
            probs.append(jnp.exp2(s - m_new).astype(BF16))
            m_ref[h:h + 1, :] = m_new
        for h in range(FOX_HEADS):
            acc_ref[h] = alphas[h] * acc_ref[h] + _dot(vt_ref[j, vrows[h], :], probs[h])

    scores(0, sa_ref)

    def pair(t, carry):
        j = 2 * t
        scores(j + 1, sb_ref)
        accumulate(j, sa_ref, False)
        scores(j + 2, sa_ref)
        accumulate(j + 1, sb_ref, False)
        return carry

    lax.fori_loop(0, i // 2, pair, 0)

    @pl.when(i % 2 == 0)
    def _():
        accumulate(i, sa_ref, True)

    @pl.when(i % 2 == 1)
    def _():
        scores(i, sb_ref)
        accumulate(i - 1, sa_ref, False)
        accumulate(i, sb_ref, True)

    for h in range(FOX_HEADS):
        acc = acc_ref[h]
        o = acc[:FOX_HEAD_DIM, :] / acc[V_ONE_ROW:V_ONE_ROW + 1, :]
        ms = jnp.mean(o * o, axis=0, keepdims=True)
        rows = slice(h * FOX_HEAD_DIM, (h + 1) * FOX_HEAD_DIM)
        o_ref[rows, :] = (o * lax.rsqrt(ms + RMS_EPS) * g_ref[rows, :]).astype(BF16)


def _fox(qt, k, vt, g, batch, seq):
    nq = seq // TQ
    return pl.pallas_call(
        _fox_kernel,
        grid=(batch, nq),
        in_specs=[
            pl.BlockSpec((1, FOX_PAD, TQ), lambda b, i: (b * nq + i, 0, 0)),
            pl.BlockSpec((seq, FOX_PAD), lambda b, i: (b, 0), pipeline_mode=pl.Buffered(1)),
            pl.BlockSpec((nq, FOX_HEADS * V_ROWS, TQ), lambda b, i: (b, 0, 0),
                         pipeline_mode=pl.Buffered(1)),
            _const_spec((FOX_WIDTH, TQ)),
        ],
        out_specs=pl.BlockSpec((FOX_WIDTH, TQ), lambda b, i: (0, b * nq + i)),
        out_shape=jax.ShapeDtypeStruct((FOX_WIDTH, batch * seq), BF16),
        scratch_shapes=[pltpu.VMEM((FOX_HEADS, TQ), F32),
                        pltpu.VMEM((FOX_HEADS, V_ROWS, TQ), F32),
                        pltpu.VMEM((FOX_HEADS, TQ, TQ), F32),
                        pltpu.VMEM((FOX_HEADS, TQ, TQ), F32)],
        compiler_params=pltpu.CompilerParams(
            dimension_semantics=("arbitrary", "arbitrary"), vmem_limit_bytes=VMEM_LIMIT),
        name="fox",
    )(qt, k, vt, g)


def _gla_kernel(qk_ref, v_ref, la_ref, sr_ref, g_ref, o_ref, st_ref):
    @pl.when(pl.program_id(1) == 0)
    def _():
        st_ref[...] = jnp.zeros_like(st_ref)

    nchunk = TG // CHUNK
    lane = lax.broadcasted_iota(jnp.int32, (1, LANES), 1)
    first = lane < GLA_KEY_DIM
    row = lax.broadcasted_iota(jnp.int32, (TG, TG), 0)
    col = lax.broadcasted_iota(jnp.int32, (TG, TG), 1)
    causal = (row - col).astype(jnp.uint32) <= (row & (CHUNK - 1)).astype(jnp.uint32)
    tri = jnp.where(causal, 1.0, 0.0).astype(BF16)
    scale = GLA_KEY_DIM ** -0.5

    b = _tri_cumsum(tri, la_ref[...])
    b_last3 = b.reshape(nchunk, CHUNK, GLA_QK)[:, CHUNK - 1:CHUNK, :]
    b_last = jnp.broadcast_to(b_last3, (nchunk, CHUNK, GLA_QK)).reshape(TG, GLA_QK)
    q = qk_ref[:, 0:GLA_QK].astype(F32)
    k = qk_ref[:, GLA_QK:2 * GLA_QK].astype(F32)
    q_dec = (q * scale * jnp.exp(b)).astype(BF16)
    k_inv = (k * jnp.exp(-b)).astype(BF16)
    k_rem = (k * jnp.exp(b_last - b)).astype(BF16)
    decay3 = jnp.exp(b_last3)
    zero = jnp.zeros((TG, LANES), BF16)
    heads = range(GLA_HEADS)
    chunks = [slice(c * CHUNK, (c + 1) * CHUNK) for c in range(nchunk)]
    prs = [slice((h // 2) * LANES, (h // 2 + 1) * LANES) for h in heads]
    cols = [slice(h * GLA_VAL_DIM, (h + 1) * GLA_VAL_DIM) for h in heads]
    hmasks = [first if h % 2 == 0 else jnp.logical_not(first) for h in heads]
    qd = [jnp.where(hmasks[h], q_dec[:, prs[h]], zero) for h in heads]
    kr = [jnp.where(hmasks[h], k_rem[:, prs[h]], zero) for h in heads]
    vh = [v_ref[:, cols[h]] for h in heads]
    kv = [[_dot_tn(vh[h][r], kr[h][r]) for r in chunks] for h in heads]
    att = [jnp.where(causal, _dot_nt(qd[h], k_inv[:, prs[h]]), 0.0).astype(BF16) for h in heads]
    states = []
    for h in heads:
        st = st_ref[h]
        per_chunk = []
        for c in range(nchunk):
            per_chunk.append(st.astype(BF16))
            st = st * decay3[c][:, prs[h]] + kv[h][c]
        st_ref[h] = st
        states.append(per_chunk)
    for h in heads:
        inter = [_dot_nt(qd[h][chunks[c]], states[h][c]) for c in range(nchunk)]
        o = _dot(att[h], vh[h]) + jnp.concatenate(inter, axis=0)
        ms = jnp.mean(o * o, axis=-1, keepdims=True)
        y = o * lax.rsqrt(ms + RMS_EPS) * g_ref[:, cols[h]] * sr_ref[:, cols[h]].astype(F32)
        o_ref[:, cols[h]] = y.astype(BF16)


def _gla(gqk, gv, la, sr, g, batch, seq):
    nt = seq // TG
    row = lambda w: pl.BlockSpec((TG, w), lambda b, i: (b * nt + i, 0))
    return pl.pallas_call(
        _gla_kernel,
        grid=(batch, nt),
        in_specs=[row(2 * GLA_QK), row(GLA_WIDTH), row(GLA_QK), row(GLA_WIDTH),
                  pl.BlockSpec((1, GLA_WIDTH), lambda b, i: (0, 0))],
        out_specs=row(GLA_WIDTH),
        out_shape=jax.ShapeDtypeStruct((batch * seq, GLA_WIDTH), BF16),
        scratch_shapes=[pltpu.VMEM((GLA_HEADS, GLA_VAL_DIM, LANES), F32)],
        compiler_params=pltpu.CompilerParams(
            dimension_semantics=("arbitrary", "arbitrary"), vmem_limit_bytes=VMEM_LIMIT),
        name="gla",
    )(gqk, gv, la, sr, g)


def _mem_kv_kernel(m_ref, g_ref, w_ref, o_ref):
    h = _rms(m_ref[...], g_ref[...]).astype(BF16)
    o_ref[...] = _dot(h, w_ref[...]).astype(BF16)


def _mem_kv(mem, g, w, mem_len):
    t = mem.shape[0]
    width = w.shape[1]
    return pl.pallas_call(
        _mem_kv_kernel,
        grid=(t // mem_len,),
        in_specs=[pl.BlockSpec((mem_len, D_MODEL), lambda i: (i, 0)),
                  _const_spec((1, D_MODEL)), _const_spec((D_MODEL, width))],
        out_specs=pl.BlockSpec((mem_len, width), lambda i: (i, 0)),
        out_shape=jax.ShapeDtypeStruct((t, width), BF16),
        compiler_params=pltpu.CompilerParams(
            dimension_semantics=("arbitrary",), vmem_limit_bytes=VMEM_LIMIT),
        name="mem_kv",
    )(mem, g, w)


def _mix_out_mem_kernel(x_ref, oft_ref, og_ref, kv_ref, wout_ref, gpost_ref, gpre_ref,
                        wq_ref, wo_ref, gmpost_ref, o_ref):
    y = _dot_tn(oft_ref[...], wout_ref[0:FOX_WIDTH, :]) + _dot(og_ref[...], wout_ref[FOX_WIDTH:, :])
    x = x_ref[...] + _rms(y, gpost_ref[...])
    h = _rms(x, gpre_ref[...]).astype(BF16)
    q = (_dot(h, wq_ref[...]) * (MEM_HEAD_DIM ** -0.5)).astype(BF16)
    width = MEM_HEADS * MEM_HEAD_DIM
    outs = []
    for hd in range(MEM_HEADS):
        cols = slice(hd * MEM_HEAD_DIM, (hd + 1) * MEM_HEAD_DIM)
        s = _dot_nt(q[:, cols], kv_ref[:, cols])
        m = jnp.max(s, axis=-1, keepdims=True)
        p = jnp.exp(s - m)
        p = p / jnp.sum(p, axis=-1, keepdims=True)
        vcols = slice(width + hd * MEM_HEAD_DIM, width + (hd + 1) * MEM_HEAD_DIM)
        outs.append(_dot(p.astype(BF16), kv_ref[:, vcols]).astype(BF16))
    o = jnp.concatenate(outs, axis=-1)
    y2 = _dot(o, wo_ref[...])
    o_ref[...] = x + _rms(y2, gmpost_ref[...])


def _mix_out_mem(x, o_fox_t, o_gla, kv, wout, gpost, gpre, wq, wo, gmpost, seq, mem_len):
    t = x.shape[0]
    tiles_per_seq = seq // TM
    row = lambda w: pl.BlockSpec((TM, w), lambda i: (i, 0))
    return pl.pallas_call(
        _mix_out_mem_kernel,
        grid=(t // TM,),
        in_specs=[
            row(D_MODEL),
            pl.BlockSpec((FOX_WIDTH, TM), lambda i: (0, i)),
            row(GLA_WIDTH),
            pl.BlockSpec((mem_len, kv.shape[1]), lambda i: (i // tiles_per_seq, 0)),
            _const_spec((D_MODEL, D_MODEL)), _const_spec((1, D_MODEL)), _const_spec((1, D_MODEL)),
            _const_spec((D_MODEL, D_MODEL)), _const_spec((D_MODEL, D_MODEL)),
            _const_spec((1, D_MODEL)),
        ],
        out_specs=row(D_MODEL),
        out_shape=jax.ShapeDtypeStruct((t, D_MODEL), F32),
        compiler_params=pltpu.CompilerParams(
            dimension_semantics=("arbitrary",), vmem_limit_bytes=VMEM_LIMIT),
        name="mix_out_mem",
    )(x, o_fox_t, o_gla, kv, wout, gpost, gpre, wq, wo, gmpost)


def _pad_cols(w, width):
    return jnp.pad(w, ((0, 0), (0, width - w.shape[1])))


def _aug_base(h):
    return FOX_HEAD_DIM if h % 2 == 0 else 0


def _placement():
    pk = np.zeros((LANES, FOX_PAD), np.float32)
    pq = np.zeros((FOX_HEADS * AUG_ROWS, LANES), np.float32)
    for h in range(FOX_HEADS):
        for t in range(C_TERMS):
            src = t * FOX_HEADS + h
            pq[h * AUG_ROWS + t, src] = 1.0
            pk[ONE_LANE, h * LANES + _aug_base(h) + t] = 1.0
            pq[h * AUG_ROWS + C_TERMS + t, ONE_LANE] = 1.0
            pk[src, h * LANES + _aug_base(h) + C_TERMS + t] = -1.0
    return jnp.asarray(pk, BF16), jnp.asarray(pq, BF16)


def kernel(x, mem, g_ff1_pre, w_ff1_gate, w_ff1_up, w_ff1_down, g_ff1_post, g_mix_pre, w_mix_in, b_fox_f, w_gla_g2, b_gla_g, g_fox_out, g_gla_out, w_mix_out, g_mix_post, g_mem_pre, g_mem_src, w_mem_q, w_mem_kv, w_mem_o, g_mem_post, g_ff2_pre, w_ff2_gate, w_ff2_up, w_ff2_down, g_ff2_post, g_final):
    batch, seq, d = x.shape
    mem_len = mem.shape[1]
    depth = w_ff1_gate.shape[0]
    xt = x.reshape(batch * seq, d)
    memt = mem.reshape(batch * mem_len, d)
    gfin = g_final.reshape(1, d)
    vec = lambda v: v.reshape(1, -1).astype(F32)
    pk, pqt = _placement()
    for l in range(depth):
        last = l == depth - 1
        xt = _ffn(xt, vec(g_ff1_pre[l]), w_ff1_gate[l].astype(BF16), w_ff1_up[l].astype(BF16),
                  w_ff1_down[l].astype(BF16), vec(g_ff1_post[l]), gfin, False)

        w_in = w_mix_in[l]
        wqt = (w_in[:, OFF_FQ:OFF_FK] * (LOG2E * FOX_HEAD_DIM ** -0.5)).T.astype(BF16)
        wk = w_in[:, OFF_FK:OFF_FV].astype(BF16)
        wvt = w_in[:, OFF_FV:OFF_FF].T.astype(BF16)
        wgla = w_in[:, OFF_GQ:OFF_GG].astype(BF16)
        wgr = w_in[:, OFF_GR:IN_WIDTH].astype(BF16)
        wsm = _pad_cols(jnp.concatenate([w_in[:, OFF_FF:OFF_GQ], w_in[:, OFF_GG:OFF_GR]], axis=1),
                        LANES).astype(BF16)
        bf = _pad_cols(vec(b_fox_f[l]), LANES)
        wg2 = jnp.pad(w_gla_g2[l].astype(F32),
                      ((FOX_HEADS, LANES - FOX_HEADS - GLA_GATE_RANK), (0, 0)))
        k, qt, vt, gqk, gv, la, sr = _mix_in(xt, vec(g_mix_pre[l]), wk, wqt, wvt, wgla, wgr, wsm,
                                             bf, pk, pqt, wg2, vec(b_gla_g[l]), seq)

        g_fox = jnp.broadcast_to(g_fox_out[l].astype(F32)[:, None], (FOX_WIDTH, TQ))
        o_fox_t = _fox(qt, k, vt, g_fox, batch, seq)
        o_gla = _gla(gqk, gv, la, sr, vec(g_gla_out[l]), batch, seq)

        kv = _mem_kv(memt, vec(g_mem_src[l]), w_mem_kv[l].astype(BF16), mem_len)
        xt = _mix_out_mem(xt, o_fox_t, o_gla, kv, w_mix_out[l].astype(BF16), vec(g_mix_post[l]),
                          vec(g_mem_pre[l]), w_mem_q[l].astype(BF16), w_mem_o[l].astype(BF16),
                          vec(g_mem_post[l]), seq, mem_len)

        xt = _ffn(xt, vec(g_ff2_pre[l]), w_ff2_gate[l].astype(BF16), w_ff2_up[l].astype(BF16),
                  w_ff2_down[l].astype(BF16), vec(g_ff2_post[l]), gfin, last)
    return xt.reshape(batch, seq, d)
```

```python
import functools

import numpy as np
import jax
import jax.numpy as jnp
from jax import lax
from jax.experimental import pallas as pl
from jax.experimental.pallas import tpu as pltpu

F32 = jnp.float32
BF16 = jnp.bfloat16

D_MODEL = 1024
CHUNK = 64
FOX_WIDTH = 512
FOX_HEADS = 8
FOX_HEAD_DIM = 64
GLA_WIDTH = 512
GLA_HEADS = 4
GLA_VAL_DIM = 128
GLA_KEY_DIM = 64
GLA_QK = 256
GLA_GATE_RANK = 16
GLA_GATE_NORM = 16.0
MEM_HEADS = 4
MEM_HEAD_DIM = 256
D_FF = 2816
RMS_EPS = 1e-6

OFF_FQ = 0
OFF_FK = OFF_FQ + FOX_WIDTH
OFF_FV = OFF_FK + FOX_WIDTH
OFF_FF = OFF_FV + FOX_WIDTH
OFF_GQ = OFF_FF + FOX_HEADS
OFF_GK = OFF_GQ + GLA_QK
OFF_GV = OFF_GK + GLA_QK
OFF_GG = OFF_GV + GLA_WIDTH
OFF_GR = OFF_GG + GLA_GATE_RANK
IN_WIDTH = OFF_GR + GLA_WIDTH

LANES = 128
VMEM_LIMIT = 56 * 1024 * 1024

TM = 512
TQ = 256
TG = 512
FFN_GROUPS = 2
TMO = 1024
MIX_GROUPS = 2

FOX_PAD = FOX_HEADS * LANES
C_TERMS = 3
ONE_LANE = C_TERMS * FOX_HEADS
AUG_ROWS = 16
V_ONE_ROW = FOX_HEAD_DIM
V_ROWS = FOX_HEAD_DIM + AUG_ROWS
LOG2E = 1.4426950408889634


def _rms(x, g):
    return x * lax.rsqrt(jnp.mean(x * x, axis=-1, keepdims=True) + RMS_EPS) * g


def _dot(a, b):
    return jnp.dot(a, b, preferred_element_type=F32)


def _dot_nt(a, b):
    return lax.dot_general(a, b, (((1,), (1,)), ((), ())), preferred_element_type=F32)


def _dot_tn(a, b):
    return lax.dot_general(a, b, (((0,), (0,)), ((), ())), preferred_element_type=F32)


def _split3(x):
    hi = x.astype(BF16)
    r1 = x - hi.astype(F32)
    mid = r1.astype(BF16)
    lo = (r1 - mid.astype(F32)).astype(BF16)
    return hi, mid, lo


def _tri_cumsum(tri_bf16, x):
    hi, mid, lo = _split3(x)
    return _dot(tri_bf16, hi) + _dot(tri_bf16, mid) + _dot(tri_bf16, lo)


def _log_sigmoid(x):
    return jnp.minimum(x, 0.0) - jnp.log1p(jnp.exp(-jnp.abs(x)))


def _silu(x):
    return x * jax.nn.sigmoid(x)


def _const_spec(shape):
    return pl.BlockSpec(shape, lambda *_: (0,) * len(shape), pipeline_mode=pl.Buffered(1))


def _ffn_kernel(x_ref, gpre_ref, wg_ref, wu_ref, wd_ref, gpost_ref, gfin_ref, o_ref, *, final):
    groups = [slice(n * TM // FFN_GROUPS, (n + 1) * TM // FFN_GROUPS) for n in range(FFN_GROUPS)]
    x = [x_ref[r, :] for r in groups]
    h = [_rms(xn, gpre_ref[...]).astype(BF16) for xn in x]
    a = [(_silu(_dot(hn, wg_ref[...])) * _dot(hn, wu_ref[...])).astype(BF16) for hn in h]
    y = [_dot(an, wd_ref[...]) for an in a]
    for n, r in enumerate(groups):
        out = x[n] + 0.5 * _rms(y[n], gpost_ref[...])
        if final:
            out = _rms(out, gfin_ref[...])
        o_ref[r, :] = out


def _ffn(x, gpre, wg, wu, wd, gpost, gfin, final):
    t = x.shape[0]
    return pl.pallas_call(
        functools.partial(_ffn_kernel, final=final),
        grid=(t // TM,),
        in_specs=[
            pl.BlockSpec((TM, D_MODEL), lambda i: (i, 0)),
            _const_spec((1, D_MODEL)),
            _const_spec((D_MODEL, D_FF)),
            _const_spec((D_MODEL, D_FF)),
            _const_spec((D_FF, D_MODEL)),
            _const_spec((1, D_MODEL)),
            _const_spec((1, D_MODEL)),
        ],
        out_specs=pl.BlockSpec((TM, D_MODEL), lambda i: (i, 0)),
        out_shape=jax.ShapeDtypeStruct((t, D_MODEL), F32),
        compiler_params=pltpu.CompilerParams(
            dimension_semantics=("arbitrary",), vmem_limit_bytes=VMEM_LIMIT),
        name="ffn_final" if final else "ffn",
    )(x, gpre, wg, wu, wd, gpost, gfin)


def _mix_in_kernel(x_ref, g_ref, wk_ref, wqt_ref, wvt_ref, wgla_ref, wgr_ref, wsm_ref,
                   bf_ref, pk_ref, pqt_ref, wg2_ref, bg_ref,
                   k_ref, qt_ref, vt_ref, gqk_ref, gv_ref, la_ref, sr_ref,
                   carry_ref, *, tiles_per_seq):
    i = pl.program_id(0)

    @pl.when(i % tiles_per_seq == 0)
    def _():
        carry_ref[...] = jnp.zeros_like(carry_ref)

    h = _rms(x_ref[...], g_ref[...]).astype(BF16)
    gl = _dot(h, wgla_ref[...])
    gqk_ref[...] = gl[:, :2 * GLA_QK].astype(BF16)
    gv_ref[...] = gl[:, 2 * GLA_QK:].astype(BF16)
    sr_ref[...] = _silu(_dot(h, wgr_ref[...])).astype(BF16)

    sm = _dot(h, wsm_ref[...])

    lane = lax.broadcasted_iota(jnp.int32, (1, LANES), 1)
    lf = jnp.where(lane < FOX_HEADS, _log_sigmoid(sm + bf_ref[...]), 0.0)
    row = lax.broadcasted_iota(jnp.int32, (TM, TM), 0)
    col = lax.broadcasted_iota(jnp.int32, (TM, TM), 1)
    tri = jnp.where(col <= row, 1.0, 0.0).astype(BF16)
    c = _tri_cumsum(tri, lf) + carry_ref[...]
    carry_ref[...] = c[TM - 1:TM, :]
    c_hi, c_mid, c_lo = _split3(c * LOG2E)
    packed = (c_hi.astype(F32) + pltpu.roll(c_mid.astype(F32), FOX_HEADS, 1)
              + pltpu.roll(c_lo.astype(F32), 2 * FOX_HEADS, 1)
              + jnp.where(lane == ONE_LANE, 1.0, 0.0)).astype(BF16)

    kp = _dot(h, wk_ref[...])
    kaug = _dot(packed, pk_ref[...])
    low = lane < FOX_HEAD_DIM
    for hd in range(FOX_HEADS):
        pair = slice((hd // 2) * LANES, (hd // 2 + 1) * LANES)
        slot = slice(hd * LANES, (hd + 1) * LANES)
        own = low if hd % 2 == 0 else jnp.logical_not(low)
        k_ref[:, slot] = jnp.where(own, kp[:, pair], kaug[:, slot]).astype(BF16)

    qt = _dot_nt(wqt_ref[...], h).astype(BF16)
    aug = _dot_nt(pqt_ref[...], packed).astype(BF16)
    vt = _dot_nt(wvt_ref[...], h).astype(BF16)
    one_rows = jnp.where(lax.broadcasted_iota(jnp.int32, (AUG_ROWS, TQ), 0) == 0,
                         1.0, 0.0).astype(BF16)
    zeros = jnp.zeros((LANES - FOX_HEAD_DIM - AUG_ROWS, TQ), BF16)
    for n in range(TM // TQ):
        cs = slice(n * TQ, (n + 1) * TQ)
        for hd in range(FOX_HEADS):
            q_h = qt[hd * FOX_HEAD_DIM:(hd + 1) * FOX_HEAD_DIM, cs]
            a_h = aug[hd * AUG_ROWS:(hd + 1) * AUG_ROWS, cs]
            base = hd * LANES
            pieces = (q_h, a_h, zeros) if hd % 2 == 0 else (a_h, zeros, q_h)
            r = base
            for piece in pieces:
                qt_ref[n, r:r + piece.shape[0], :] = piece
                r += piece.shape[0]
            vb = hd * V_ROWS
            vt_ref[n, vb:vb + FOX_HEAD_DIM, :] = vt[hd * FOX_HEAD_DIM:(hd + 1) * FOX_HEAD_DIM, cs]
            vt_ref[n, vb + FOX_HEAD_DIM:vb + V_ROWS, :] = one_rows

    g_hi, g_mid, _ = _split3(sm)
    w_hi, w_mid, _ = _split3(wg2_ref[...])
    gate = _dot(g_hi, w_hi) + _dot(g_mid, w_hi) + _dot(g_hi, w_mid) + bg_ref[...]
    la_ref[...] = _log_sigmoid(gate) / GLA_GATE_NORM


def _mix_in(x, g, wk, wqt, wvt, wgla, wgr, wsm, bf, pk, pqt, wg2, bg, seq):
    t = x.shape[0]
    row = lambda w: pl.BlockSpec((TM, w), lambda i: (i, 0))
    tblk = lambda rows: pl.BlockSpec((TM // TQ, rows, TQ), lambda i: (i, 0, 0))
    return pl.pallas_call(
        functools.partial(_mix_in_kernel, tiles_per_seq=seq // TM),
        grid=(t // TM,),
        in_specs=[
            row(D_MODEL),
            _const_spec((1, D_MODEL)),
            _const_spec((D_MODEL, FOX_WIDTH)),
            _const_spec((FOX_WIDTH, D_MODEL)),
            _const_spec((FOX_WIDTH, D_MODEL)),
            _const_spec((D_MODEL, 2 * GLA_QK + GLA_WIDTH)),
            _const_spec((D_MODEL, GLA_WIDTH)),
            _const_spec((D_MODEL, LANES)),
            _const_spec((1, LANES)),
            _const_spec((LANES, FOX_PAD)),
            _const_spec((FOX_HEADS * AUG_ROWS, LANES)),
            _const_spec((LANES, GLA_QK)),
            _const_spec((1, GLA_QK)),
        ],
        out_specs=[row(FOX_PAD), tblk(FOX_PAD), tblk(FOX_HEADS * V_ROWS), row(2 * GLA_QK),
                   row(GLA_WIDTH), row(GLA_QK), row(GLA_WIDTH)],
        out_shape=[
            jax.ShapeDtypeStruct((t, FOX_PAD), BF16),
            jax.ShapeDtypeStruct((t // TQ, FOX_PAD, TQ), BF16),
            jax.ShapeDtypeStruct((t // TQ, FOX_HEADS * V_ROWS, TQ), BF16),
            jax.ShapeDtypeStruct((t, 2 * GLA_QK), BF16),
            jax.ShapeDtypeStruct((t, GLA_WIDTH), BF16),
            jax.ShapeDtypeStruct((t, GLA_QK), F32),
            jax.ShapeDtypeStruct((t, GLA_WIDTH), BF16),
        ],
        scratch_shapes=[pltpu.VMEM((1, LANES), F32)],
        compiler_params=pltpu.CompilerParams(
            dimension_semantics=("arbitrary",), vmem_limit_bytes=VMEM_LIMIT),
        name="mix_in",
    )(x, g, wk, wqt, wvt, wgla, wgr, wsm, bf, pk, pqt, wg2, bg)


def _fox_kernel(qt_ref, k_ref, vt_ref, g_ref, o_ref, m_ref, acc_ref, sa_ref, sb_ref):
    i = pl.program_id(1)
    m_ref[...] = jnp.full(m_ref.shape, -jnp.inf, F32)
    acc_ref[...] = jnp.zeros(acc_ref.shape, F32)
    key = lax.broadcasted_iota(jnp.int32, (TQ, TQ), 0)
    qry = lax.broadcasted_iota(jnp.int32, (TQ, TQ), 1)
    causal = key <= qry
    slots = [slice(h * LANES, (h + 1) * LANES) for h in range(FOX_HEADS)]
    vrows = [slice(h * V_ROWS, (h + 1) * V_ROWS) for h in range(FOX_HEADS)]

    def scores(j, s_ref):
        start = pl.multiple_of(j * TQ, TQ)
        for h in range(FOX_HEADS):
            s_ref[h] = _dot(k_ref[pl.ds(start, TQ), slots[h]], qt_ref[0, slots[h], :])

    def accumulate(j, s_ref, diagonal):
        probs, alphas = [], []
        for h in range(FOX_HEADS):
            s = s_ref[h]
            if diagonal:
                s = jnp.where(causal, s, -jnp.inf)
            m_old = m_ref[h:h + 1, :]
            m_new = jnp.maximum(m_old, jnp.max(s, axis=0, keepdims=True))
            alphas.append(jnp.exp2(m_old - m_new))
            probs.append(jnp.exp2(s - m_new).astype(BF16))
            m_ref[h:h + 1, :] = m_new
        for h in range(FOX_HEADS):
            acc_ref[h] = alphas[h] * acc_ref[h] + _dot(vt_ref[j, vrows[h], :], probs[h])

    scores(0, sa_ref)

    def pair(t, carry):
        j = 2 * t
        scores(j + 1, sb_ref)
        accumulate(j, sa_ref, False)
        scores(j + 2, sa_ref)
        accumulate(j + 1, sb_ref, False)
        return carry

    lax.fori_loop(0, i // 2, pair, 0)

    @pl.when(i % 2 == 0)
    def _():
        accumulate(i, sa_ref, True)

    @pl.when(i % 2 == 1)
    def _():
        scores(i, sb_ref)
        accumulate(i - 1, sa_ref, False)
        accumulate(i, sb_ref, True)

    for h in range(FOX_HEADS):
        acc = acc_ref[h]
        o = acc[:FOX_HEAD_DIM, :] / acc[V_ONE_ROW:V_ONE_ROW + 1, :]
        ms = jnp.mean(o * o, axis=0, keepdims=True)
        rows = slice(h * FOX_HEAD_DIM, (h + 1) * FOX_HEAD_DIM)
        o_ref[rows, :] = (o * lax.rsqrt(ms + RMS_EPS) * g_ref[rows, :]).astype(BF16)


def _fox(qt, k, vt, g, batch, seq):
    nq = seq // TQ
    return pl.pallas_call(
        _fox_kernel,
        grid=(batch, nq),
        in_specs=[
            pl.BlockSpec((1, FOX_PAD, TQ), lambda b, i: (b * nq + i, 0, 0)),
            pl.BlockSpec((seq, FOX_PAD), lambda b, i: (b, 0), pipeline_mode=pl.Buffered(1)),
            pl.BlockSpec((nq, FOX_HEADS * V_ROWS, TQ), lambda b, i: (b, 0, 0),
                         pipeline_mode=pl.Buffered(1)),
            _const_spec((FOX_WIDTH, TQ)),
        ],
        out_specs=pl.BlockSpec((FOX_WIDTH, TQ), lambda b, i: (0, b * nq + i)),
        out_shape=jax.ShapeDtypeStruct((FOX_WIDTH, batch * seq), BF16),
        scratch_shapes=[pltpu.VMEM((FOX_HEADS, TQ), F32),
                        pltpu.VMEM((FOX_HEADS, V_ROWS, TQ), F32),
                        pltpu.VMEM((FOX_HEADS, TQ, TQ), F32),
                        pltpu.VMEM((FOX_HEADS, TQ, TQ), F32)],
        compiler_params=pltpu.CompilerParams(
            dimension_semantics=("arbitrary", "arbitrary"), vmem_limit_bytes=VMEM_LIMIT),
        name="fox",
    )(qt, k, vt, g)


def _gla_kernel(qk_ref, v_ref, la_ref, sr_ref, g_ref, o_ref, st_ref):
    @pl.when(pl.program_id(1) == 0)
    def _():
        st_ref[...] = jnp.zeros_like(st_ref)

    nchunk = TG // CHUNK
    lane = lax.broadcasted_iota(jnp.int32, (1, LANES), 1)
    first = lane < GLA_KEY_DIM
    row = lax.broadcasted_iota(jnp.int32, (TG, TG), 0)
    col = lax.broadcasted_iota(jnp.int32, (TG, TG), 1)
    causal = (row - col).astype(jnp.uint32) <= (row & (CHUNK - 1)).astype(jnp.uint32)
    tri = jnp.where(causal, 1.0, 0.0).astype(BF16)
    scale = GLA_KEY_DIM ** -0.5

    b = _tri_cumsum(tri, la_ref[...])
    b_last3 = b.reshape(nchunk, CHUNK, GLA_QK)[:, CHUNK - 1:CHUNK, :]
    b_last = jnp.broadcast_to(b_last3, (nchunk, CHUNK, GLA_QK)).reshape(TG, GLA_QK)
    q = qk_ref[:, 0:GLA_QK].astype(F32)
    k = qk_ref[:, GLA_QK:2 * GLA_QK].astype(F32)
    q_dec = (q * scale * jnp.exp(b)).astype(BF16)
    k_inv = (k * jnp.exp(-b)).astype(BF16)
    k_rem = (k * jnp.exp(b_last - b)).astype(BF16)
    decay3 = jnp.exp(b_last3)
    zero = jnp.zeros((TG, LANES), BF16)
    heads = range(GLA_HEADS)
    chunks = [slice(c * CHUNK, (c + 1) * CHUNK) for c in range(nchunk)]
    prs = [slice((h // 2) * LANES, (h // 2 + 1) * LANES) for h in heads]
    cols = [slice(h * GLA_VAL_DIM, (h + 1) * GLA_VAL_DIM) for h in heads]
    hmasks = [first if h % 2 == 0 else jnp.logical_not(first) for h in heads]
    qd = [jnp.where(hmasks[h], q_dec[:, prs[h]], zero) for h in heads]
    kr = [jnp.where(hmasks[h], k_rem[:, prs[h]], zero) for h in heads]
    vh = [v_ref[:, cols[h]] for h in heads]
    kv = [[_dot_tn(vh[h][r], kr[h][r]) for r in chunks] for h in heads]
    att = [jnp.where(causal, _dot_nt(qd[h], k_inv[:, prs[h]]), 0.0).astype(BF16) for h in heads]
    states = []
    for h in heads:
        st = st_ref[h]
        per_chunk = []
        for c in range(nchunk):
            per_chunk.append(st.astype(BF16))
            st = st * decay3[c][:, prs[h]] + kv[h][c]
        st_ref[h] = st
        states.append(per_chunk)
    for h in heads:
        inter = [_dot_nt(qd[h][chunks[c]], states[h][c]) for c in range(nchunk)]
        o = _dot(att[h], vh[h]) + jnp.concatenate(inter, axis=0)
        ms = jnp.mean(o * o, axis=-1, keepdims=True)
        y = o * lax.rsqrt(ms + RMS_EPS) * g_ref[:, cols[h]] * sr_ref[:, cols[h]].astype(F32)
        o_ref[:, cols[h]] = y.astype(BF16)


def _gla(gqk, gv, la, sr, g, batch, seq):
    nt = seq // TG
    row = lambda w: pl.BlockSpec((TG, w), lambda b, i: (b * nt + i, 0))
    return pl.pallas_call(
        _gla_kernel,
        grid=(batch, nt),
        in_specs=[row(2 * GLA_QK), row(GLA_WIDTH), row(GLA_QK), row(GLA_WIDTH),
                  pl.BlockSpec((1, GLA_WIDTH), lambda b, i: (0, 0))],
        out_specs=row(GLA_WIDTH),
        out_shape=jax.ShapeDtypeStruct((batch * seq, GLA_WIDTH), BF16),
        scratch_shapes=[pltpu.VMEM((GLA_HEADS, GLA_VAL_DIM, LANES), F32)],
        compiler_params=pltpu.CompilerParams(
            dimension_semantics=("arbitrary", "arbitrary"), vmem_limit_bytes=VMEM_LIMIT),
        name="gla",
    )(gqk, gv, la, sr, g)


def _mem_kv_kernel(m_ref, g_ref, w_ref, o_ref):
    h = _rms(m_ref[...], g_ref[...]).astype(BF16)
    o_ref[...] = _dot(h, w_ref[...]).astype(BF16)


def _mem_kv(mem, g, w, mem_len):
    t = mem.shape[0]
    width = w.shape[1]
    return pl.pallas_call(
        _mem_kv_kernel,
        grid=(t // mem_len,),
        in_specs=[pl.BlockSpec((mem_len, D_MODEL), lambda i: (i, 0)),
                  _const_spec((1, D_MODEL)), _const_spec((D_MODEL, width))],
        out_specs=pl.BlockSpec((mem_len, width), lambda i: (i, 0)),
        out_shape=jax.ShapeDtypeStruct((t, width), BF16),
        compiler_params=pltpu.CompilerParams(
            dimension_semantics=("arbitrary",), vmem_limit_bytes=VMEM_LIMIT),
        name="mem_kv",
    )(mem, g, w)


def _mix_out_mem_kernel(x_ref, oft_ref, og_ref, kv_ref, wout_ref, gpost_ref, gpre_ref,
                        wq_ref, wo_ref, gmpost_ref, o_ref):
    width = MEM_HEADS * MEM_HEAD_DIM
    groups = [slice(n * TMO // MIX_GROUPS, (n + 1) * TMO // MIX_GROUPS) for n in range(MIX_GROUPS)]
    y = [_dot_tn(oft_ref[:, r], wout_ref[0:FOX_WIDTH, :]) + _dot(og_ref[r, :], wout_ref[FOX_WIDTH:, :])
         for r in groups]
    x = [x_ref[r, :] + _rms(y[n], gpost_ref[...]) for n, r in enumerate(groups)]
    h = [_rms(xn, gpre_ref[...]).astype(BF16) for xn in x]
    q = [(_dot(hn, wq_ref[...]) * (MEM_HEAD_DIM ** -0.5)).astype(BF16) for hn in h]
    o = []
    for qn in q:
        outs = []
        for hd in range(MEM_HEADS):
            cols = slice(hd * MEM_HEAD_DIM, (hd + 1) * MEM_HEAD_DIM)
            s = _dot_nt(qn[:, cols], kv_ref[:, cols])
            m = jnp.max(s, axis=-1, keepdims=True)
            p = jnp.exp(s - m)
            p = p / jnp.sum(p, axis=-1, keepdims=True)
            vcols = slice(width + hd * MEM_HEAD_DIM, width + (hd + 1) * MEM_HEAD_DIM)
            outs.append(_dot(p.astype(BF16), kv_ref[:, vcols]).astype(BF16))
        o.append(jnp.concatenate(outs, axis=-1))
    y2 = [_dot(on, wo_ref[...]) for on in o]
    for n, r in enumerate(groups):
        o_ref[r, :] = x[n] + _rms(y2[n], gmpost_ref[...])


def _mix_out_mem(x, o_fox_t, o_gla, kv, wout, gpost, gpre, wq, wo, gmpost, seq, mem_len):
    t = x.shape[0]
    tiles_per_seq = seq // TMO
    row = lambda w: pl.BlockSpec((TMO, w), lambda i: (i, 0))
    return pl.pallas_call(
        _mix_out_mem_kernel,
        grid=(t // TMO,),
        in_specs=[
            row(D_MODEL),
            pl.BlockSpec((FOX_WIDTH, TMO), lambda i: (0, i)),
            row(GLA_WIDTH),
            pl.BlockSpec((mem_len, kv.shape[1]), lambda i: (i // tiles_per_seq, 0)),
            _const_spec((D_MODEL, D_MODEL)), _const_spec((1, D_MODEL)), _const_spec((1, D_MODEL)),
            _const_spec((D_MODEL, D_MODEL)), _const_spec((D_MODEL, D_MODEL)),
            _const_spec((1, D_MODEL)),
        ],
        out_specs=row(D_MODEL),
        out_shape=jax.ShapeDtypeStruct((t, D_MODEL), F32),
        compiler_params=pltpu.CompilerParams(
            dimension_semantics=("arbitrary",), vmem_limit_bytes=VMEM_LIMIT),
        name="mix_out_mem",
    )(x, o_fox_t, o_gla, kv, wout, gpost, gpre, wq, wo, gmpost)


def _pad_cols(w, width):
    return jnp.pad(w, ((0, 0), (0, width - w.shape[1])))


def _aug_base(h):
    return FOX_HEAD_DIM if h % 2 == 0 else 0


def _placement():
    pk = np.zeros((LANES, FOX_PAD), np.float32)
    pq = np.zeros((FOX_HEADS * AUG_ROWS, LANES), np.float32)
    for h in range(FOX_HEADS):
        for t in range(C_TERMS):
            src = t * FOX_HEADS + h
            pq[h * AUG_ROWS + t, src] = 1.0
            pk[ONE_LANE, h * LANES + _aug_base(h) + t] = 1.0
            pq[h * AUG_ROWS + C_TERMS + t, ONE_LANE] = 1.0
            pk[src, h * LANES + _aug_base(h) + C_TERMS + t] = -1.0
    return jnp.asarray(pk, BF16), jnp.asarray(pq, BF16)


def kernel(x, mem, g_ff1_pre, w_ff1_gate, w_ff1_up, w_ff1_down, g_ff1_post, g_mix_pre, w_mix_in, b_fox_f, w_gla_g2, b_gla_g, g_fox_out, g_gla_out, w_mix_out, g_mix_post, g_mem_pre, g_mem_src, w_mem_q, w_mem_kv, w_mem_o, g_mem_post, g_ff2_pre, w_ff2_gate, w_ff2_up, w_ff2_down, g_ff2_post, g_final):
    batch, seq, d = x.shape
    mem_len = mem.shape[1]
    depth = w_ff1_gate.shape[0]
    xt = x.reshape(batch * seq, d)
    memt = mem.reshape(batch * mem_len, d)
    gfin = g_final.reshape(1, d)
    vec = lambda v: v.reshape(1, -1).astype(F32)
    pk, pqt = _placement()
    for l in range(depth):
        last = l == depth - 1
        xt = _ffn(xt, vec(g_ff1_pre[l]), w_ff1_gate[l].astype(BF16), w_ff1_up[l].astype(BF16),
                  w_ff1_down[l].astype(BF16), vec(g_ff1_post[l]), gfin, False)

        w_in = w_mix_in[l]
        wqt = (w_in[:, OFF_FQ:OFF_FK] * (LOG2E * FOX_HEAD_DIM ** -0.5)).T.astype(BF16)
        wk = w_in[:, OFF_FK:OFF_FV].astype(BF16)
        wvt = w_in[:, OFF_FV:OFF_FF].T.astype(BF16)
        wgla = w_in[:, OFF_GQ:OFF_GG].astype(BF16)
        wgr = w_in[:, OFF_GR:IN_WIDTH].astype(BF16)
        wsm = _pad_cols(jnp.concatenate([w_in[:, OFF_FF:OFF_GQ], w_in[:, OFF_GG:OFF_GR]], axis=1),
                        LANES).astype(BF16)
        bf = _pad_cols(vec(b_fox_f[l]), LANES)
        wg2 = jnp.pad(w_gla_g2[l].astype(F32),
                      ((FOX_HEADS, LANES - FOX_HEADS - GLA_GATE_RANK), (0, 0)))
        k, qt, vt, gqk, gv, la, sr = _mix_in(xt, vec(g_mix_pre[l]), wk, wqt, wvt, wgla, wgr, wsm,
                                             bf, pk, pqt, wg2, vec(b_gla_g[l]), seq)

        g_fox = jnp.broadcast_to(g_fox_out[l].astype(F32)[:, None], (FOX_WIDTH, TQ))
        o_fox_t = _fox(qt, k, vt, g_fox, batch, seq)
        o_gla = _gla(gqk, gv, la, sr, vec(g_gla_out[l]), batch, seq)

        kv = _mem_kv(memt, vec(g_mem_src[l]), w_mem_kv[l].astype(BF16), mem_len)
        xt = _mix_out_mem(xt, o_fox_t, o_gla, kv, w_mix_out[l].astype(BF16), vec(g_mix_post[l]),
                          vec(g_mem_pre[l]), w_mem_q[l].astype(BF16), w_mem_o[l].astype(BF16),
                          vec(g_mem_post[l]), seq, mem_len)

        xt = _ffn(xt, vec(g_ff2_pre[l]), w_ff2_gate[l].astype(BF16), w_ff2_up[l].astype(BF16),
                  w_ff2_down[l].astype(BF16), vec(g_ff2_post[l]), gfin, last)
    return xt.reshape(batch, seq, d)
```

```python
import functools

import numpy as np
import jax
import jax.numpy as jnp
from jax import lax
from jax.experimental import pallas as pl
from jax.experimental.pallas import tpu as pltpu

F32 = jnp.float32
BF16 = jnp.bfloat16

D_MODEL = 1024
CHUNK = 64
FOX_WIDTH = 512
FOX_HEADS = 8
FOX_HEAD_DIM = 64
GLA_WIDTH = 512
GLA_HEADS = 4
GLA_VAL_DIM = 128
GLA_KEY_DIM = 64
GLA_QK = 256
GLA_GATE_RANK = 16
GLA_GATE_NORM = 16.0
MEM_HEADS = 4
MEM_HEAD_DIM = 256
D_FF = 2816
RMS_EPS = 1e-6

OFF_FQ = 0
OFF_FK = OFF_FQ + FOX_WIDTH
OFF_FV = OFF_FK + FOX_WIDTH
OFF_FF = OFF_FV + FOX_WIDTH
OFF_GQ = OFF_FF + FOX_HEADS
OFF_GK = OFF_GQ + GLA_QK
OFF_GV = OFF_GK + GLA_QK
OFF_GG = OFF_GV + GLA_WIDTH
OFF_GR = OFF_GG + GLA_GATE_RANK
IN_WIDTH = OFF_GR + GLA_WIDTH

LANES = 128
VMEM_LIMIT = 56 * 1024 * 1024

TM = 512
TQ = 256
TG = 512
GLA_SEQS = 2
FFN_GROUPS = 2
TMO = 1024
MIX_GROUPS = 2

FOX_PAD = FOX_HEADS * LANES
C_TERMS = 3
ONE_LANE = C_TERMS * FOX_HEADS
AUG_ROWS = 16
V_ONE_ROW = FOX_HEAD_DIM
V_ROWS = FOX_HEAD_DIM + AUG_ROWS
LOG2E = 1.4426950408889634


def _rms(x, g):
    return x * lax.rsqrt(jnp.mean(x * x, axis=-1, keepdims=True) + RMS_EPS) * g


def _dot(a, b):
    return jnp.dot(a, b, preferred_element_type=F32)


def _dot_nt(a, b):
    return lax.dot_general(a, b, (((1,), (1,)), ((), ())), preferred_element_type=F32)


def _dot_tn(a, b):
    return lax.dot_general(a, b, (((0,), (0,)), ((), ())), preferred_element_type=F32)


def _split3(x):
    hi = x.astype(BF16)
    r1 = x - hi.astype(F32)
    mid = r1.astype(BF16)
    lo = (r1 - mid.astype(F32)).astype(BF16)
    return hi, mid, lo


def _tri_cumsum(tri_bf16, x):
    hi, mid, lo = _split3(x)
    return _dot(tri_bf16, hi) + _dot(tri_bf16, mid) + _dot(tri_bf16, lo)


def _pack_terms(x, one_lane=None):
    hi, mid, lo = _split3(x)
    out = (hi.astype(F32) + pltpu.roll(mid.astype(F32), FOX_HEADS, 1)
           + pltpu.roll(lo.astype(F32), 2 * FOX_HEADS, 1))
    if one_lane is not None:
        lane = lax.broadcasted_iota(jnp.int32, (1, LANES), 1)
        out = out + jnp.where(lane == one_lane, 1.0, 0.0)
    return out.astype(BF16)


def _log_sigmoid(x):
    return jnp.minimum(x, 0.0) - jnp.log1p(jnp.exp(-jnp.abs(x)))


def _silu(x):
    return x * jax.nn.sigmoid(x)


def _const_spec(shape):
    return pl.BlockSpec(shape, lambda *_: (0,) * len(shape), pipeline_mode=pl.Buffered(1))


def _ffn_kernel(x_ref, gpre_ref, wg_ref, wu_ref, wd_ref, gpost_ref, gfin_ref, o_ref, *, final):
    groups = [slice(n * TM // FFN_GROUPS, (n + 1) * TM // FFN_GROUPS) for n in range(FFN_GROUPS)]
    x = [x_ref[r, :] for r in groups]
    h = [_rms(xn, gpre_ref[...]).astype(BF16) for xn in x]
    a = [(_silu(_dot(hn, wg_ref[...])) * _dot(hn, wu_ref[...])).astype(BF16) for hn in h]
    y = [_dot(an, wd_ref[...]) for an in a]
    for n, r in enumerate(groups):
        out = x[n] + 0.5 * _rms(y[n], gpost_ref[...])
        if final:
            out = _rms(out, gfin_ref[...])
        o_ref[r, :] = out


def _ffn(x, gpre, wg, wu, wd, gpost, gfin, final):
    t = x.shape[0]
    return pl.pallas_call(
        functools.partial(_ffn_kernel, final=final),
        grid=(t // TM,),
        in_specs=[
            pl.BlockSpec((TM, D_MODEL), lambda i: (i, 0)),
            _const_spec((1, D_MODEL)),
            _const_spec((D_MODEL, D_FF)),
            _const_spec((D_MODEL, D_FF)),
            _const_spec((D_FF, D_MODEL)),
            _const_spec((1, D_MODEL)),
            _const_spec((1, D_MODEL)),
        ],
        out_specs=pl.BlockSpec((TM, D_MODEL), lambda i: (i, 0)),
        out_shape=jax.ShapeDtypeStruct((t, D_MODEL), F32),
        compiler_params=pltpu.CompilerParams(
            dimension_semantics=("arbitrary",), vmem_limit_bytes=VMEM_LIMIT),
        name="ffn_final" if final else "ffn",
    )(x, gpre, wg, wu, wd, gpost, gfin)


def _mix_in_kernel(x_ref, g_ref, wk_ref, wqt_ref, wvt_ref, wgla_ref, wgr_ref, wsm_ref,
                   bf_ref, pk_ref, pqt_ref, wg2_ref, bg_ref,
                   k_ref, qt_ref, vt_ref, gqk_ref, gv_ref, la_ref, sr_ref,
                   carry_ref, *, tiles_per_seq):
    i = pl.program_id(0)

    @pl.when(i % tiles_per_seq == 0)
    def _():
        carry_ref[...] = jnp.zeros_like(carry_ref)

    h = _rms(x_ref[...], g_ref[...]).astype(BF16)
    gl = _dot(h, wgla_ref[...])
    gqk_ref[...] = gl[:, :2 * GLA_QK].astype(BF16)
    gv_ref[...] = gl[:, 2 * GLA_QK:].astype(BF16)
    sr_ref[...] = _silu(_dot(h, wgr_ref[...])).astype(BF16)

    sm = _dot(h, wsm_ref[...])

    lane = lax.broadcasted_iota(jnp.int32, (1, LANES), 1)
    lf = jnp.where(lane < FOX_HEADS, _log_sigmoid(sm + bf_ref[...]), 0.0)
    row = lax.broadcasted_iota(jnp.int32, (TM, TM), 0)
    col = lax.broadcasted_iota(jnp.int32, (TM, TM), 1)
    tri = jnp.where(col <= row, 1.0, 0.0).astype(BF16)
    c3 = _dot(tri, _pack_terms(lf))
    c = jnp.where(lane < FOX_HEADS,
                  c3 + pltpu.roll(c3, LANES - FOX_HEADS, 1) + pltpu.roll(c3, LANES - 2 * FOX_HEADS, 1),
                  0.0) + carry_ref[...]
    carry_ref[...] = c[TM - 1:TM, :]
    packed = _pack_terms(c * LOG2E, one_lane=ONE_LANE)

    kp = _dot(h, wk_ref[...])
    kaug = _dot(packed, pk_ref[...])
    low = lane < FOX_HEAD_DIM
    for hd in range(FOX_HEADS):
        pair = slice((hd // 2) * LANES, (hd // 2 + 1) * LANES)
        slot = slice(hd * LANES, (hd + 1) * LANES)
        own = low if hd % 2 == 0 else jnp.logical_not(low)
        k_ref[:, slot] = jnp.where(own, kp[:, pair], kaug[:, slot]).astype(BF16)

    qt = _dot_nt(wqt_ref[...], h).astype(BF16)
    aug = _dot_nt(pqt_ref[...], packed).astype(BF16)
    vt = _dot_nt(wvt_ref[...], h).astype(BF16)
    one_rows = jnp.where(lax.broadcasted_iota(jnp.int32, (AUG_ROWS, TQ), 0) == 0,
                         1.0, 0.0).astype(BF16)
    zeros = jnp.zeros((LANES - FOX_HEAD_DIM - AUG_ROWS, TQ), BF16)
    for n in range(TM // TQ):
        cs = slice(n * TQ, (n + 1) * TQ)
        for hd in range(FOX_HEADS):
            q_h = qt[hd * FOX_HEAD_DIM:(hd + 1) * FOX_HEAD_DIM, cs]
            a_h = aug[hd * AUG_ROWS:(hd + 1) * AUG_ROWS, cs]
            base = hd * LANES
            pieces = (q_h, a_h, zeros) if hd % 2 == 0 else (a_h, zeros, q_h)
            r = base
            for piece in pieces:
                qt_ref[n, r:r + piece.shape[0], :] = piece
                r += piece.shape[0]
            vb = hd * V_ROWS
            vt_ref[n, vb:vb + FOX_HEAD_DIM, :] = vt[hd * FOX_HEAD_DIM:(hd + 1) * FOX_HEAD_DIM, cs]
            vt_ref[n, vb + FOX_HEAD_DIM:vb + V_ROWS, :] = one_rows

    g_hi, g_mid, _ = _split3(sm)
    w_hi, w_mid, _ = _split3(wg2_ref[...])
    gate = _dot(g_hi, w_hi) + _dot(g_mid, w_hi) + _dot(g_hi, w_mid) + bg_ref[...]
    la_ref[...] = _log_sigmoid(gate) / GLA_GATE_NORM


def _mix_in(x, g, wk, wqt, wvt, wgla, wgr, wsm, bf, pk, pqt, wg2, bg, seq):
    t = x.shape[0]
    row = lambda w: pl.BlockSpec((TM, w), lambda i: (i, 0))
    tblk = lambda rows: pl.BlockSpec((TM // TQ, rows, TQ), lambda i: (i, 0, 0))
    return pl.pallas_call(
        functools.partial(_mix_in_kernel, tiles_per_seq=seq // TM),
        grid=(t // TM,),
        in_specs=[
            row(D_MODEL),
            _const_spec((1, D_MODEL)),
            _const_spec((D_MODEL, FOX_WIDTH)),
            _const_spec((FOX_WIDTH, D_MODEL)),
            _const_spec((FOX_WIDTH, D_MODEL)),
            _const_spec((D_MODEL, 2 * GLA_QK + GLA_WIDTH)),
            _const_spec((D_MODEL, GLA_WIDTH)),
            _const_spec((D_MODEL, LANES)),
            _const_spec((1, LANES)),
            _const_spec((LANES, FOX_PAD)),
            _const_spec((FOX_HEADS * AUG_ROWS, LANES)),
            _const_spec((LANES, GLA_QK)),
            _const_spec((1, GLA_QK)),
        ],
        out_specs=[row(FOX_PAD), tblk(FOX_PAD), tblk(FOX_HEADS * V_ROWS), row(2 * GLA_QK),
                   row(GLA_WIDTH), row(GLA_QK), row(GLA_WIDTH)],
        out_shape=[
            jax.ShapeDtypeStruct((t, FOX_PAD), BF16),
            jax.ShapeDtypeStruct((t // TQ, FOX_PAD, TQ), BF16),
            jax.ShapeDtypeStruct((t // TQ, FOX_HEADS * V_ROWS, TQ), BF16),
            jax.ShapeDtypeStruct((t, 2 * GLA_QK), BF16),
            jax.ShapeDtypeStruct((t, GLA_WIDTH), BF16),
            jax.ShapeDtypeStruct((t, GLA_QK), F32),
            jax.ShapeDtypeStruct((t, GLA_WIDTH), BF16),
        ],
        scratch_shapes=[pltpu.VMEM((1, LANES), F32)],
        compiler_params=pltpu.CompilerParams(
            dimension_semantics=("arbitrary",), vmem_limit_bytes=VMEM_LIMIT),
        name="mix_in",
    )(x, g, wk, wqt, wvt, wgla, wgr, wsm, bf, pk, pqt, wg2, bg)


def _fox_kernel(qt_ref, k_ref, vt_ref, g_ref, o_ref, m_ref, acc_ref, sa_ref, sb_ref):
    i = pl.program_id(1)
    m_ref[...] = jnp.full(m_ref.shape, -jnp.inf, F32)
    acc_ref[...] = jnp.zeros(acc_ref.shape, F32)
    key = lax.broadcasted_iota(jnp.int32, (TQ, TQ), 0)
    qry = lax.broadcasted_iota(jnp.int32, (TQ, TQ), 1)
    causal = key <= qry
    slots = [slice(h * LANES, (h + 1) * LANES) for h in range(FOX_HEADS)]
    vrows = [slice(h * V_ROWS, (h + 1) * V_ROWS) for h in range(FOX_HEADS)]

    def scores(j, s_ref):
        start = pl.multiple_of(j * TQ, TQ)
        for h in range(FOX_HEADS):
            s_ref[h] = _dot(k_ref[pl.ds(start, TQ), slots[h]], qt_ref[0, slots[h], :])

    def accumulate(j, s_ref, diagonal):
        probs, alphas = [], []
        for h in range(FOX_HEADS):
            s = s_ref[h]
            if diagonal:
                s = jnp.where(causal, s, -jnp.inf)
            m_old = m_ref[h:h + 1, :]
            m_new = jnp.maximum(m_old, jnp.max(s, axis=0, keepdims=True))
            alphas.append(jnp.exp2(m_old - m_new))
            probs.append(jnp.exp2(s - m_new).astype(BF16))
            m_ref[h:h + 1, :] = m_new
        for h in range(FOX_HEADS):
            acc_ref[h] = alphas[h] * acc_ref[h] + _dot(vt_ref[j, vrows[h], :], probs[h])

    scores(0, sa_ref)

    def pair(t, carry):
        j = 2 * t
        scores(j + 1, sb_ref)
        accumulate(j, sa_ref, False)
        scores(j + 2, sa_ref)
        accumulate(j + 1, sb_ref, False)
        return carry

    lax.fori_loop(0, i // 2, pair, 0)

    @pl.when(i % 2 == 0)
    def _():
        accumulate(i, sa_ref, True)

    @pl.when(i % 2 == 1)
    def _():
        scores(i, sb_ref)
        accumulate(i - 1, sa_ref, False)
        accumulate(i, sb_ref, True)

    for h in range(FOX_HEADS):
        acc = acc_ref[h]
        o = acc[:FOX_HEAD_DIM, :] / acc[V_ONE_ROW:V_ONE_ROW + 1, :]
        ms = jnp.mean(o * o, axis=0, keepdims=True)
        rows = slice(h * FOX_HEAD_DIM, (h + 1) * FOX_HEAD_DIM)
        o_ref[rows, :] = (o * lax.rsqrt(ms + RMS_EPS) * g_ref[rows, :]).astype(BF16)


def _fox(qt, k, vt, g, batch, seq):
    nq = seq // TQ
    return pl.pallas_call(
        _fox_kernel,
        grid=(batch, nq),
        in_specs=[
            pl.BlockSpec((1, FOX_PAD, TQ), lambda b, i: (b * nq + i, 0, 0)),
            pl.BlockSpec((seq, FOX_PAD), lambda b, i: (b, 0)),
            pl.BlockSpec((nq, FOX_HEADS * V_ROWS, TQ), lambda b, i: (b, 0, 0)),
            _const_spec((FOX_WIDTH, TQ)),
        ],
        out_specs=pl.BlockSpec((FOX_WIDTH, TQ), lambda b, i: (0, b * nq + i)),
        out_shape=jax.ShapeDtypeStruct((FOX_WIDTH, batch * seq), BF16),
        scratch_shapes=[pltpu.VMEM((FOX_HEADS, TQ), F32),
                        pltpu.VMEM((FOX_HEADS, V_ROWS, TQ), F32),
                        pltpu.VMEM((FOX_HEADS, TQ, TQ), F32),
                        pltpu.VMEM((FOX_HEADS, TQ, TQ), F32)],
        compiler_params=pltpu.CompilerParams(
            dimension_semantics=("arbitrary", "arbitrary"), vmem_limit_bytes=VMEM_LIMIT),
        name="fox",
    )(qt, k, vt, g)


def _gla_kernel(qk_ref, v_ref, la_ref, sr_ref, g_ref, o_ref, st_ref):
    @pl.when(pl.program_id(1) == 0)
    def _():
        st_ref[...] = jnp.zeros_like(st_ref)

    nchunk = TG // CHUNK
    seqs = range(GLA_SEQS)
    heads = range(GLA_HEADS)
    lane = lax.broadcasted_iota(jnp.int32, (1, LANES), 1)
    first = lane < GLA_KEY_DIM
    row = lax.broadcasted_iota(jnp.int32, (TG, TG), 0)
    col = lax.broadcasted_iota(jnp.int32, (TG, TG), 1)
    causal = (row - col).astype(jnp.uint32) <= (row & (CHUNK - 1)).astype(jnp.uint32)
    tri = jnp.where(causal, 1.0, 0.0).astype(BF16)
    scale = GLA_KEY_DIM ** -0.5
    zero = jnp.zeros((TG, LANES), BF16)
    chunks = [slice(c * CHUNK, (c + 1) * CHUNK) for c in range(nchunk)]
    prs = [slice((h // 2) * LANES, (h // 2 + 1) * LANES) for h in heads]
    cols = [slice(h * GLA_VAL_DIM, (h + 1) * GLA_VAL_DIM) for h in heads]
    hmasks = [first if h % 2 == 0 else jnp.logical_not(first) for h in heads]

    b = [_tri_cumsum(tri, la_ref[n]) for n in seqs]
    qd, kr, vh, k_inv, decay3 = [], [], [], [], []
    for n in seqs:
        b_last3 = b[n].reshape(nchunk, CHUNK, GLA_QK)[:, CHUNK - 1:CHUNK, :]
        b_last = jnp.broadcast_to(b_last3, (nchunk, CHUNK, GLA_QK)).reshape(TG, GLA_QK)
        q = qk_ref[n, :, 0:GLA_QK].astype(F32)
        k = qk_ref[n, :, GLA_QK:2 * GLA_QK].astype(F32)
        q_dec = (q * scale * jnp.exp(b[n])).astype(BF16)
        k_rem = (k * jnp.exp(b_last - b[n])).astype(BF16)
        k_inv.append((k * jnp.exp(-b[n])).astype(BF16))
        decay3.append(jnp.exp(b_last3))
        qd.append([jnp.where(hmasks[h], q_dec[:, prs[h]], zero) for h in heads])
        kr.append([jnp.where(hmasks[h], k_rem[:, prs[h]], zero) for h in heads])
        vh.append([v_ref[n, :, cols[h]] for h in heads])
    kv, att = [], []
    for n in seqs:
        kv.append([[_dot_tn(vh[n][h][r], kr[n][h][r]) for r in chunks] for h in heads])
        att.append([jnp.where(causal, _dot_nt(qd[n][h], k_inv[n][:, prs[h]]), 0.0).astype(BF16)
                    for h in heads])
    states = []
    for n in seqs:
        per_head = []
        for h in heads:
            st = st_ref[n, h]
            per_chunk = []
            for c in range(nchunk):
                per_chunk.append(st.astype(BF16))
                st = st * decay3[n][c][:, prs[h]] + kv[n][h][c]
            st_ref[n, h] = st
            per_head.append(per_chunk)
        states.append(per_head)
    for n in seqs:
        for h in heads:
            inter = [_dot_nt(qd[n][h][chunks[c]], states[n][h][c]) for c in range(nchunk)]
            o = _dot(att[n][h], vh[n][h]) + jnp.concatenate(inter, axis=0)
            ms = jnp.mean(o * o, axis=-1, keepdims=True)
            y = o * lax.rsqrt(ms + RMS_EPS) * g_ref[:, cols[h]] * sr_ref[n, :, cols[h]].astype(F32)
            o_ref[n, :, cols[h]] = y.astype(BF16)


def _gla(gqk, gv, la, sr, g, batch, seq):
    blk = lambda w: pl.BlockSpec((GLA_SEQS, TG, w), lambda b, i: (b, i, 0))
    return pl.pallas_call(
        _gla_kernel,
        grid=(batch // GLA_SEQS, seq // TG),
        in_specs=[blk(2 * GLA_QK), blk(GLA_WIDTH), blk(GLA_QK), blk(GLA_WIDTH),
                  pl.BlockSpec((1, GLA_WIDTH), lambda b, i: (0, 0))],
        out_specs=blk(GLA_WIDTH),
        out_shape=jax.ShapeDtypeStruct((batch, seq, GLA_WIDTH), BF16),
        scratch_shapes=[pltpu.VMEM((GLA_SEQS, GLA_HEADS, GLA_VAL_DIM, LANES), F32)],
        compiler_params=pltpu.CompilerParams(
            dimension_semantics=("arbitrary", "arbitrary"), vmem_limit_bytes=VMEM_LIMIT),
        name="gla",
    )(gqk, gv, la, sr, g)


def _mem_kv_kernel(m_ref, g_ref, w_ref, o_ref):
    h = _rms(m_ref[...], g_ref[...]).astype(BF16)
    o_ref[...] = _dot(h, w_ref[...]).astype(BF16)


def _mem_kv(mem, g, w, mem_len):
    t = mem.shape[0]
    width = w.shape[1]
    return pl.pallas_call(
        _mem_kv_kernel,
        grid=(t // mem_len,),
        in_specs=[pl.BlockSpec((mem_len, D_MODEL), lambda i: (i, 0)),
                  _const_spec((1, D_MODEL)), _const_spec((D_MODEL, width))],
        out_specs=pl.BlockSpec((mem_len, width), lambda i: (i, 0)),
        out_shape=jax.ShapeDtypeStruct((t, width), BF16),
        compiler_params=pltpu.CompilerParams(
            dimension_semantics=("arbitrary",), vmem_limit_bytes=VMEM_LIMIT),
        name="mem_kv",
    )(mem, g, w)


def _mix_out_mem_kernel(x_ref, oft_ref, og_ref, kv_ref, wout_ref, gpost_ref, gpre_ref,
                        wq_ref, wo_ref, gmpost_ref, o_ref):
    width = MEM_HEADS * MEM_HEAD_DIM
    groups = [slice(n * TMO // MIX_GROUPS, (n + 1) * TMO // MIX_GROUPS) for n in range(MIX_GROUPS)]
    y = [_dot_tn(oft_ref[:, r], wout_ref[0:FOX_WIDTH, :]) + _dot(og_ref[r, :], wout_ref[FOX_WIDTH:, :])
         for r in groups]
    x = [x_ref[r, :] + _rms(y[n], gpost_ref[...]) for n, r in enumerate(groups)]
    h = [_rms(xn, gpre_ref[...]).astype(BF16) for xn in x]
    q = [(_dot(hn, wq_ref[...]) * (MEM_HEAD_DIM ** -0.5)).astype(BF16) for hn in h]
    o = []
    for qn in q:
        outs = []
        for hd in range(MEM_HEADS):
            cols = slice(hd * MEM_HEAD_DIM, (hd + 1) * MEM_HEAD_DIM)
            s = _dot_nt(qn[:, cols], kv_ref[:, cols])
            m = jnp.max(s, axis=-1, keepdims=True)
            p = jnp.exp(s - m)
            p = p / jnp.sum(p, axis=-1, keepdims=True)
            vcols = slice(width + hd * MEM_HEAD_DIM, width + (hd + 1) * MEM_HEAD_DIM)
            outs.append(_dot(p.astype(BF16), kv_ref[:, vcols]).astype(BF16))
        o.append(jnp.concatenate(outs, axis=-1))
    y2 = [_dot(on, wo_ref[...]) for on in o]
    for n, r in enumerate(groups):
        o_ref[r, :] = x[n] + _rms(y2[n], gmpost_ref[...])


def _mix_out_mem(x, o_fox_t, o_gla, kv, wout, gpost, gpre, wq, wo, gmpost, seq, mem_len):
    t = x.shape[0]
    tiles_per_seq = seq // TMO
    row = lambda w: pl.BlockSpec((TMO, w), lambda i: (i, 0))
    return pl.pallas_call(
        _mix_out_mem_kernel,
        grid=(t // TMO,),
        in_specs=[
            row(D_MODEL),
            pl.BlockSpec((FOX_WIDTH, TMO), lambda i: (0, i)),
            row(GLA_WIDTH),
            pl.BlockSpec((mem_len, kv.shape[1]), lambda i: (i // tiles_per_seq, 0)),
            _const_spec((D_MODEL, D_MODEL)), _const_spec((1, D_MODEL)), _const_spec((1, D_MODEL)),
            _const_spec((D_MODEL, D_MODEL)), _const_spec((D_MODEL, D_MODEL)),
            _const_spec((1, D_MODEL)),
        ],
        out_specs=row(D_MODEL),
        out_shape=jax.ShapeDtypeStruct((t, D_MODEL), F32),
        compiler_params=pltpu.CompilerParams(
            dimension_semantics=("arbitrary",), vmem_limit_bytes=VMEM_LIMIT),
        name="mix_out_mem",
    )(x, o_fox_t, o_gla, kv, wout, gpost, gpre, wq, wo, gmpost)


def _pad_cols(w, width):
    return jnp.pad(w, ((0, 0), (0, width - w.shape[1])))


def _aug_base(h):
    return FOX_HEAD_DIM if h % 2 == 0 else 0


def _placement():
    pk = np.zeros((LANES, FOX_PAD), np.float32)
    pq = np.zeros((FOX_HEADS * AUG_ROWS, LANES), np.float32)
    for h in range(FOX_HEADS):
        for t in range(C_TERMS):
            src = t * FOX_HEADS + h
            pq[h * AUG_ROWS + t, src] = 1.0
            pk[ONE_LANE, h * LANES + _aug_base(h) + t] = 1.0
            pq[h * AUG_ROWS + C_TERMS + t, ONE_LANE] = 1.0
            pk[src, h * LANES + _aug_base(h) + C_TERMS + t] = -1.0
    return jnp.asarray(pk, BF16), jnp.asarray(pq, BF16)


def kernel(x, mem, g_ff1_pre, w_ff1_gate, w_ff1_up, w_ff1_down, g_ff1_post, g_mix_pre, w_mix_in, b_fox_f, w_gla_g2, b_gla_g, g_fox_out, g_gla_out, w_mix_out, g_mix_post, g_mem_pre, g_mem_src, w_mem_q, w_mem_kv, w_mem_o, g_mem_post, g_ff2_pre, w_ff2_gate, w_ff2_up, w_ff2_down, g_ff2_post, g_final):
    batch, seq, d = x.shape
    mem_len = mem.shape[1]
    depth = w_ff1_gate.shape[0]
    xt = x.reshape(batch * seq, d)
    memt = mem.reshape(batch * mem_len, d)
    gfin = g_final.reshape(1, d)
    vec = lambda v: v.reshape(1, -1).astype(F32)
    pk, pqt = _placement()
    for l in range(depth):
        last = l == depth - 1
        xt = _ffn(xt, vec(g_ff1_pre[l]), w_ff1_gate[l].astype(BF16), w_ff1_up[l].astype(BF16),
                  w_ff1_down[l].astype(BF16), vec(g_ff1_post[l]), gfin, False)

        w_in = w_mix_in[l]
        wqt = (w_in[:, OFF_FQ:OFF_FK] * (LOG2E * FOX_HEAD_DIM ** -0.5)).T.astype(BF16)
        wk = w_in[:, OFF_FK:OFF_FV].astype(BF16)
        wvt = w_in[:, OFF_FV:OFF_FF].T.astype(BF16)
        wgla = w_in[:, OFF_GQ:OFF_GG].astype(BF16)
        wgr = w_in[:, OFF_GR:IN_WIDTH].astype(BF16)
        wsm = _pad_cols(jnp.concatenate([w_in[:, OFF_FF:OFF_GQ], w_in[:, OFF_GG:OFF_GR]], axis=1),
                        LANES).astype(BF16)
        bf = _pad_cols(vec(b_fox_f[l]), LANES)
        wg2 = jnp.pad(w_gla_g2[l].astype(F32),
                      ((FOX_HEADS, LANES - FOX_HEADS - GLA_GATE_RANK), (0, 0)))
        k, qt, vt, gqk, gv, la, sr = _mix_in(xt, vec(g_mix_pre[l]), wk, wqt, wvt, wgla, wgr, wsm,
                                             bf, pk, pqt, wg2, vec(b_gla_g[l]), seq)

        g_fox = jnp.broadcast_to(g_fox_out[l].astype(F32)[:, None], (FOX_WIDTH, TQ))
        o_fox_t = _fox(qt, k, vt, g_fox, batch, seq)
        bsd = lambda a: a.reshape(batch, seq, a.shape[-1])
        o_gla = _gla(bsd(gqk), bsd(gv), bsd(la), bsd(sr), vec(g_gla_out[l]), batch, seq)
        o_gla = o_gla.reshape(batch * seq, GLA_WIDTH)

        kv = _mem_kv(memt, vec(g_mem_src[l]), w_mem_kv[l].astype(BF16), mem_len)
        xt = _mix_out_mem(xt, o_fox_t, o_gla, kv, w_mix_out[l].astype(BF16), vec(g_mix_post[l]),
                          vec(g_mem_pre[l]), w_mem_q[l].astype(BF16), w_mem_o[l].astype(BF16),
                          vec(g_mem_post[l]), seq, mem_len)

        xt = _ffn(xt, vec(g_ff2_pre[l]), w_ff2_gate[l].astype(BF16), w_ff2_up[l].astype(BF16),
                  w_ff2_down[l].astype(BF16), vec(g_ff2_post[l]), gfin, last)
    return xt.reshape(batch, seq, d)
```

```python
import functools

import numpy as np
import jax
import jax.numpy as jnp
from jax import lax
from jax.experimental import pallas as pl
from jax.experimental.pallas import tpu as pltpu

F32 = jnp.float32
BF16 = jnp.bfloat16

D_MODEL = 1024
CHUNK = 64
FOX_WIDTH = 512
FOX_HEADS = 8
FOX_HEAD_DIM = 64
GLA_WIDTH = 512
GLA_HEADS = 4
GLA_VAL_DIM = 128
GLA_KEY_DIM = 64
GLA_QK = 256
GLA_GATE_RANK = 16
GLA_GATE_NORM = 16.0
MEM_HEADS = 4
MEM_HEAD_DIM = 256
D_FF = 2816
RMS_EPS = 1e-6

OFF_FQ = 0
OFF_FK = OFF_FQ + FOX_WIDTH
OFF_FV = OFF_FK + FOX_WIDTH
OFF_FF = OFF_FV + FOX_WIDTH
OFF_GQ = OFF_FF + FOX_HEADS
OFF_GK = OFF_GQ + GLA_QK
OFF_GV = OFF_GK + GLA_QK
OFF_GG = OFF_GV + GLA_WIDTH
OFF_GR = OFF_GG + GLA_GATE_RANK
IN_WIDTH = OFF_GR + GLA_WIDTH

LANES = 128
VMEM_LIMIT = 56 * 1024 * 1024

TM = 512
TQ = 256
TG = 512
GLA_SEQS = 2
SCORE_LEAD = 1
FFN_GROUPS = 2
TMO = 1024
MIX_GROUPS = 2

FOX_PAD = FOX_HEADS * LANES
C_TERMS = 3
ONE_LANE = C_TERMS * FOX_HEADS
AUG_ROWS = 16
V_ONE_ROW = FOX_HEAD_DIM
V_ROWS = FOX_HEAD_DIM + AUG_ROWS
LOG2E = 1.4426950408889634


def _rms(x, g):
    return x * lax.rsqrt(jnp.mean(x * x, axis=-1, keepdims=True) + RMS_EPS) * g


def _dot(a, b):
    return jnp.dot(a, b, preferred_element_type=F32)


def _dot_nt(a, b):
    return lax.dot_general(a, b, (((1,), (1,)), ((), ())), preferred_element_type=F32)


def _dot_tn(a, b):
    return lax.dot_general(a, b, (((0,), (0,)), ((), ())), preferred_element_type=F32)


def _split3(x):
    hi = x.astype(BF16)
    r1 = x - hi.astype(F32)
    mid = r1.astype(BF16)
    lo = (r1 - mid.astype(F32)).astype(BF16)
    return hi, mid, lo


def _tri_cumsum(tri_bf16, x):
    hi, mid, lo = _split3(x)
    return _dot(tri_bf16, hi) + _dot(tri_bf16, mid) + _dot(tri_bf16, lo)


def _pack_terms(x, one_lane=None):
    hi, mid, lo = _split3(x)
    out = (hi.astype(F32) + pltpu.roll(mid.astype(F32), FOX_HEADS, 1)
           + pltpu.roll(lo.astype(F32), 2 * FOX_HEADS, 1))
    if one_lane is not None:
        lane = lax.broadcasted_iota(jnp.int32, (1, LANES), 1)
        out = out + jnp.where(lane == one_lane, 1.0, 0.0)
    return out.astype(BF16)


def _log_sigmoid(x):
    return jnp.minimum(x, 0.0) - jnp.log1p(jnp.exp(-jnp.abs(x)))


def _silu(x):
    return x * jax.nn.sigmoid(x)


def _const_spec(shape):
    return pl.BlockSpec(shape, lambda *_: (0,) * len(shape), pipeline_mode=pl.Buffered(1))


def _ffn_kernel(x_ref, gpre_ref, wg_ref, wu_ref, wd_ref, gpost_ref, gfin_ref, o_ref, *, final):
    groups = [slice(n * TM // FFN_GROUPS, (n + 1) * TM // FFN_GROUPS) for n in range(FFN_GROUPS)]
    x = [x_ref[r, :] for r in groups]
    h = [_rms(xn, gpre_ref[...]).astype(BF16) for xn in x]
    a = [(_silu(_dot(hn, wg_ref[...])) * _dot(hn, wu_ref[...])).astype(BF16) for hn in h]
    y = [_dot(an, wd_ref[...]) for an in a]
    for n, r in enumerate(groups):
        out = x[n] + 0.5 * _rms(y[n], gpost_ref[...])
        if final:
            out = _rms(out, gfin_ref[...])
        o_ref[r, :] = out


def _ffn(x, gpre, wg, wu, wd, gpost, gfin, final):
    t = x.shape[0]
    return pl.pallas_call(
        functools.partial(_ffn_kernel, final=final),
        grid=(t // TM,),
        in_specs=[
            pl.BlockSpec((TM, D_MODEL), lambda i: (i, 0)),
            _const_spec((1, D_MODEL)),
            _const_spec((D_MODEL, D_FF)),
            _const_spec((D_MODEL, D_FF)),
            _const_spec((D_FF, D_MODEL)),
            _const_spec((1, D_MODEL)),
            _const_spec((1, D_MODEL)),
        ],
        out_specs=pl.BlockSpec((TM, D_MODEL), lambda i: (i, 0)),
        out_shape=jax.ShapeDtypeStruct((t, D_MODEL), F32),
        compiler_params=pltpu.CompilerParams(
            dimension_semantics=("arbitrary",), vmem_limit_bytes=VMEM_LIMIT),
        name="ffn_final" if final else "ffn",
    )(x, gpre, wg, wu, wd, gpost, gfin)


def _mix_in_kernel(x_ref, g_ref, wk_ref, wqt_ref, wvt_ref, wgla_ref, wgr_ref, wsm_ref,
                   bf_ref, pk_ref, pqt_ref, wg2_ref, bg_ref,
                   k_ref, qt_ref, vt_ref, gqk_ref, gv_ref, la_ref, sr_ref,
                   carry_ref, *, tiles_per_seq):
    i = pl.program_id(0)

    @pl.when(i % tiles_per_seq == 0)
    def _():
        carry_ref[...] = jnp.zeros_like(carry_ref)

    h = _rms(x_ref[...], g_ref[...]).astype(BF16)
    gl = _dot(h, wgla_ref[...])
    gqk_ref[...] = gl[:, :2 * GLA_QK].astype(BF16)
    gv_ref[...] = gl[:, 2 * GLA_QK:].astype(BF16)
    sr_ref[...] = _silu(_dot(h, wgr_ref[...])).astype(BF16)

    sm = _dot(h, wsm_ref[...])

    lane = lax.broadcasted_iota(jnp.int32, (1, LANES), 1)
    lf = jnp.where(lane < FOX_HEADS, _log_sigmoid(sm + bf_ref[...]), 0.0)
    row = lax.broadcasted_iota(jnp.int32, (TM, TM), 0)
    col = lax.broadcasted_iota(jnp.int32, (TM, TM), 1)
    tri = jnp.where(col <= row, 1.0, 0.0).astype(BF16)
    c3 = _dot(tri, _pack_terms(lf))
    c = jnp.where(lane < FOX_HEADS,
                  c3 + pltpu.roll(c3, LANES - FOX_HEADS, 1) + pltpu.roll(c3, LANES - 2 * FOX_HEADS, 1),
                  0.0) + carry_ref[...]
    carry_ref[...] = c[TM - 1:TM, :]
    packed = _pack_terms(c * LOG2E, one_lane=ONE_LANE)

    kp = _dot(h, wk_ref[...])
    kaug = _dot(packed, pk_ref[...])
    low = lane < FOX_HEAD_DIM
    for hd in range(FOX_HEADS):
        pair = slice((hd // 2) * LANES, (hd // 2 + 1) * LANES)
        slot = slice(hd * LANES, (hd + 1) * LANES)
        own = low if hd % 2 == 0 else jnp.logical_not(low)
        k_ref[:, slot] = jnp.where(own, kp[:, pair], kaug[:, slot]).astype(BF16)

    qt = _dot_nt(wqt_ref[...], h).astype(BF16)
    aug = _dot_nt(pqt_ref[...], packed).astype(BF16)
    vt = _dot_nt(wvt_ref[...], h).astype(BF16)
    one_rows = jnp.where(lax.broadcasted_iota(jnp.int32, (AUG_ROWS, TQ), 0) == 0,
                         1.0, 0.0).astype(BF16)
    zeros = jnp.zeros((LANES - FOX_HEAD_DIM - AUG_ROWS, TQ), BF16)
    for n in range(TM // TQ):
        cs = slice(n * TQ, (n + 1) * TQ)
        for hd in range(FOX_HEADS):
            q_h = qt[hd * FOX_HEAD_DIM:(hd + 1) * FOX_HEAD_DIM, cs]
            a_h = aug[hd * AUG_ROWS:(hd + 1) * AUG_ROWS, cs]
            base = hd * LANES
            pieces = (q_h, a_h, zeros) if hd % 2 == 0 else (a_h, zeros, q_h)
            r = base
            for piece in pieces:
                qt_ref[n, r:r + piece.shape[0], :] = piece
                r += piece.shape[0]
            vb = hd * V_ROWS
            vt_ref[n, vb:vb + FOX_HEAD_DIM, :] = vt[hd * FOX_HEAD_DIM:(hd + 1) * FOX_HEAD_DIM, cs]
            vt_ref[n, vb + FOX_HEAD_DIM:vb + V_ROWS, :] = one_rows

    g_hi, g_mid, _ = _split3(sm)
    w_hi, w_mid, _ = _split3(wg2_ref[...])
    gate = _dot(g_hi, w_hi) + _dot(g_mid, w_hi) + _dot(g_hi, w_mid) + bg_ref[...]
    la_ref[...] = _log_sigmoid(gate) / GLA_GATE_NORM


def _mix_in(x, g, wk, wqt, wvt, wgla, wgr, wsm, bf, pk, pqt, wg2, bg, seq):
    t = x.shape[0]
    row = lambda w: pl.BlockSpec((TM, w), lambda i: (i, 0))
    tblk = lambda rows: pl.BlockSpec((TM // TQ, rows, TQ), lambda i: (i, 0, 0))
    return pl.pallas_call(
        functools.partial(_mix_in_kernel, tiles_per_seq=seq // TM),
        grid=(t // TM,),
        in_specs=[
            row(D_MODEL),
            _const_spec((1, D_MODEL)),
            _const_spec((D_MODEL, FOX_WIDTH)),
            _const_spec((FOX_WIDTH, D_MODEL)),
            _const_spec((FOX_WIDTH, D_MODEL)),
            _const_spec((D_MODEL, 2 * GLA_QK + GLA_WIDTH)),
            _const_spec((D_MODEL, GLA_WIDTH)),
            _const_spec((D_MODEL, LANES)),
            _const_spec((1, LANES)),
            _const_spec((LANES, FOX_PAD)),
            _const_spec((FOX_HEADS * AUG_ROWS, LANES)),
            _const_spec((LANES, GLA_QK)),
            _const_spec((1, GLA_QK)),
        ],
        out_specs=[row(FOX_PAD), tblk(FOX_PAD), tblk(FOX_HEADS * V_ROWS), row(2 * GLA_QK),
                   row(GLA_WIDTH), row(GLA_QK), row(GLA_WIDTH)],
        out_shape=[
            jax.ShapeDtypeStruct((t, FOX_PAD), BF16),
            jax.ShapeDtypeStruct((t // TQ, FOX_PAD, TQ), BF16),
            jax.ShapeDtypeStruct((t // TQ, FOX_HEADS * V_ROWS, TQ), BF16),
            jax.ShapeDtypeStruct((t, 2 * GLA_QK), BF16),
            jax.ShapeDtypeStruct((t, GLA_WIDTH), BF16),
            jax.ShapeDtypeStruct((t, GLA_QK), F32),
            jax.ShapeDtypeStruct((t, GLA_WIDTH), BF16),
        ],
        scratch_shapes=[pltpu.VMEM((1, LANES), F32)],
        compiler_params=pltpu.CompilerParams(
            dimension_semantics=("arbitrary",), vmem_limit_bytes=VMEM_LIMIT),
        name="mix_in",
    )(x, g, wk, wqt, wvt, wgla, wgr, wsm, bf, pk, pqt, wg2, bg)


def _fox_kernel(qt_ref, k_ref, vt_ref, g_ref, o_ref, m_ref, acc_ref, sa_ref, sb_ref):
    i = pl.program_id(1)
    m_ref[...] = jnp.full(m_ref.shape, -jnp.inf, F32)
    acc_ref[...] = jnp.zeros(acc_ref.shape, F32)
    key = lax.broadcasted_iota(jnp.int32, (TQ, TQ), 0)
    qry = lax.broadcasted_iota(jnp.int32, (TQ, TQ), 1)
    causal = key <= qry
    slots = [slice(h * LANES, (h + 1) * LANES) for h in range(FOX_HEADS)]
    vrows = [slice(h * V_ROWS, (h + 1) * V_ROWS) for h in range(FOX_HEADS)]
    both = (0, 1)

    def score(j, s_ref, g, h):
        start = pl.multiple_of(j * TQ, TQ)
        s_ref[g, h] = _dot(k_ref[pl.ds(start, TQ), slots[h]], qt_ref[g, slots[h], :])

    def accumulate(j, s_ref, g, h, diagonal):
        s = s_ref[g, h]
        if diagonal:
            s = jnp.where(causal, s, -jnp.inf)
        m_old = m_ref[g, h:h + 1, :]
        m_new = jnp.maximum(m_old, jnp.max(s, axis=0, keepdims=True))
        alpha = jnp.exp2(m_old - m_new)
        p = jnp.exp2(s - m_new).astype(BF16)
        m_ref[g, h:h + 1, :] = m_new
        acc_ref[g, h] = alpha * acc_ref[g, h] + _dot(vt_ref[j, vrows[h], :], p)

    def stage(cur, nxt):
        todo = [(g, h) for g in nxt[2] for h in range(FOX_HEADS)] if nxt else []
        units = [(g, h) for g in cur[2] for h in range(FOX_HEADS)] if cur else []
        for g, h in todo[:SCORE_LEAD]:
            score(nxt[0], nxt[1], g, h)
        todo = todo[SCORE_LEAD:]
        per_unit = -(-len(todo) // len(units)) if units else 0
        for g, h in units:
            accumulate(cur[0], cur[1], g, h, g == cur[3])
            for g2, h2 in todo[:per_unit]:
                score(nxt[0], nxt[1], g2, h2)
            todo = todo[per_unit:]
        for g, h in todo:
            score(nxt[0], nxt[1], g, h)

    stage(None, (0, sa_ref, both))

    def pair(t, carry):
        j = 2 * t
        stage((j, sa_ref, both, None), (j + 1, sb_ref, both))
        stage((j + 1, sb_ref, both, None), (j + 2, sa_ref, both))
        return carry

    lax.fori_loop(0, i, pair, 0)
    stage((2 * i, sa_ref, both, 0), (2 * i + 1, sb_ref, (1,)))
    stage((2 * i + 1, sb_ref, (1,), 1), None)

    for g in both:
        for h in range(FOX_HEADS):
            acc = acc_ref[g, h]
            o = acc[:FOX_HEAD_DIM, :] / acc[V_ONE_ROW:V_ONE_ROW + 1, :]
            ms = jnp.mean(o * o, axis=0, keepdims=True)
            rows = slice(h * FOX_HEAD_DIM, (h + 1) * FOX_HEAD_DIM)
            o_ref[rows, g * TQ:(g + 1) * TQ] = (o * lax.rsqrt(ms + RMS_EPS)
                                                * g_ref[rows, :]).astype(BF16)


def _fox(qt, k, vt, g, batch, seq):
    nq = seq // TQ
    steps = nq // 2
    return pl.pallas_call(
        _fox_kernel,
        grid=(batch, steps),
        in_specs=[
            pl.BlockSpec((2, FOX_PAD, TQ), lambda b, i: (b * steps + i, 0, 0)),
            pl.BlockSpec((seq, FOX_PAD), lambda b, i: (b, 0)),
            pl.BlockSpec((nq, FOX_HEADS * V_ROWS, TQ), lambda b, i: (b, 0, 0)),
            _const_spec((FOX_WIDTH, TQ)),
        ],
        out_specs=pl.BlockSpec((FOX_WIDTH, 2 * TQ), lambda b, i: (0, b * steps + i)),
        out_shape=jax.ShapeDtypeStruct((FOX_WIDTH, batch * seq), BF16),
        scratch_shapes=[pltpu.VMEM((2, FOX_HEADS, TQ), F32),
                        pltpu.VMEM((2, FOX_HEADS, V_ROWS, TQ), F32),
                        pltpu.VMEM((2, FOX_HEADS, TQ, TQ), F32),
                        pltpu.VMEM((2, FOX_HEADS, TQ, TQ), F32)],
        compiler_params=pltpu.CompilerParams(
            dimension_semantics=("arbitrary", "arbitrary"), vmem_limit_bytes=VMEM_LIMIT),
        name="fox",
    )(qt, k, vt, g)


def _gla_kernel(qk_ref, v_ref, la_ref, sr_ref, g_ref, o_ref, st_ref):
    @pl.when(pl.program_id(1) == 0)
    def _():
        st_ref[...] = jnp.zeros_like(st_ref)

    nchunk = TG // CHUNK
    seqs = range(GLA_SEQS)
    heads = range(GLA_HEADS)
    lane = lax.broadcasted_iota(jnp.int32, (1, LANES), 1)
    first = lane < GLA_KEY_DIM
    row = lax.broadcasted_iota(jnp.int32, (TG, TG), 0)
    col = lax.broadcasted_iota(jnp.int32, (TG, TG), 1)
    causal = (row - col).astype(jnp.uint32) <= (row & (CHUNK - 1)).astype(jnp.uint32)
    tri = jnp.where(causal, 1.0, 0.0).astype(BF16)
    scale = GLA_KEY_DIM ** -0.5
    zero = jnp.zeros((TG, LANES), BF16)
    chunks = [slice(c * CHUNK, (c + 1) * CHUNK) for c in range(nchunk)]
    prs = [slice((h // 2) * LANES, (h // 2 + 1) * LANES) for h in heads]
    cols = [slice(h * GLA_VAL_DIM, (h + 1) * GLA_VAL_DIM) for h in heads]
    hmasks = [first if h % 2 == 0 else jnp.logical_not(first) for h in heads]

    b = [_tri_cumsum(tri, la_ref[n]) for n in seqs]
    qd, kr, vh, k_inv, decay3 = [], [], [], [], []
    for n in seqs:
        b_last3 = b[n].reshape(nchunk, CHUNK, GLA_QK)[:, CHUNK - 1:CHUNK, :]
        b_last = jnp.broadcast_to(b_last3, (nchunk, CHUNK, GLA_QK)).reshape(TG, GLA_QK)
        q = qk_ref[n, :, 0:GLA_QK].astype(F32)
        k = qk_ref[n, :, GLA_QK:2 * GLA_QK].astype(F32)
        q_dec = (q * scale * jnp.exp(b[n])).astype(BF16)
        k_rem = (k * jnp.exp(b_last - b[n])).astype(BF16)
        k_inv.append((k * jnp.exp(-b[n])).astype(BF16))
        decay3.append(jnp.exp(b_last3))
        qd.append([jnp.where(hmasks[h], q_dec[:, prs[h]], zero) for h in heads])
        kr.append([jnp.where(hmasks[h], k_rem[:, prs[h]], zero) for h in heads])
        vh.append([v_ref[n, :, cols[h]] for h in heads])
    kv, att = [], []
    for n in seqs:
        kv.append([[_dot_tn(vh[n][h][r], kr[n][h][r]) for r in chunks] for h in heads])
        att.append([jnp.where(causal, _dot_nt(qd[n][h], k_inv[n][:, prs[h]]), 0.0).astype(BF16)
                    for h in heads])
    states = []
    for n in seqs:
        per_head = []
        for h in heads:
            st = st_ref[n, h]
            per_chunk = []
            for c in range(nchunk):
                per_chunk.append(st.astype(BF16))
                st = st * decay3[n][c][:, prs[h]] + kv[n][h][c]
            st_ref[n, h] = st
            per_head.append(per_chunk)
        states.append(per_head)
    for n in seqs:
        for h in heads:
            inter = [_dot_nt(qd[n][h][chunks[c]], states[n][h][c]) for c in range(nchunk)]
            o = _dot(att[n][h], vh[n][h]) + jnp.concatenate(inter, axis=0)
            ms = jnp.mean(o * o, axis=-1, keepdims=True)
            y = o * lax.rsqrt(ms + RMS_EPS) * g_ref[:, cols[h]] * sr_ref[n, :, cols[h]].astype(F32)
            o_ref[n, :, cols[h]] = y.astype(BF16)


def _gla(gqk, gv, la, sr, g, batch, seq):
    blk = lambda w: pl.BlockSpec((GLA_SEQS, TG, w), lambda b, i: (b, i, 0))
    return pl.pallas_call(
        _gla_kernel,
        grid=(batch // GLA_SEQS, seq // TG),
        in_specs=[blk(2 * GLA_QK), blk(GLA_WIDTH), blk(GLA_QK), blk(GLA_WIDTH),
                  pl.BlockSpec((1, GLA_WIDTH), lambda b, i: (0, 0))],
        out_specs=blk(GLA_WIDTH),
        out_shape=jax.ShapeDtypeStruct((batch, seq, GLA_WIDTH), BF16),
        scratch_shapes=[pltpu.VMEM((GLA_SEQS, GLA_HEADS, GLA_VAL_DIM, LANES), F32)],
        compiler_params=pltpu.CompilerParams(
            dimension_semantics=("arbitrary", "arbitrary"), vmem_limit_bytes=VMEM_LIMIT),
        name="gla",
    )(gqk, gv, la, sr, g)


def _mem_kv_kernel(m_ref, g_ref, w_ref, o_ref):
    h = _rms(m_ref[...], g_ref[...]).astype(BF16)
    o_ref[...] = _dot(h, w_ref[...]).astype(BF16)


def _mem_kv(mem, g, w, mem_len):
    t = mem.shape[0]
    width = w.shape[1]
    return pl.pallas_call(
        _mem_kv_kernel,
        grid=(t // mem_len,),
        in_specs=[pl.BlockSpec((mem_len, D_MODEL), lambda i: (i, 0)),
                  _const_spec((1, D_MODEL)), _const_spec((D_MODEL, width))],
        out_specs=pl.BlockSpec((mem_len, width), lambda i: (i, 0)),
        out_shape=jax.ShapeDtypeStruct((t, width), BF16),
        compiler_params=pltpu.CompilerParams(
            dimension_semantics=("arbitrary",), vmem_limit_bytes=VMEM_LIMIT),
        name="mem_kv",
    )(mem, g, w)


def _mix_out_mem_kernel(x_ref, oft_ref, og_ref, kv_ref, wout_ref, gpost_ref, gpre_ref,
                        wq_ref, wo_ref, gmpost_ref, o_ref):
    width = MEM_HEADS * MEM_HEAD_DIM
    groups = [slice(n * TMO // MIX_GROUPS, (n + 1) * TMO // MIX_GROUPS) for n in range(MIX_GROUPS)]
    y = [_dot_tn(oft_ref[:, r], wout_ref[0:FOX_WIDTH, :]) + _dot(og_ref[r, :], wout_ref[FOX_WIDTH:, :])
         for r in groups]
    x = [x_ref[r, :] + _rms(y[n], gpost_ref[...]) for n, r in enumerate(groups)]
    h = [_rms(xn, gpre_ref[...]).astype(BF16) for xn in x]
    q = [(_dot(hn, wq_ref[...]) * (MEM_HEAD_DIM ** -0.5)).astype(BF16) for hn in h]
    o = []
    for qn in q:
        outs = []
        for hd in range(MEM_HEADS):
            cols = slice(hd * MEM_HEAD_DIM, (hd + 1) * MEM_HEAD_DIM)
            s = _dot_nt(qn[:, cols], kv_ref[:, cols])
            m = jnp.max(s, axis=-1, keepdims=True)
            p = jnp.exp(s - m)
            p = p / jnp.sum(p, axis=-1, keepdims=True)
            vcols = slice(width + hd * MEM_HEAD_DIM, width + (hd + 1) * MEM_HEAD_DIM)
            outs.append(_dot(p.astype(BF16), kv_ref[:, vcols]).astype(BF16))
        o.append(jnp.concatenate(outs, axis=-1))
    y2 = [_dot(on, wo_ref[...]) for on in o]
    for n, r in enumerate(groups):
        o_ref[r, :] = x[n] + _rms(y2[n], gmpost_ref[...])


def _mix_out_mem(x, o_fox_t, o_gla, kv, wout, gpost, gpre, wq, wo, gmpost, seq, mem_len):
    t = x.shape[0]
    tiles_per_seq = seq // TMO
    row = lambda w: pl.BlockSpec((TMO, w), lambda i: (i, 0))
    return pl.pallas_call(
        _mix_out_mem_kernel,
        grid=(t // TMO,),
        in_specs=[
            row(D_MODEL),
            pl.BlockSpec((FOX_WIDTH, TMO), lambda i: (0, i)),
            row(GLA_WIDTH),
            pl.BlockSpec((mem_len, kv.shape[1]), lambda i: (i // tiles_per_seq, 0)),
            _const_spec((D_MODEL, D_MODEL)), _const_spec((1, D_MODEL)), _const_spec((1, D_MODEL)),
            _const_spec((D_MODEL, D_MODEL)), _const_spec((D_MODEL, D_MODEL)),
            _const_spec((1, D_MODEL)),
        ],
        out_specs=row(D_MODEL),
        out_shape=jax.ShapeDtypeStruct((t, D_MODEL), F32),
        compiler_params=pltpu.CompilerParams(
            dimension_semantics=("arbitrary",), vmem_limit_bytes=VMEM_LIMIT),
        name="mix_out_mem",
    )(x, o_fox_t, o_gla, kv, wout, gpost, gpre, wq, wo, gmpost)


def _pad_cols(w, width):
    return jnp.pad(w, ((0, 0), (0, width - w.shape[1])))


def _aug_base(h):
    return FOX_HEAD_DIM if h % 2 == 0 else 0


def _placement():
    pk = np.zeros((LANES, FOX_PAD), np.float32)
    pq = np.zeros((FOX_HEADS * AUG_ROWS, LANES), np.float32)
    for h in range(FOX_HEADS):
        for t in range(C_TERMS):
            src = t * FOX_HEADS + h
            pq[h * AUG_ROWS + t, src] = 1.0
            pk[ONE_LANE, h * LANES + _aug_base(h) + t] = 1.0
            pq[h * AUG_ROWS + C_TERMS + t, ONE_LANE] = 1.0
            pk[src, h * LANES + _aug_base(h) + C_TERMS + t] = -1.0
    return jnp.asarray(pk, BF16), jnp.asarray(pq, BF16)


def kernel(x, mem, g_ff1_pre, w_ff1_gate, w_ff1_up, w_ff1_down, g_ff1_post, g_mix_pre, w_mix_in, b_fox_f, w_gla_g2, b_gla_g, g_fox_out, g_gla_out, w_mix_out, g_mix_post, g_mem_pre, g_mem_src, w_mem_q, w_mem_kv, w_mem_o, g_mem_post, g_ff2_pre, w_ff2_gate, w_ff2_up, w_ff2_down, g_ff2_post, g_final):
    batch, seq, d = x.shape
    mem_len = mem.shape[1]
    depth = w_ff1_gate.shape[0]
    xt = x.reshape(batch * seq, d)
    memt = mem.reshape(batch * mem_len, d)
    gfin = g_final.reshape(1, d)
    vec = lambda v: v.reshape(1, -1).astype(F32)
    pk, pqt = _placement()
    for l in range(depth):
        last = l == depth - 1
        xt = _ffn(xt, vec(g_ff1_pre[l]), w_ff1_gate[l].astype(BF16), w_ff1_up[l].astype(BF16),
                  w_ff1_down[l].astype(BF16), vec(g_ff1_post[l]), gfin, False)

        w_in = w_mix_in[l]
        wqt = (w_in[:, OFF_FQ:OFF_FK] * (LOG2E * FOX_HEAD_DIM ** -0.5)).T.astype(BF16)
        wk = w_in[:, OFF_FK:OFF_FV].astype(BF16)
        wvt = w_in[:, OFF_FV:OFF_FF].T.astype(BF16)
        wgla = w_in[:, OFF_GQ:OFF_GG].astype(BF16)
        wgr = w_in[:, OFF_GR:IN_WIDTH].astype(BF16)
        wsm = _pad_cols(jnp.concatenate([w_in[:, OFF_FF:OFF_GQ], w_in[:, OFF_GG:OFF_GR]], axis=1),
                        LANES).astype(BF16)
        bf = _pad_cols(vec(b_fox_f[l]), LANES)
        wg2 = jnp.pad(w_gla_g2[l].astype(F32),
                      ((FOX_HEADS, LANES - FOX_HEADS - GLA_GATE_RANK), (0, 0)))
        k, qt, vt, gqk, gv, la, sr = _mix_in(xt, vec(g_mix_pre[l]), wk, wqt, wvt, wgla, wgr, wsm,
                                             bf, pk, pqt, wg2, vec(b_gla_g[l]), seq)

        g_fox = jnp.broadcast_to(g_fox_out[l].astype(F32)[:, None], (FOX_WIDTH, TQ))
        o_fox_t = _fox(qt, k, vt, g_fox, batch, seq)
        bsd = lambda a: a.reshape(batch, seq, a.shape[-1])
        o_gla = _gla(bsd(gqk), bsd(gv), bsd(la), bsd(sr), vec(g_gla_out[l]), batch, seq)
        o_gla = o_gla.reshape(batch * seq, GLA_WIDTH)

        kv = _mem_kv(memt, vec(g_mem_src[l]), w_mem_kv[l].astype(BF16), mem_len)
        xt = _mix_out_mem(xt, o_fox_t, o_gla, kv, w_mix_out[l].astype(BF16), vec(g_mix_post[l]),
                          vec(g_mem_pre[l]), w_mem_q[l].astype(BF16), w_mem_o[l].astype(BF16),
                          vec(g_mem_post[l]), seq, mem_len)

        xt = _ffn(xt, vec(g_ff2_pre[l]), w_ff2_gate[l].astype(BF16), w_ff2_up[l].astype(BF16),
                  w_ff2_down[l].astype(BF16), vec(g_ff2_post[l]), gfin, last)
    return xt.reshape(batch, seq, d)
```

```python
import functools

import numpy as np
import jax
import jax.numpy as jnp
from jax import lax
from jax.experimental import pallas as pl
from jax.experimental.pallas import tpu as pltpu

F32 = jnp.float32
BF16 = jnp.bfloat16

D_MODEL = 1024
CHUNK = 64
FOX_WIDTH = 512
FOX_HEADS = 8
FOX_HEAD_DIM = 64
GLA_WIDTH = 512
GLA_HEADS = 4
GLA_VAL_DIM = 128
GLA_KEY_DIM = 64
GLA_QK = 256
GLA_GATE_RANK = 16
GLA_GATE_NORM = 16.0
MEM_HEADS = 4
MEM_HEAD_DIM = 256
D_FF = 2816
RMS_EPS = 1e-6

OFF_FQ = 0
OFF_FK = OFF_FQ + FOX_WIDTH
OFF_FV = OFF_FK + FOX_WIDTH
OFF_FF = OFF_FV + FOX_WIDTH
OFF_GQ = OFF_FF + FOX_HEADS
OFF_GK = OFF_GQ + GLA_QK
OFF_GV = OFF_GK + GLA_QK
OFF_GG = OFF_GV + GLA_WIDTH
OFF_GR = OFF_GG + GLA_GATE_RANK
IN_WIDTH = OFF_GR + GLA_WIDTH

LANES = 128
VMEM_LIMIT = 56 * 1024 * 1024

TM = 512
TQ = 256
TG = 512
GLA_SEQS = 4
SCORE_LEAD = 1
FFN_GROUPS = 2
TMO = 1024
MIX_GROUPS = 2

FOX_PAD = FOX_HEADS * LANES
C_TERMS = 3
ONE_LANE = C_TERMS * FOX_HEADS
AUG_ROWS = 16
V_ONE_ROW = FOX_HEAD_DIM
V_ROWS = FOX_HEAD_DIM + AUG_ROWS
LOG2E = 1.4426950408889634


def _rms(x, g):
    return x * lax.rsqrt(jnp.mean(x * x, axis=-1, keepdims=True) + RMS_EPS) * g


def _dot(a, b):
    return jnp.dot(a, b, preferred_element_type=F32)


def _dot_nt(a, b):
    return lax.dot_general(a, b, (((1,), (1,)), ((), ())), preferred_element_type=F32)


def _dot_tn(a, b):
    return lax.dot_general(a, b, (((0,), (0,)), ((), ())), preferred_element_type=F32)


def _split3(x):
    hi = x.astype(BF16)
    r1 = x - hi.astype(F32)
    mid = r1.astype(BF16)
    lo = (r1 - mid.astype(F32)).astype(BF16)
    return hi, mid, lo


def _tri_cumsum(tri_bf16, x):
    hi, mid, lo = _split3(x)
    return _dot(tri_bf16, hi) + _dot(tri_bf16, mid) + _dot(tri_bf16, lo)


def _pack_terms(x, one_lane=None):
    hi, mid, lo = _split3(x)
    out = (hi.astype(F32) + pltpu.roll(mid.astype(F32), FOX_HEADS, 1)
           + pltpu.roll(lo.astype(F32), 2 * FOX_HEADS, 1))
    if one_lane is not None:
        lane = lax.broadcasted_iota(jnp.int32, (1, LANES), 1)
        out = out + jnp.where(lane == one_lane, 1.0, 0.0)
    return out.astype(BF16)


def _log_sigmoid(x):
    return jnp.minimum(x, 0.0) - jnp.log1p(jnp.exp(-jnp.abs(x)))


def _silu(x):
    return x * jax.nn.sigmoid(x)


def _const_spec(shape):
    return pl.BlockSpec(shape, lambda *_: (0,) * len(shape), pipeline_mode=pl.Buffered(1))


def _ffn_kernel(x_ref, gpre_ref, wg_ref, wu_ref, wd_ref, gpost_ref, gfin_ref, o_ref, *, final):
    groups = [slice(n * TM // FFN_GROUPS, (n + 1) * TM // FFN_GROUPS) for n in range(FFN_GROUPS)]
    x = [x_ref[r, :] for r in groups]
    h = [_rms(xn, gpre_ref[...]).astype(BF16) for xn in x]
    a = [(_silu(_dot(hn, wg_ref[...])) * _dot(hn, wu_ref[...])).astype(BF16) for hn in h]
    y = [_dot(an, wd_ref[...]) for an in a]
    for n, r in enumerate(groups):
        out = x[n] + 0.5 * _rms(y[n], gpost_ref[...])
        if final:
            out = _rms(out, gfin_ref[...])
        o_ref[r, :] = out


def _ffn(x, gpre, wg, wu, wd, gpost, gfin, final):
    t = x.shape[0]
    return pl.pallas_call(
        functools.partial(_ffn_kernel, final=final),
        grid=(t // TM,),
        in_specs=[
            pl.BlockSpec((TM, D_MODEL), lambda i: (i, 0)),
            _const_spec((1, D_MODEL)),
            _const_spec((D_MODEL, D_FF)),
            _const_spec((D_MODEL, D_FF)),
            _const_spec((D_FF, D_MODEL)),
            _const_spec((1, D_MODEL)),
            _const_spec((1, D_MODEL)),
        ],
        out_specs=pl.BlockSpec((TM, D_MODEL), lambda i: (i, 0)),
        out_shape=jax.ShapeDtypeStruct((t, D_MODEL), F32),
        compiler_params=pltpu.CompilerParams(
            dimension_semantics=("arbitrary",), vmem_limit_bytes=VMEM_LIMIT),
        name="ffn_final" if final else "ffn",
    )(x, gpre, wg, wu, wd, gpost, gfin)


def _mix_in_kernel(x_ref, g_ref, wk_ref, wqt_ref, wvt_ref, wgla_ref, wgr_ref, wsm_ref,
                   bf_ref, pk_ref, pqt_ref, wg2_ref, bg_ref,
                   k_ref, qt_ref, vt_ref, gqk_ref, gv_ref, la_ref, sr_ref,
                   carry_ref, *, tiles_per_seq):
    i = pl.program_id(0)

    @pl.when(i % tiles_per_seq == 0)
    def _():
        carry_ref[...] = jnp.zeros_like(carry_ref)

    h = _rms(x_ref[...], g_ref[...]).astype(BF16)
    gl = _dot(h, wgla_ref[...])
    gqk_ref[...] = gl[:, :2 * GLA_QK].astype(BF16)
    gv_ref[...] = gl[:, 2 * GLA_QK:].astype(BF16)
    sr_ref[...] = _silu(_dot(h, wgr_ref[...])).astype(BF16)

    sm = _dot(h, wsm_ref[...])

    lane = lax.broadcasted_iota(jnp.int32, (1, LANES), 1)
    lf = jnp.where(lane < FOX_HEADS, _log_sigmoid(sm + bf_ref[...]), 0.0)
    row = lax.broadcasted_iota(jnp.int32, (TM, TM), 0)
    col = lax.broadcasted_iota(jnp.int32, (TM, TM), 1)
    tri = jnp.where(col <= row, 1.0, 0.0).astype(BF16)
    c3 = _dot(tri, _pack_terms(lf))
    c = jnp.where(lane < FOX_HEADS,
                  c3 + pltpu.roll(c3, LANES - FOX_HEADS, 1) + pltpu.roll(c3, LANES - 2 * FOX_HEADS, 1),
                  0.0) + carry_ref[...]
    carry_ref[...] = c[TM - 1:TM, :]
    packed = _pack_terms(c * LOG2E, one_lane=ONE_LANE)

    kp = _dot(h, wk_ref[...])
    kaug = _dot(packed, pk_ref[...])
    low = lane < FOX_HEAD_DIM
    for hd in range(FOX_HEADS):
        pair = slice((hd // 2) * LANES, (hd // 2 + 1) * LANES)
        slot = slice(hd * LANES, (hd + 1) * LANES)
        own = low if hd % 2 == 0 else jnp.logical_not(low)
        k_ref[:, slot] = jnp.where(own, kp[:, pair], kaug[:, pair]).astype(BF16)

    qt = _dot_nt(wqt_ref[...], h).astype(BF16)
    aug = _dot_nt(pqt_ref[...], packed).astype(BF16)
    vt = _dot_nt(wvt_ref[...], h).astype(BF16)
    one_rows = jnp.where(lax.broadcasted_iota(jnp.int32, (AUG_ROWS, TQ), 0) == 0,
                         1.0, 0.0).astype(BF16)
    zeros = jnp.zeros((LANES - FOX_HEAD_DIM - AUG_ROWS, TQ), BF16)
    for n in range(TM // TQ):
        cs = slice(n * TQ, (n + 1) * TQ)
        for hd in range(FOX_HEADS):
            q_h = qt[hd * FOX_HEAD_DIM:(hd + 1) * FOX_HEAD_DIM, cs]
            a_h = aug[hd * AUG_ROWS:(hd + 1) * AUG_ROWS, cs]
            base = hd * LANES
            pieces = (q_h, a_h, zeros) if hd % 2 == 0 else (a_h, zeros, q_h)
            r = base
            for piece in pieces:
                qt_ref[n, r:r + piece.shape[0], :] = piece
                r += piece.shape[0]
            vb = hd * V_ROWS
            vt_ref[n, vb:vb + FOX_HEAD_DIM, :] = vt[hd * FOX_HEAD_DIM:(hd + 1) * FOX_HEAD_DIM, cs]
            vt_ref[n, vb + FOX_HEAD_DIM:vb + V_ROWS, :] = one_rows

    g_hi, g_mid, _ = _split3(sm)
    w_hi, w_mid, _ = _split3(wg2_ref[...])
    gate = _dot(g_hi, w_hi) + _dot(g_mid, w_hi) + _dot(g_hi, w_mid) + bg_ref[...]
    la_ref[...] = _log_sigmoid(gate) / GLA_GATE_NORM


def _mix_in(x, g, wk, wqt, wvt, wgla, wgr, wsm, bf, pk, pqt, wg2, bg, seq):
    t = x.shape[0]
    row = lambda w: pl.BlockSpec((TM, w), lambda i: (i, 0))
    tblk = lambda rows: pl.BlockSpec((TM // TQ, rows, TQ), lambda i: (i, 0, 0))
    return pl.pallas_call(
        functools.partial(_mix_in_kernel, tiles_per_seq=seq // TM),
        grid=(t // TM,),
        in_specs=[
            row(D_MODEL),
            _const_spec((1, D_MODEL)),
            _const_spec((D_MODEL, FOX_WIDTH)),
            _const_spec((FOX_WIDTH, D_MODEL)),
            _const_spec((FOX_WIDTH, D_MODEL)),
            _const_spec((D_MODEL, 2 * GLA_QK + GLA_WIDTH)),
            _const_spec((D_MODEL, GLA_WIDTH)),
            _const_spec((D_MODEL, LANES)),
            _const_spec((1, LANES)),
            _const_spec((LANES, FOX_WIDTH)),
            _const_spec((FOX_HEADS * AUG_ROWS, LANES)),
            _const_spec((LANES, GLA_QK)),
            _const_spec((1, GLA_QK)),
        ],
        out_specs=[row(FOX_PAD), tblk(FOX_PAD), tblk(FOX_HEADS * V_ROWS), row(2 * GLA_QK),
                   row(GLA_WIDTH), row(GLA_QK), row(GLA_WIDTH)],
        out_shape=[
            jax.ShapeDtypeStruct((t, FOX_PAD), BF16),
            jax.ShapeDtypeStruct((t // TQ, FOX_PAD, TQ), BF16),
            jax.ShapeDtypeStruct((t // TQ, FOX_HEADS * V_ROWS, TQ), BF16),
            jax.ShapeDtypeStruct((t, 2 * GLA_QK), BF16),
            jax.ShapeDtypeStruct((t, GLA_WIDTH), BF16),
            jax.ShapeDtypeStruct((t, GLA_QK), F32),
            jax.ShapeDtypeStruct((t, GLA_WIDTH), BF16),
        ],
        scratch_shapes=[pltpu.VMEM((1, LANES), F32)],
        compiler_params=pltpu.CompilerParams(
            dimension_semantics=("arbitrary",), vmem_limit_bytes=VMEM_LIMIT),
        name="mix_in",
    )(x, g, wk, wqt, wvt, wgla, wgr, wsm, bf, pk, pqt, wg2, bg)


def _fox_kernel(qt_ref, k_ref, vt_ref, g_ref, o_ref, m_ref, acc_ref, sa_ref, sb_ref):
    i = pl.program_id(1)
    m_ref[...] = jnp.full(m_ref.shape, -jnp.inf, F32)
    acc_ref[...] = jnp.zeros(acc_ref.shape, F32)
    key = lax.broadcasted_iota(jnp.int32, (TQ, TQ), 0)
    qry = lax.broadcasted_iota(jnp.int32, (TQ, TQ), 1)
    causal = key <= qry
    slots = [slice(h * LANES, (h + 1) * LANES) for h in range(FOX_HEADS)]
    vrows = [slice(h * V_ROWS, (h + 1) * V_ROWS) for h in range(FOX_HEADS)]
    both = (0, 1)

    def score(j, s_ref, g, h):
        start = pl.multiple_of(j * TQ, TQ)
        s_ref[g, h] = _dot(k_ref[pl.ds(start, TQ), slots[h]], qt_ref[g, slots[h], :])

    def accumulate(j, s_ref, g, h, diagonal):
        s = s_ref[g, h]
        if diagonal:
            s = jnp.where(causal, s, -jnp.inf)
        m_old = m_ref[g, h:h + 1, :]
        m_new = jnp.maximum(m_old, jnp.max(s, axis=0, keepdims=True))
        alpha = jnp.exp2(m_old - m_new)
        p = jnp.exp2(s - m_new).astype(BF16)
        m_ref[g, h:h + 1, :] = m_new
        acc_ref[g, h] = alpha * acc_ref[g, h] + _dot(vt_ref[j, vrows[h], :], p)

    def stage(cur, nxt):
        todo = [(g, h) for g in nxt[2] for h in range(FOX_HEADS)] if nxt else []
        units = [(g, h) for g in cur[2] for h in range(FOX_HEADS)] if cur else []
        for g, h in todo[:SCORE_LEAD]:
            score(nxt[0], nxt[1], g, h)
        todo = todo[SCORE_LEAD:]
        per_unit = -(-len(todo) // len(units)) if units else 0
        for g, h in units:
            accumulate(cur[0], cur[1], g, h, g == cur[3])
            for g2, h2 in todo[:per_unit]:
                score(nxt[0], nxt[1], g2, h2)
            todo = todo[per_unit:]
        for g, h in todo:
            score(nxt[0], nxt[1], g, h)

    stage(None, (0, sa_ref, both))

    def pair(t, carry):
        j = 2 * t
        stage((j, sa_ref, both, None), (j + 1, sb_ref, both))
        stage((j + 1, sb_ref, both, None), (j + 2, sa_ref, both))
        return carry

    lax.fori_loop(0, i, pair, 0)
    stage((2 * i, sa_ref, both, 0), (2 * i + 1, sb_ref, (1,)))
    stage((2 * i + 1, sb_ref, (1,), 1), None)

    for g in both:
        for h in range(FOX_HEADS):
            acc = acc_ref[g, h]
            o = acc[:FOX_HEAD_DIM, :] / acc[V_ONE_ROW:V_ONE_ROW + 1, :]
            ms = jnp.mean(o * o, axis=0, keepdims=True)
            rows = slice(h * FOX_HEAD_DIM, (h + 1) * FOX_HEAD_DIM)
            o_ref[rows, g * TQ:(g + 1) * TQ] = (o * lax.rsqrt(ms + RMS_EPS)
                                                * g_ref[rows, :]).astype(BF16)


def _fox(qt, k, vt, g, batch, seq):
    nq = seq // TQ
    steps = nq // 2
    return pl.pallas_call(
        _fox_kernel,
        grid=(batch, steps),
        in_specs=[
            pl.BlockSpec((2, FOX_PAD, TQ), lambda b, i: (b * steps + i, 0, 0)),
            pl.BlockSpec((seq, FOX_PAD), lambda b, i: (b, 0)),
            pl.BlockSpec((nq, FOX_HEADS * V_ROWS, TQ), lambda b, i: (b, 0, 0)),
            _const_spec((FOX_WIDTH, TQ)),
        ],
        out_specs=pl.BlockSpec((FOX_WIDTH, 2 * TQ), lambda b, i: (0, b * steps + i)),
        out_shape=jax.ShapeDtypeStruct((FOX_WIDTH, batch * seq), BF16),
        scratch_shapes=[pltpu.VMEM((2, FOX_HEADS, TQ), F32),
                        pltpu.VMEM((2, FOX_HEADS, V_ROWS, TQ), F32),
                        pltpu.VMEM((2, FOX_HEADS, TQ, TQ), F32),
                        pltpu.VMEM((2, FOX_HEADS, TQ, TQ), F32)],
        compiler_params=pltpu.CompilerParams(
            dimension_semantics=("arbitrary", "arbitrary"), vmem_limit_bytes=VMEM_LIMIT),
        name="fox",
    )(qt, k, vt, g)


def _gla_kernel(qk_ref, v_ref, la_ref, sr_ref, g_ref, o_ref, st_ref):
    @pl.when(pl.program_id(1) == 0)
    def _():
        st_ref[...] = jnp.zeros_like(st_ref)

    nchunk = TG // CHUNK
    seqs = range(GLA_SEQS)
    heads = range(GLA_HEADS)
    lane = lax.broadcasted_iota(jnp.int32, (1, LANES), 1)
    first = lane < GLA_KEY_DIM
    row = lax.broadcasted_iota(jnp.int32, (TG, TG), 0)
    col = lax.broadcasted_iota(jnp.int32, (TG, TG), 1)
    causal = (row - col).astype(jnp.uint32) <= (row & (CHUNK - 1)).astype(jnp.uint32)
    tri = jnp.where(causal, 1.0, 0.0).astype(BF16)
    scale = GLA_KEY_DIM ** -0.5
    zero = jnp.zeros((TG, LANES), BF16)
    chunks = [slice(c * CHUNK, (c + 1) * CHUNK) for c in range(nchunk)]
    prs = [slice((h // 2) * LANES, (h // 2 + 1) * LANES) for h in heads]
    cols = [slice(h * GLA_VAL_DIM, (h + 1) * GLA_VAL_DIM) for h in heads]
    hmasks = [first if h % 2 == 0 else jnp.logical_not(first) for h in heads]

    srow = lax.broadcasted_iota(jnp.int32, (LANES, 2 * GLA_VAL_DIM), 0) // GLA_KEY_DIM
    scol = lax.broadcasted_iota(jnp.int32, (LANES, 2 * GLA_VAL_DIM), 1) // GLA_VAL_DIM
    own_block = srow == scol
    pairs = range(GLA_HEADS // 2)
    pcols = [slice(p * 2 * GLA_VAL_DIM, (p + 1) * 2 * GLA_VAL_DIM) for p in pairs]
    pk = [slice(p * LANES, (p + 1) * LANES) for p in pairs]

    b = [_tri_cumsum(tri, la_ref[n]) for n in seqs]
    qp, qd, kr, k_inv, decay_t = [], [], [], [], []
    for n in seqs:
        b_last3 = b[n].reshape(nchunk, CHUNK, GLA_QK)[:, CHUNK - 1:CHUNK, :]
        b_last = jnp.broadcast_to(b_last3, (nchunk, CHUNK, GLA_QK)).reshape(TG, GLA_QK)
        q = qk_ref[n, :, 0:GLA_QK].astype(F32)
        k = qk_ref[n, :, GLA_QK:2 * GLA_QK].astype(F32)
        q_dec = (q * scale * jnp.exp(b[n])).astype(BF16)
        kr.append((k * jnp.exp(b_last - b[n])).astype(BF16))
        k_inv.append((k * jnp.exp(-b[n])).astype(BF16))
        decay = jnp.exp(b_last3).reshape(nchunk, GLA_QK)
        decay_t.append(jnp.concatenate(
            [decay, jnp.zeros((LANES - nchunk, GLA_QK), F32)], axis=0).T)
        qp.append(q_dec)
        qd.append([jnp.where(hmasks[h], q_dec[:, prs[h]], zero) for h in heads])
    kv, att = [], []
    for n in seqs:
        kv.append([[_dot_tn(kr[n][r, pk[p]], v_ref[n, r, pcols[p]]) for r in chunks]
                   for p in pairs])
        att.append([jnp.where(causal, _dot_nt(qd[n][h], k_inv[n][:, prs[h]]), 0.0).astype(BF16)
                    for h in heads])
    states = []
    for n in seqs:
        per_pair = []
        for p in pairs:
            st = st_ref[n, p]
            per_chunk = []
            for c in range(nchunk):
                per_chunk.append(jnp.where(own_block, st, 0.0).astype(BF16))
                st = st * decay_t[n][pk[p], c:c + 1] + kv[n][p][c]
            st_ref[n, p] = st
            per_pair.append(per_chunk)
        states.append(per_pair)
    for n in seqs:
        inter = [jnp.concatenate([_dot(qp[n][chunks[c], pk[p]], states[n][p][c])
                                  for c in range(nchunk)], axis=0) for p in pairs]
        for h in heads:
            half = slice((h % 2) * GLA_VAL_DIM, (h % 2 + 1) * GLA_VAL_DIM)
            o = _dot(att[n][h], v_ref[n, :, cols[h]]) + inter[h // 2][:, half]
            ms = jnp.mean(o * o, axis=-1, keepdims=True)
            y = o * lax.rsqrt(ms + RMS_EPS) * g_ref[:, cols[h]] * sr_ref[n, :, cols[h]].astype(F32)
            o_ref[n, :, cols[h]] = y.astype(BF16)


def _gla(gqk, gv, la, sr, g, batch, seq):
    blk = lambda w: pl.BlockSpec((GLA_SEQS, TG, w), lambda b, i: (b, i, 0))
    return pl.pallas_call(
        _gla_kernel,
        grid=(batch // GLA_SEQS, seq // TG),
        in_specs=[blk(2 * GLA_QK), blk(GLA_WIDTH), blk(GLA_QK), blk(GLA_WIDTH),
                  pl.BlockSpec((1, GLA_WIDTH), lambda b, i: (0, 0))],
        out_specs=blk(GLA_WIDTH),
        out_shape=jax.ShapeDtypeStruct((batch, seq, GLA_WIDTH), BF16),
        scratch_shapes=[pltpu.VMEM((GLA_SEQS, GLA_HEADS // 2, LANES, 2 * GLA_VAL_DIM), F32)],
        compiler_params=pltpu.CompilerParams(
            dimension_semantics=("arbitrary", "arbitrary"), vmem_limit_bytes=VMEM_LIMIT),
        name="gla",
    )(gqk, gv, la, sr, g)


def _mem_kv_kernel(m_ref, g_ref, w_ref, o_ref):
    h = _rms(m_ref[...], g_ref[...]).astype(BF16)
    o_ref[...] = _dot(h, w_ref[...]).astype(BF16)


def _mem_kv(mem, g, w, mem_len):
    t = mem.shape[0]
    width = w.shape[1]
    return pl.pallas_call(
        _mem_kv_kernel,
        grid=(t // mem_len,),
        in_specs=[pl.BlockSpec((mem_len, D_MODEL), lambda i: (i, 0)),
                  _const_spec((1, D_MODEL)), _const_spec((D_MODEL, width))],
        out_specs=pl.BlockSpec((mem_len, width), lambda i: (i, 0)),
        out_shape=jax.ShapeDtypeStruct((t, width), BF16),
        compiler_params=pltpu.CompilerParams(
            dimension_semantics=("arbitrary",), vmem_limit_bytes=VMEM_LIMIT),
        name="mem_kv",
    )(mem, g, w)


def _mix_out_mem_kernel(x_ref, oft_ref, og_ref, kv_ref, wout_ref, gpost_ref, gpre_ref,
                        wq_ref, wo_ref, gmpost_ref, o_ref):
    width = MEM_HEADS * MEM_HEAD_DIM
    groups = [slice(n * TMO // MIX_GROUPS, (n + 1) * TMO // MIX_GROUPS) for n in range(MIX_GROUPS)]
    y = [_dot_tn(oft_ref[:, r], wout_ref[0:FOX_WIDTH, :]) + _dot(og_ref[r, :], wout_ref[FOX_WIDTH:, :])
         for r in groups]
    x = [x_ref[r, :] + _rms(y[n], gpost_ref[...]) for n, r in enumerate(groups)]
    h = [_rms(xn, gpre_ref[...]).astype(BF16) for xn in x]
    q = [(_dot(hn, wq_ref[...]) * (MEM_HEAD_DIM ** -0.5)).astype(BF16) for hn in h]
    o = []
    for qn in q:
        outs = []
        for hd in range(MEM_HEADS):
            cols = slice(hd * MEM_HEAD_DIM, (hd + 1) * MEM_HEAD_DIM)
            s = _dot_nt(qn[:, cols], kv_ref[:, cols])
            m = jnp.max(s, axis=-1, keepdims=True)
            p = jnp.exp(s - m)
            p = p / jnp.sum(p, axis=-1, keepdims=True)
            vcols = slice(width + hd * MEM_HEAD_DIM, width + (hd + 1) * MEM_HEAD_DIM)
            outs.append(_dot(p.astype(BF16), kv_ref[:, vcols]).astype(BF16))
        o.append(jnp.concatenate(outs, axis=-1))
    y2 = [_dot(on, wo_ref[...]) for on in o]
    for n, r in enumerate(groups):
        o_ref[r, :] = x[n] + _rms(y2[n], gmpost_ref[...])


def _mix_out_mem(x, o_fox_t, o_gla, kv, wout, gpost, gpre, wq, wo, gmpost, seq, mem_len):
    t = x.shape[0]
    tiles_per_seq = seq // TMO
    row = lambda w: pl.BlockSpec((TMO, w), lambda i: (i, 0))
    return pl.pallas_call(
        _mix_out_mem_kernel,
        grid=(t // TMO,),
        in_specs=[
            row(D_MODEL),
            pl.BlockSpec((FOX_WIDTH, TMO), lambda i: (0, i)),
            row(GLA_WIDTH),
            pl.BlockSpec((mem_len, kv.shape[1]), lambda i: (i // tiles_per_seq, 0)),
            _const_spec((D_MODEL, D_MODEL)), _const_spec((1, D_MODEL)), _const_spec((1, D_MODEL)),
            _const_spec((D_MODEL, D_MODEL)), _const_spec((D_MODEL, D_MODEL)),
            _const_spec((1, D_MODEL)),
        ],
        out_specs=row(D_MODEL),
        out_shape=jax.ShapeDtypeStruct((t, D_MODEL), F32),
        compiler_params=pltpu.CompilerParams(
            dimension_semantics=("arbitrary",), vmem_limit_bytes=VMEM_LIMIT),
        name="mix_out_mem",
    )(x, o_fox_t, o_gla, kv, wout, gpost, gpre, wq, wo, gmpost)


def _pad_cols(w, width):
    return jnp.pad(w, ((0, 0), (0, width - w.shape[1])))


def _aug_base(h):
    return FOX_HEAD_DIM if h % 2 == 0 else 0


def _placement():
    pk = np.zeros((LANES, FOX_WIDTH), np.float32)
    pq = np.zeros((FOX_HEADS * AUG_ROWS, LANES), np.float32)
    for h in range(FOX_HEADS):
        kbase = (h // 2) * LANES + _aug_base(h)
        for t in range(C_TERMS):
            src = t * FOX_HEADS + h
            pq[h * AUG_ROWS + t, src] = 1.0
            pk[ONE_LANE, kbase + t] = 1.0
            pq[h * AUG_ROWS + C_TERMS + t, ONE_LANE] = 1.0
            pk[src, kbase + C_TERMS + t] = -1.0
    return jnp.asarray(pk, BF16), jnp.asarray(pq, BF16)


def kernel(x, mem, g_ff1_pre, w_ff1_gate, w_ff1_up, w_ff1_down, g_ff1_post, g_mix_pre, w_mix_in, b_fox_f, w_gla_g2, b_gla_g, g_fox_out, g_gla_out, w_mix_out, g_mix_post, g_mem_pre, g_mem_src, w_mem_q, w_mem_kv, w_mem_o, g_mem_post, g_ff2_pre, w_ff2_gate, w_ff2_up, w_ff2_down, g_ff2_post, g_final):
    batch, seq, d = x.shape
    mem_len = mem.shape[1]
    depth = w_ff1_gate.shape[0]
    xt = x.reshape(batch * seq, d)
    memt = mem.reshape(batch * mem_len, d)
    gfin = g_final.reshape(1, d)
    vec = lambda v: v.reshape(1, -1).astype(F32)
    pk, pqt = _placement()
    for l in range(depth):
        last = l == depth - 1
        xt = _ffn(xt, vec(g_ff1_pre[l]), w_ff1_gate[l].astype(BF16), w_ff1_up[l].astype(BF16),
                  w_ff1_down[l].astype(BF16), vec(g_ff1_post[l]), gfin, False)

        w_in = w_mix_in[l]
        wqt = (w_in[:, OFF_FQ:OFF_FK] * (LOG2E * FOX_HEAD_DIM ** -0.5)).T.astype(BF16)
        wk = w_in[:, OFF_FK:OFF_FV].astype(BF16)
        wvt = w_in[:, OFF_FV:OFF_FF].T.astype(BF16)
        wgla = w_in[:, OFF_GQ:OFF_GG].astype(BF16)
        wgr = w_in[:, OFF_GR:IN_WIDTH].astype(BF16)
        wsm = _pad_cols(jnp.concatenate([w_in[:, OFF_FF:OFF_GQ], w_in[:, OFF_GG:OFF_GR]], axis=1),
                        LANES).astype(BF16)
        bf = _pad_cols(vec(b_fox_f[l]), LANES)
        wg2 = jnp.pad(w_gla_g2[l].astype(F32),
                      ((FOX_HEADS, LANES - FOX_HEADS - GLA_GATE_RANK), (0, 0)))
        k, qt, vt, gqk, gv, la, sr = _mix_in(xt, vec(g_mix_pre[l]), wk, wqt, wvt, wgla, wgr, wsm,
                                             bf, pk, pqt, wg2, vec(b_gla_g[l]), seq)

        g_fox = jnp.broadcast_to(g_fox_out[l].astype(F32)[:, None], (FOX_WIDTH, TQ))
        o_fox_t = _fox(qt, k, vt, g_fox, batch, seq)
        bsd = lambda a: a.reshape(batch, seq, a.shape[-1])
        o_gla = _gla(bsd(gqk), bsd(gv), bsd(la), bsd(sr), vec(g_gla_out[l]), batch, seq)
        o_gla = o_gla.reshape(batch * seq, GLA_WIDTH)

        kv = _mem_kv(memt, vec(g_mem_src[l]), w_mem_kv[l].astype(BF16), mem_len)
        xt = _mix_out_mem(xt, o_fox_t, o_gla, kv, w_mix_out[l].astype(BF16), vec(g_mix_post[l]),
                          vec(g_mem_pre[l]), w_mem_q[l].astype(BF16), w_mem_o[l].astype(BF16),
                          vec(g_mem_post[l]), seq, mem_len)

        xt = _ffn(xt, vec(g_ff2_pre[l]), w_ff2_gate[l].astype(BF16), w_ff2_up[l].astype(BF16),
                  w_ff2_down[l].astype(BF16), vec(g_ff2_post[l]), gfin, last)
    return xt.reshape(batch, seq, d)
```

```python
import functools

import numpy as np
import jax
import jax.numpy as jnp
from jax import lax
from jax.experimental import pallas as pl
from jax.experimental.pallas import tpu as pltpu

F32 = jnp.float32
BF16 = jnp.bfloat16

D_MODEL = 1024
CHUNK = 64
FOX_WIDTH = 512
FOX_HEADS = 8
FOX_HEAD_DIM = 64
GLA_WIDTH = 512
GLA_HEADS = 4
GLA_VAL_DIM = 128
GLA_KEY_DIM = 64
GLA_QK = 256
GLA_GATE_RANK = 16
GLA_GATE_NORM = 16.0
MEM_HEADS = 4
MEM_HEAD_DIM = 256
D_FF = 2816
RMS_EPS = 1e-6

OFF_FQ = 0
OFF_FK = OFF_FQ + FOX_WIDTH
OFF_FV = OFF_FK + FOX_WIDTH
OFF_FF = OFF_FV + FOX_WIDTH
OFF_GQ = OFF_FF + FOX_HEADS
OFF_GK = OFF_GQ + GLA_QK
OFF_GV = OFF_GK + GLA_QK
OFF_GG = OFF_GV + GLA_WIDTH
OFF_GR = OFF_GG + GLA_GATE_RANK
IN_WIDTH = OFF_GR + GLA_WIDTH

LANES = 128
VMEM_LIMIT = 56 * 1024 * 1024

TM = 512
TMF = 1024
TQ = 256
TG = 512
GLA_SEQS = 4
SCORE_LEAD = 1
FFN_GROUPS = 2
TMO = 1024
MIX_GROUPS = 2

FOX_PAD = FOX_HEADS * LANES
C_TERMS = 3
ONE_LANE = C_TERMS * FOX_HEADS
AUG_ROWS = 16
V_ONE_ROW = FOX_HEAD_DIM
V_ROWS = FOX_HEAD_DIM + AUG_ROWS
LOG2E = 1.4426950408889634


def _rms(x, g):
    return x * lax.rsqrt(jnp.mean(x * x, axis=-1, keepdims=True) + RMS_EPS) * g


def _dot(a, b):
    return jnp.dot(a, b, preferred_element_type=F32)


def _dot_nt(a, b):
    return lax.dot_general(a, b, (((1,), (1,)), ((), ())), preferred_element_type=F32)


def _dot_tn(a, b):
    return lax.dot_general(a, b, (((0,), (0,)), ((), ())), preferred_element_type=F32)


def _split3(x):
    hi = x.astype(BF16)
    r1 = x - hi.astype(F32)
    mid = r1.astype(BF16)
    lo = (r1 - mid.astype(F32)).astype(BF16)
    return hi, mid, lo


def _tri_cumsum(tri_bf16, x):
    hi, mid, lo = _split3(x)
    return _dot(tri_bf16, hi) + _dot(tri_bf16, mid) + _dot(tri_bf16, lo)


def _pack_terms(x, one_lane=None):
    hi, mid, lo = _split3(x)
    out = (hi.astype(F32) + pltpu.roll(mid.astype(F32), FOX_HEADS, 1)
           + pltpu.roll(lo.astype(F32), 2 * FOX_HEADS, 1))
    if one_lane is not None:
        lane = lax.broadcasted_iota(jnp.int32, (1, LANES), 1)
        out = out + jnp.where(lane == one_lane, 1.0, 0.0)
    return out.astype(BF16)


def _log_sigmoid(x):
    return jnp.minimum(x, 0.0) - jnp.log1p(jnp.exp(-jnp.abs(x)))


def _silu(x):
    return x * jax.nn.sigmoid(x)


def _const_spec(shape):
    return pl.BlockSpec(shape, lambda *_: (0,) * len(shape), pipeline_mode=pl.Buffered(1))


def _ffn_kernel(x_ref, gpre_ref, wg_ref, wu_ref, wd_ref, gpost_ref, gfin_ref, o_ref, *, final):
    groups = [slice(n * TMF // FFN_GROUPS, (n + 1) * TMF // FFN_GROUPS) for n in range(FFN_GROUPS)]
    x = [x_ref[r, :] for r in groups]
    h = [_rms(xn, gpre_ref[...]).astype(BF16) for xn in x]
    a = [(_silu(_dot(hn, wg_ref[...])) * _dot(hn, wu_ref[...])).astype(BF16) for hn in h]
    y = [_dot(an, wd_ref[...]) for an in a]
    for n, r in enumerate(groups):
        out = x[n] + 0.5 * _rms(y[n], gpost_ref[...])
        if final:
            out = _rms(out, gfin_ref[...])
        o_ref[r, :] = out


def _ffn(x, gpre, wg, wu, wd, gpost, gfin, final):
    t = x.shape[0]
    return pl.pallas_call(
        functools.partial(_ffn_kernel, final=final),
        grid=(t // TMF,),
        in_specs=[
            pl.BlockSpec((TMF, D_MODEL), lambda i: (i, 0)),
            _const_spec((1, D_MODEL)),
            _const_spec((D_MODEL, D_FF)),
            _const_spec((D_MODEL, D_FF)),
            _const_spec((D_FF, D_MODEL)),
            _const_spec((1, D_MODEL)),
            _const_spec((1, D_MODEL)),
        ],
        out_specs=pl.BlockSpec((TMF, D_MODEL), lambda i: (i, 0)),
        out_shape=jax.ShapeDtypeStruct((t, D_MODEL), F32),
        compiler_params=pltpu.CompilerParams(
            dimension_semantics=("arbitrary",), vmem_limit_bytes=VMEM_LIMIT),
        name="ffn_final" if final else "ffn",
    )(x, gpre, wg, wu, wd, gpost, gfin)


def _mix_in_kernel(x_ref, g_ref, wk_ref, wqt_ref, wvt_ref, wgla_ref, wgr_ref, wsm_ref,
                   bf_ref, pk_ref, pqt_ref, wg2_ref, bg_ref,
                   k_ref, qt_ref, vt_ref, gqk_ref, gv_ref, la_ref, sr_ref,
                   carry_ref, *, tiles_per_seq):
    i = pl.program_id(0)

    @pl.when(i % tiles_per_seq == 0)
    def _():
        carry_ref[...] = jnp.zeros_like(carry_ref)

    h = _rms(x_ref[...], g_ref[...]).astype(BF16)
    gl = _dot(h, wgla_ref[...])
    gqk_ref[...] = gl[:, :2 * GLA_QK].astype(BF16)
    gv_ref[...] = gl[:, 2 * GLA_QK:].astype(BF16)
    sr_ref[...] = _silu(_dot(h, wgr_ref[...])).astype(BF16)

    sm = _dot(h, wsm_ref[...])

    lane = lax.broadcasted_iota(jnp.int32, (1, LANES), 1)
    lf = jnp.where(lane < FOX_HEADS, _log_sigmoid(sm + bf_ref[...]), 0.0)
    row = lax.broadcasted_iota(jnp.int32, (TM, TM), 0)
    col = lax.broadcasted_iota(jnp.int32, (TM, TM), 1)
    tri = jnp.where(col <= row, 1.0, 0.0).astype(BF16)
    c3 = _dot(tri, _pack_terms(lf))
    c = jnp.where(lane < FOX_HEADS,
                  c3 + pltpu.roll(c3, LANES - FOX_HEADS, 1) + pltpu.roll(c3, LANES - 2 * FOX_HEADS, 1),
                  0.0) + carry_ref[...]
    carry_ref[...] = c[TM - 1:TM, :]
    packed = _pack_terms(c * LOG2E, one_lane=ONE_LANE)

    kp = _dot(h, wk_ref[...])
    kaug = _dot(packed, pk_ref[...])
    low = lane < FOX_HEAD_DIM
    for hd in range(FOX_HEADS):
        pair = slice((hd // 2) * LANES, (hd // 2 + 1) * LANES)
        slot = slice(hd * LANES, (hd + 1) * LANES)
        own = low if hd % 2 == 0 else jnp.logical_not(low)
        k_ref[:, slot] = jnp.where(own, kp[:, pair], kaug[:, pair]).astype(BF16)

    qt = _dot_nt(wqt_ref[...], h).astype(BF16)
    aug = _dot_nt(pqt_ref[...], packed).astype(BF16)
    vt = _dot_nt(wvt_ref[...], h).astype(BF16)
    one_rows = jnp.where(lax.broadcasted_iota(jnp.int32, (AUG_ROWS, TQ), 0) == 0,
                         1.0, 0.0).astype(BF16)
    zeros = jnp.zeros((LANES - FOX_HEAD_DIM - AUG_ROWS, TQ), BF16)
    for n in range(TM // TQ):
        cs = slice(n * TQ, (n + 1) * TQ)
        for hd in range(FOX_HEADS):
            q_h = qt[hd * FOX_HEAD_DIM:(hd + 1) * FOX_HEAD_DIM, cs]
            a_h = aug[hd * AUG_ROWS:(hd + 1) * AUG_ROWS, cs]
            base = hd * LANES
            pieces = (q_h, a_h, zeros) if hd % 2 == 0 else (a_h, zeros, q_h)
            r = base
            for piece in pieces:
                qt_ref[n, r:r + piece.shape[0], :] = piece
                r += piece.shape[0]
            vb = hd * V_ROWS
            vt_ref[n, vb:vb + FOX_HEAD_DIM, :] = vt[hd * FOX_HEAD_DIM:(hd + 1) * FOX_HEAD_DIM, cs]
            vt_ref[n, vb + FOX_HEAD_DIM:vb + V_ROWS, :] = one_rows

    g_hi, g_mid, _ = _split3(sm)
    w_hi, w_mid, _ = _split3(wg2_ref[...])
    gate = _dot(g_hi, w_hi) + _dot(g_mid, w_hi) + _dot(g_hi, w_mid) + bg_ref[...]
    la_ref[...] = _log_sigmoid(gate) / GLA_GATE_NORM


def _mix_in(x, g, wk, wqt, wvt, wgla, wgr, wsm, bf, pk, pqt, wg2, bg, seq):
    t = x.shape[0]
    row = lambda w: pl.BlockSpec((TM, w), lambda i: (i, 0))
    tblk = lambda rows: pl.BlockSpec((TM // TQ, rows, TQ), lambda i: (i, 0, 0))
    return pl.pallas_call(
        functools.partial(_mix_in_kernel, tiles_per_seq=seq // TM),
        grid=(t // TM,),
        in_specs=[
            row(D_MODEL),
            _const_spec((1, D_MODEL)),
            _const_spec((D_MODEL, FOX_WIDTH)),
            _const_spec((FOX_WIDTH, D_MODEL)),
            _const_spec((FOX_WIDTH, D_MODEL)),
            _const_spec((D_MODEL, 2 * GLA_QK + GLA_WIDTH)),
            _const_spec((D_MODEL, GLA_WIDTH)),
            _const_spec((D_MODEL, LANES)),
            _const_spec((1, LANES)),
            _const_spec((LANES, FOX_WIDTH)),
            _const_spec((FOX_HEADS * AUG_ROWS, LANES)),
            _const_spec((LANES, GLA_QK)),
            _const_spec((1, GLA_QK)),
        ],
        out_specs=[row(FOX_PAD), tblk(FOX_PAD), tblk(FOX_HEADS * V_ROWS), row(2 * GLA_QK),
                   row(GLA_WIDTH), row(GLA_QK), row(GLA_WIDTH)],
        out_shape=[
            jax.ShapeDtypeStruct((t, FOX_PAD), BF16),
            jax.ShapeDtypeStruct((t // TQ, FOX_PAD, TQ), BF16),
            jax.ShapeDtypeStruct((t // TQ, FOX_HEADS * V_ROWS, TQ), BF16),
            jax.ShapeDtypeStruct((t, 2 * GLA_QK), BF16),
            jax.ShapeDtypeStruct((t, GLA_WIDTH), BF16),
            jax.ShapeDtypeStruct((t, GLA_QK), F32),
            jax.ShapeDtypeStruct((t, GLA_WIDTH), BF16),
        ],
        scratch_shapes=[pltpu.VMEM((1, LANES), F32)],
        compiler_params=pltpu.CompilerParams(
            dimension_semantics=("arbitrary",), vmem_limit_bytes=VMEM_LIMIT),
        name="mix_in",
    )(x, g, wk, wqt, wvt, wgla, wgr, wsm, bf, pk, pqt, wg2, bg)


def _fox_kernel(qt_ref, k_ref, vt_ref, g_ref, o_ref, m_ref, acc_ref, sa_ref, sb_ref):
    i = pl.program_id(1)
    m_ref[...] = jnp.full(m_ref.shape, -jnp.inf, F32)
    acc_ref[...] = jnp.zeros(acc_ref.shape, F32)
    key = lax.broadcasted_iota(jnp.int32, (TQ, TQ), 0)
    qry = lax.broadcasted_iota(jnp.int32, (TQ, TQ), 1)
    causal = key <= qry
    slots = [slice(h * LANES, (h + 1) * LANES) for h in range(FOX_HEADS)]
    vrows = [slice(h * V_ROWS, (h + 1) * V_ROWS) for h in range(FOX_HEADS)]
    both = (0, 1)

    def score(j, s_ref, g, h):
        start = pl.multiple_of(j * TQ, TQ)
        s_ref[g, h] = _dot(k_ref[pl.ds(start, TQ), slots[h]], qt_ref[g, slots[h], :])

    def accumulate(j, s_ref, g, h, diagonal):
        s = s_ref[g, h]
        if diagonal:
            s = jnp.where(causal, s, -jnp.inf)
        m_old = m_ref[g, h:h + 1, :]
        m_new = jnp.maximum(m_old, jnp.max(s, axis=0, keepdims=True))
        alpha = jnp.exp2(m_old - m_new)
        p = jnp.exp2(s - m_new).astype(BF16)
        m_ref[g, h:h + 1, :] = m_new
        acc_ref[g, h] = alpha * acc_ref[g, h] + _dot(vt_ref[j, vrows[h], :], p)

    def stage(cur, nxt):
        todo = [(g, h) for g in nxt[2] for h in range(FOX_HEADS)] if nxt else []
        units = [(g, h) for g in cur[2] for h in range(FOX_HEADS)] if cur else []
        for g, h in todo[:SCORE_LEAD]:
            score(nxt[0], nxt[1], g, h)
        todo = todo[SCORE_LEAD:]
        per_unit = -(-len(todo) // len(units)) if units else 0
        for g, h in units:
            accumulate(cur[0], cur[1], g, h, g == cur[3])
            for g2, h2 in todo[:per_unit]:
                score(nxt[0], nxt[1], g2, h2)
            todo = todo[per_unit:]
        for g, h in todo:
            score(nxt[0], nxt[1], g, h)

    stage(None, (0, sa_ref, both))

    def pair(t, carry):
        j = 2 * t
        stage((j, sa_ref, both, None), (j + 1, sb_ref, both))
        stage((j + 1, sb_ref, both, None), (j + 2, sa_ref, both))
        return carry

    lax.fori_loop(0, i, pair, 0)
    stage((2 * i, sa_ref, both, 0), (2 * i + 1, sb_ref, (1,)))
    stage((2 * i + 1, sb_ref, (1,), 1), None)

    for g in both:
        for h in range(FOX_HEADS):
            acc = acc_ref[g, h]
            o = acc[:FOX_HEAD_DIM, :] / acc[V_ONE_ROW:V_ONE_ROW + 1, :]
            ms = jnp.mean(o * o, axis=0, keepdims=True)
            rows = slice(h * FOX_HEAD_DIM, (h + 1) * FOX_HEAD_DIM)
            o_ref[rows, g * TQ:(g + 1) * TQ] = (o * lax.rsqrt(ms + RMS_EPS)
                                                * g_ref[rows, :]).astype(BF16)


def _fox(qt, k, vt, g, batch, seq):
    nq = seq // TQ
    steps = nq // 2
    return pl.pallas_call(
        _fox_kernel,
        grid=(batch, steps),
        in_specs=[
            pl.BlockSpec((2, FOX_PAD, TQ), lambda b, i: (b * steps + i, 0, 0)),
            pl.BlockSpec((seq, FOX_PAD), lambda b, i: (b, 0)),
            pl.BlockSpec((nq, FOX_HEADS * V_ROWS, TQ), lambda b, i: (b, 0, 0)),
            _const_spec((FOX_WIDTH, TQ)),
        ],
        out_specs=pl.BlockSpec((FOX_WIDTH, 2 * TQ), lambda b, i: (0, b * steps + i)),
        out_shape=jax.ShapeDtypeStruct((FOX_WIDTH, batch * seq), BF16),
        scratch_shapes=[pltpu.VMEM((2, FOX_HEADS, TQ), F32),
                        pltpu.VMEM((2, FOX_HEADS, V_ROWS, TQ), F32),
                        pltpu.VMEM((2, FOX_HEADS, TQ, TQ), F32),
                        pltpu.VMEM((2, FOX_HEADS, TQ, TQ), F32)],
        compiler_params=pltpu.CompilerParams(
            dimension_semantics=("arbitrary", "arbitrary"), vmem_limit_bytes=VMEM_LIMIT),
        name="fox",
    )(qt, k, vt, g)


def _gla_kernel(qk_ref, v_ref, la_ref, sr_ref, g_ref, o_ref, st_ref):
    @pl.when(pl.program_id(1) == 0)
    def _():
        st_ref[...] = jnp.zeros_like(st_ref)

    nchunk = TG // CHUNK
    seqs = range(GLA_SEQS)
    heads = range(GLA_HEADS)
    lane = lax.broadcasted_iota(jnp.int32, (1, LANES), 1)
    first = lane < GLA_KEY_DIM
    row = lax.broadcasted_iota(jnp.int32, (TG, TG), 0)
    col = lax.broadcasted_iota(jnp.int32, (TG, TG), 1)
    causal = (row - col).astype(jnp.uint32) <= (row & (CHUNK - 1)).astype(jnp.uint32)
    tri = jnp.where(causal, 1.0, 0.0).astype(BF16)
    scale = GLA_KEY_DIM ** -0.5
    zero = jnp.zeros((TG, LANES), BF16)
    chunks = [slice(c * CHUNK, (c + 1) * CHUNK) for c in range(nchunk)]
    prs = [slice((h // 2) * LANES, (h // 2 + 1) * LANES) for h in heads]
    cols = [slice(h * GLA_VAL_DIM, (h + 1) * GLA_VAL_DIM) for h in heads]
    hmasks = [first if h % 2 == 0 else jnp.logical_not(first) for h in heads]

    srow = lax.broadcasted_iota(jnp.int32, (LANES, 2 * GLA_VAL_DIM), 0) // GLA_KEY_DIM
    scol = lax.broadcasted_iota(jnp.int32, (LANES, 2 * GLA_VAL_DIM), 1) // GLA_VAL_DIM
    own_block = srow == scol
    pairs = range(GLA_HEADS // 2)
    pcols = [slice(p * 2 * GLA_VAL_DIM, (p + 1) * 2 * GLA_VAL_DIM) for p in pairs]
    pk = [slice(p * LANES, (p + 1) * LANES) for p in pairs]

    b = [_tri_cumsum(tri, la_ref[n]) for n in seqs]
    qp, qd, kr, k_inv, decay_t = [], [], [], [], []
    for n in seqs:
        b_last3 = b[n].reshape(nchunk, CHUNK, GLA_QK)[:, CHUNK - 1:CHUNK, :]
        b_last = jnp.broadcast_to(b_last3, (nchunk, CHUNK, GLA_QK)).reshape(TG, GLA_QK)
        q = qk_ref[n, :, 0:GLA_QK].astype(F32)
        k = qk_ref[n, :, GLA_QK:2 * GLA_QK].astype(F32)
        q_dec = (q * scale * jnp.exp(b[n])).astype(BF16)
        kr.append((k * jnp.exp(b_last - b[n])).astype(BF16))
        k_inv.append((k * jnp.exp(-b[n])).astype(BF16))
        decay = jnp.exp(b_last3).reshape(nchunk, GLA_QK)
        decay_t.append(jnp.concatenate(
            [decay, jnp.zeros((LANES - nchunk, GLA_QK), F32)], axis=0).T)
        qp.append(q_dec)
        qd.append([jnp.where(hmasks[h], q_dec[:, prs[h]], zero) for h in heads])
    kv, att = [], []
    for n in seqs:
        kv.append([[_dot_tn(kr[n][r, pk[p]], v_ref[n, r, pcols[p]]) for r in chunks]
                   for p in pairs])
        att.append([jnp.where(causal, _dot_nt(qd[n][h], k_inv[n][:, prs[h]]), 0.0).astype(BF16)
                    for h in heads])
    states = []
    for n in seqs:
        per_pair = []
        for p in pairs:
            st = st_ref[n, p]
            per_chunk = []
            for c in range(nchunk):
                per_chunk.append(jnp.where(own_block, st, 0.0).astype(BF16))
                st = st * decay_t[n][pk[p], c:c + 1] + kv[n][p][c]
            st_ref[n, p] = st
            per_pair.append(per_chunk)
        states.append(per_pair)
    for n in seqs:
        inter = [jnp.concatenate([_dot(qp[n][chunks[c], pk[p]], states[n][p][c])
                                  for c in range(nchunk)], axis=0) for p in pairs]
        for h in heads:
            half = slice((h % 2) * GLA_VAL_DIM, (h % 2 + 1) * GLA_VAL_DIM)
            o = _dot(att[n][h], v_ref[n, :, cols[h]]) + inter[h // 2][:, half]
            ms = jnp.mean(o * o, axis=-1, keepdims=True)
            y = o * lax.rsqrt(ms + RMS_EPS) * g_ref[:, cols[h]] * sr_ref[n, :, cols[h]].astype(F32)
            o_ref[n, :, cols[h]] = y.astype(BF16)


def _gla(gqk, gv, la, sr, g, batch, seq):
    blk = lambda w: pl.BlockSpec((GLA_SEQS, TG, w), lambda b, i: (b, i, 0))
    return pl.pallas_call(
        _gla_kernel,
        grid=(batch // GLA_SEQS, seq // TG),
        in_specs=[blk(2 * GLA_QK), blk(GLA_WIDTH), blk(GLA_QK), blk(GLA_WIDTH),
                  pl.BlockSpec((1, GLA_WIDTH), lambda b, i: (0, 0))],
        out_specs=blk(GLA_WIDTH),
        out_shape=jax.ShapeDtypeStruct((batch, seq, GLA_WIDTH), BF16),
        scratch_shapes=[pltpu.VMEM((GLA_SEQS, GLA_HEADS // 2, LANES, 2 * GLA_VAL_DIM), F32)],
        compiler_params=pltpu.CompilerParams(
            dimension_semantics=("arbitrary", "arbitrary"), vmem_limit_bytes=VMEM_LIMIT),
        name="gla",
    )(gqk, gv, la, sr, g)


def _mem_kv_kernel(m_ref, g_ref, w_ref, o_ref):
    h = _rms(m_ref[...], g_ref[...]).astype(BF16)
    o_ref[...] = _dot(h, w_ref[...]).astype(BF16)


def _mem_kv(mem, g, w, mem_len):
    t = mem.shape[0]
    width = w.shape[1]
    return pl.pallas_call(
        _mem_kv_kernel,
        grid=(t // mem_len,),
        in_specs=[pl.BlockSpec((mem_len, D_MODEL), lambda i: (i, 0)),
                  _const_spec((1, D_MODEL)), _const_spec((D_MODEL, width))],
        out_specs=pl.BlockSpec((mem_len, width), lambda i: (i, 0)),
        out_shape=jax.ShapeDtypeStruct((t, width), BF16),
        compiler_params=pltpu.CompilerParams(
            dimension_semantics=("arbitrary",), vmem_limit_bytes=VMEM_LIMIT),
        name="mem_kv",
    )(mem, g, w)


def _mix_out_mem_kernel(x_ref, oft_ref, og_ref, kv_ref, wout_ref, gpost_ref, gpre_ref,
                        wq_ref, wo_ref, gmpost_ref, o_ref):
    width = MEM_HEADS * MEM_HEAD_DIM
    groups = [slice(n * TMO // MIX_GROUPS, (n + 1) * TMO // MIX_GROUPS) for n in range(MIX_GROUPS)]
    y = [_dot_tn(oft_ref[:, r], wout_ref[0:FOX_WIDTH, :]) + _dot(og_ref[r, :], wout_ref[FOX_WIDTH:, :])
         for r in groups]
    x = [x_ref[r, :] + _rms(y[n], gpost_ref[...]) for n, r in enumerate(groups)]
    h = [_rms(xn, gpre_ref[...]).astype(BF16) for xn in x]
    q = [(_dot(hn, wq_ref[...]) * (MEM_HEAD_DIM ** -0.5)).astype(BF16) for hn in h]
    hcols = [slice(hd * MEM_HEAD_DIM, (hd + 1) * MEM_HEAD_DIM) for hd in range(MEM_HEADS)]
    vcols = [slice(width + hd * MEM_HEAD_DIM, width + (hd + 1) * MEM_HEAD_DIM)
             for hd in range(MEM_HEADS)]
    s = [[_dot_nt(qn[:, c], kv_ref[:, c]) for c in hcols] for qn in q]
    o = []
    for sn in s:
        outs = []
        for hd in range(MEM_HEADS):
            m = jnp.max(sn[hd], axis=-1, keepdims=True)
            p = jnp.exp(sn[hd] - m)
            p = p / jnp.sum(p, axis=-1, keepdims=True)
            outs.append(_dot(p.astype(BF16), kv_ref[:, vcols[hd]]).astype(BF16))
        o.append(jnp.concatenate(outs, axis=-1))
    y2 = [_dot(on, wo_ref[...]) for on in o]
    for n, r in enumerate(groups):
        o_ref[r, :] = x[n] + _rms(y2[n], gmpost_ref[...])


def _mix_out_mem(x, o_fox_t, o_gla, kv, wout, gpost, gpre, wq, wo, gmpost, seq, mem_len):
    t = x.shape[0]
    tiles_per_seq = seq // TMO
    row = lambda w: pl.BlockSpec((TMO, w), lambda i: (i, 0))
    return pl.pallas_call(
        _mix_out_mem_kernel,
        grid=(t // TMO,),
        in_specs=[
            row(D_MODEL),
            pl.BlockSpec((FOX_WIDTH, TMO), lambda i: (0, i)),
            row(GLA_WIDTH),
            pl.BlockSpec((mem_len, kv.shape[1]), lambda i: (i // tiles_per_seq, 0)),
            _const_spec((D_MODEL, D_MODEL)), _const_spec((1, D_MODEL)), _const_spec((1, D_MODEL)),
            _const_spec((D_MODEL, D_MODEL)), _const_spec((D_MODEL, D_MODEL)),
            _const_spec((1, D_MODEL)),
        ],
        out_specs=row(D_MODEL),
        out_shape=jax.ShapeDtypeStruct((t, D_MODEL), F32),
        compiler_params=pltpu.CompilerParams(
            dimension_semantics=("arbitrary",), vmem_limit_bytes=VMEM_LIMIT),
        name="mix_out_mem",
    )(x, o_fox_t, o_gla, kv, wout, gpost, gpre, wq, wo, gmpost)


def _pad_cols(w, width):
    return jnp.pad(w, ((0, 0), (0, width - w.shape[1])))


def _aug_base(h):
    return FOX_HEAD_DIM if h % 2 == 0 else 0


def _placement():
    pk = np.zeros((LANES, FOX_WIDTH), np.float32)
    pq = np.zeros((FOX_HEADS * AUG_ROWS, LANES), np.float32)
    for h in range(FOX_HEADS):
        kbase = (h // 2) * LANES + _aug_base(h)
        for t in range(C_TERMS):
            src = t * FOX_HEADS + h
            pq[h * AUG_ROWS + t, src] = 1.0
            pk[ONE_LANE, kbase + t] = 1.0
            pq[h * AUG_ROWS + C_TERMS + t, ONE_LANE] = 1.0
            pk[src, kbase + C_TERMS + t] = -1.0
    return jnp.asarray(pk, BF16), jnp.asarray(pq, BF16)


def kernel(x, mem, g_ff1_pre, w_ff1_gate, w_ff1_up, w_ff1_down, g_ff1_post, g_mix_pre, w_mix_in, b_fox_f, w_gla_g2, b_gla_g, g_fox_out, g_gla_out, w_mix_out, g_mix_post, g_mem_pre, g_mem_src, w_mem_q, w_mem_kv, w_mem_o, g_mem_post, g_ff2_pre, w_ff2_gate, w_ff2_up, w_ff2_down, g_ff2_post, g_final):
    batch, seq, d = x.shape
    mem_len = mem.shape[1]
    depth = w_ff1_gate.shape[0]
    xt = x.reshape(batch * seq, d)
    memt = mem.reshape(batch * mem_len, d)
    gfin = g_final.reshape(1, d)
    vec = lambda v: v.reshape(1, -1).astype(F32)
    pk, pqt = _placement()
    for l in range(depth):
        last = l == depth - 1
        xt = _ffn(xt, vec(g_ff1_pre[l]), w_ff1_gate[l].astype(BF16), w_ff1_up[l].astype(BF16),
                  w_ff1_down[l].astype(BF16), vec(g_ff1_post[l]), gfin, False)

        w_in = w_mix_in[l]
        wqt = (w_in[:, OFF_FQ:OFF_FK] * (LOG2E * FOX_HEAD_DIM ** -0.5)).T.astype(BF16)
        wk = w_in[:, OFF_FK:OFF_FV].astype(BF16)
        wvt = w_in[:, OFF_FV:OFF_FF].T.astype(BF16)
        wgla = w_in[:, OFF_GQ:OFF_GG].astype(BF16)
        wgr = w_in[:, OFF_GR:IN_WIDTH].astype(BF16)
        wsm = _pad_cols(jnp.concatenate([w_in[:, OFF_FF:OFF_GQ], w_in[:, OFF_GG:OFF_GR]], axis=1),
                        LANES).astype(BF16)
        bf = _pad_cols(vec(b_fox_f[l]), LANES)
        wg2 = jnp.pad(w_gla_g2[l].astype(F32),
                      ((FOX_HEADS, LANES - FOX_HEADS - GLA_GATE_RANK), (0, 0)))
        k, qt, vt, gqk, gv, la, sr = _mix_in(xt, vec(g_mix_pre[l]), wk, wqt, wvt, wgla, wgr, wsm,
                                             bf, pk, pqt, wg2, vec(b_gla_g[l]), seq)

        g_fox = jnp.broadcast_to(g_fox_out[l].astype(F32)[:, None], (FOX_WIDTH, TQ))
        o_fox_t = _fox(qt, k, vt, g_fox, batch, seq)
        bsd = lambda a: a.reshape(batch, seq, a.shape[-1])
        o_gla = _gla(bsd(gqk), bsd(gv), bsd(la), bsd(sr), vec(g_gla_out[l]), batch, seq)
        o_gla = o_gla.reshape(batch * seq, GLA_WIDTH)

        kv = _mem_kv(memt, vec(g_mem_src[l]), w_mem_kv[l].astype(BF16), mem_len)
        xt = _mix_out_mem(xt, o_fox_t, o_gla, kv, w_mix_out[l].astype(BF16), vec(g_mix_post[l]),
                          vec(g_mem_pre[l]), w_mem_q[l].astype(BF16), w_mem_o[l].astype(BF16),
                          vec(g_mem_post[l]), seq, mem_len)

        xt = _ffn(xt, vec(g_ff2_pre[l]), w_ff2_gate[l].astype(BF16), w_ff2_up[l].astype(BF16),
                  w_ff2_down[l].astype(BF16), vec(g_ff2_post[l]), gfin, last)
    return xt.reshape(batch, seq, d)
```

```python
import functools

import numpy as np
import jax
import jax.numpy as jnp
from jax import lax
from jax.experimental import pallas as pl
from jax.experimental.pallas import tpu as pltpu

F32 = jnp.float32
BF16 = jnp.bfloat16

D_MODEL = 1024
CHUNK = 64
FOX_WIDTH = 512
FOX_HEADS = 8
FOX_HEAD_DIM = 64
GLA_WIDTH = 512
GLA_HEADS = 4
GLA_VAL_DIM = 128
GLA_KEY_DIM = 64
GLA_QK = 256
GLA_GATE_RANK = 16
GLA_GATE_NORM = 16.0
MEM_HEADS = 4
MEM_HEAD_DIM = 256
D_FF = 2816
RMS_EPS = 1e-6

OFF_FQ = 0
OFF_FK = OFF_FQ + FOX_WIDTH
OFF_FV = OFF_FK + FOX_WIDTH
OFF_FF = OFF_FV + FOX_WIDTH
OFF_GQ = OFF_FF + FOX_HEADS
OFF_GK = OFF_GQ + GLA_QK
OFF_GV = OFF_GK + GLA_QK
OFF_GG = OFF_GV + GLA_WIDTH
OFF_GR = OFF_GG + GLA_GATE_RANK
IN_WIDTH = OFF_GR + GLA_WIDTH

LANES = 128
VMEM_LIMIT = 56 * 1024 * 1024

TM = 512
TMF = 1024
TQ = 256
TG = 512
GLA_SEQS = 4
GLA_SUB = 256
SCORE_LEAD = 1
FFN_GROUPS = 2
TMO = 1024
MIX_GROUPS = 2

FOX_PAD = FOX_HEADS * LANES
C_TERMS = 3
ONE_LANE = C_TERMS * FOX_HEADS
AUG_ROWS = 16
V_ONE_ROW = FOX_HEAD_DIM
V_ROWS = FOX_HEAD_DIM + AUG_ROWS
LOG2E = 1.4426950408889634


def _rms(x, g):
    return x * lax.rsqrt(jnp.mean(x * x, axis=-1, keepdims=True) + RMS_EPS) * g


def _dot(a, b):
    return jnp.dot(a, b, preferred_element_type=F32)


def _dot_nt(a, b):
    return lax.dot_general(a, b, (((1,), (1,)), ((), ())), preferred_element_type=F32)


def _dot_tn(a, b):
    return lax.dot_general(a, b, (((0,), (0,)), ((), ())), preferred_element_type=F32)


def _split3(x):
    hi = x.astype(BF16)
    r1 = x - hi.astype(F32)
    mid = r1.astype(BF16)
    lo = (r1 - mid.astype(F32)).astype(BF16)
    return hi, mid, lo


def _tri_cumsum(tri_bf16, x):
    hi, mid, lo = _split3(x)
    return _dot(tri_bf16, hi) + _dot(tri_bf16, mid) + _dot(tri_bf16, lo)


def _pack_terms(x, one_lane=None):
    hi, mid, lo = _split3(x)
    out = (hi.astype(F32) + pltpu.roll(mid.astype(F32), FOX_HEADS, 1)
           + pltpu.roll(lo.astype(F32), 2 * FOX_HEADS, 1))
    if one_lane is not None:
        lane = lax.broadcasted_iota(jnp.int32, (1, LANES), 1)
        out = out + jnp.where(lane == one_lane, 1.0, 0.0)
    return out.astype(BF16)


def _log_sigmoid(x):
    return jnp.minimum(x, 0.0) - jnp.log1p(jnp.exp(-jnp.abs(x)))


def _silu(x):
    return x * jax.nn.sigmoid(x)


def _const_spec(shape):
    return pl.BlockSpec(shape, lambda *_: (0,) * len(shape), pipeline_mode=pl.Buffered(1))


def _ffn_kernel(x_ref, gpre_ref, wg_ref, wu_ref, wd_ref, gpost_ref, gfin_ref, o_ref, *, final):
    groups = [slice(n * TMF // FFN_GROUPS, (n + 1) * TMF // FFN_GROUPS) for n in range(FFN_GROUPS)]
    x = [x_ref[r, :] for r in groups]
    h = [_rms(xn, gpre_ref[...]).astype(BF16) for xn in x]
    a = [(_silu(_dot(hn, wg_ref[...])) * _dot(hn, wu_ref[...])).astype(BF16) for hn in h]
    y = [_dot(an, wd_ref[...]) for an in a]
    for n, r in enumerate(groups):
        out = x[n] + 0.5 * _rms(y[n], gpost_ref[...])
        if final:
            out = _rms(out, gfin_ref[...])
        o_ref[r, :] = out


def _ffn(x, gpre, wg, wu, wd, gpost, gfin, final):
    t = x.shape[0]
    return pl.pallas_call(
        functools.partial(_ffn_kernel, final=final),
        grid=(t // TMF,),
        in_specs=[
            pl.BlockSpec((TMF, D_MODEL), lambda i: (i, 0)),
            _const_spec((1, D_MODEL)),
            _const_spec((D_MODEL, D_FF)),
            _const_spec((D_MODEL, D_FF)),
            _const_spec((D_FF, D_MODEL)),
            _const_spec((1, D_MODEL)),
            _const_spec((1, D_MODEL)),
        ],
        out_specs=pl.BlockSpec((TMF, D_MODEL), lambda i: (i, 0)),
        out_shape=jax.ShapeDtypeStruct((t, D_MODEL), F32),
        compiler_params=pltpu.CompilerParams(
            dimension_semantics=("arbitrary",), vmem_limit_bytes=VMEM_LIMIT),
        name="ffn_final" if final else "ffn",
    )(x, gpre, wg, wu, wd, gpost, gfin)


def _mix_in_kernel(x_ref, g_ref, wk_ref, wqt_ref, wvt_ref, wgla_ref, wgr_ref, wsm_ref,
                   bf_ref, pk_ref, pqt_ref, wg2_ref, bg_ref,
                   k_ref, qt_ref, vt_ref, gqk_ref, gv_ref, la_ref, sr_ref,
                   carry_ref, *, tiles_per_seq):
    i = pl.program_id(0)

    @pl.when(i % tiles_per_seq == 0)
    def _():
        carry_ref[...] = jnp.zeros_like(carry_ref)

    h = _rms(x_ref[...], g_ref[...]).astype(BF16)
    gl = _dot(h, wgla_ref[...])
    gqk_ref[...] = gl[:, :2 * GLA_QK].astype(BF16)
    gv_ref[...] = gl[:, 2 * GLA_QK:].astype(BF16)
    sr_ref[...] = _silu(_dot(h, wgr_ref[...])).astype(BF16)

    sm = _dot(h, wsm_ref[...])

    lane = lax.broadcasted_iota(jnp.int32, (1, LANES), 1)
    lf = jnp.where(lane < FOX_HEADS, _log_sigmoid(sm + bf_ref[...]), 0.0)
    row = lax.broadcasted_iota(jnp.int32, (TM, TM), 0)
    col = lax.broadcasted_iota(jnp.int32, (TM, TM), 1)
    tri = jnp.where(col <= row, 1.0, 0.0).astype(BF16)
    c3 = _dot(tri, _pack_terms(lf))
    c = jnp.where(lane < FOX_HEADS,
                  c3 + pltpu.roll(c3, LANES - FOX_HEADS, 1) + pltpu.roll(c3, LANES - 2 * FOX_HEADS, 1),
                  0.0) + carry_ref[...]
    carry_ref[...] = c[TM - 1:TM, :]
    packed = _pack_terms(c * LOG2E, one_lane=ONE_LANE)

    kp = _dot(h, wk_ref[...])
    kaug = _dot(packed, pk_ref[...])
    low = lane < FOX_HEAD_DIM
    for hd in range(FOX_HEADS):
        pair = slice((hd // 2) * LANES, (hd // 2 + 1) * LANES)
        slot = slice(hd * LANES, (hd + 1) * LANES)
        own = low if hd % 2 == 0 else jnp.logical_not(low)
        k_ref[:, slot] = jnp.where(own, kp[:, pair], kaug[:, pair]).astype(BF16)

    qt = _dot_nt(wqt_ref[...], h).astype(BF16)
    aug = _dot_nt(pqt_ref[...], packed).astype(BF16)
    vt = _dot_nt(wvt_ref[...], h).astype(BF16)
    one_rows = jnp.where(lax.broadcasted_iota(jnp.int32, (AUG_ROWS, TQ), 0) == 0,
                         1.0, 0.0).astype(BF16)
    zeros = jnp.zeros((LANES - FOX_HEAD_DIM - AUG_ROWS, TQ), BF16)
    for n in range(TM // TQ):
        cs = slice(n * TQ, (n + 1) * TQ)
        for hd in range(FOX_HEADS):
            q_h = qt[hd * FOX_HEAD_DIM:(hd + 1) * FOX_HEAD_DIM, cs]
            a_h = aug[hd * AUG_ROWS:(hd + 1) * AUG_ROWS, cs]
            base = hd * LANES
            pieces = (q_h, a_h, zeros) if hd % 2 == 0 else (a_h, zeros, q_h)
            r = base
            for piece in pieces:
                qt_ref[n, r:r + piece.shape[0], :] = piece
                r += piece.shape[0]
            vb = hd * V_ROWS
            vt_ref[n, vb:vb + FOX_HEAD_DIM, :] = vt[hd * FOX_HEAD_DIM:(hd + 1) * FOX_HEAD_DIM, cs]
            vt_ref[n, vb + FOX_HEAD_DIM:vb + V_ROWS, :] = one_rows

    g_hi, g_mid, _ = _split3(sm)
    w_hi, w_mid, _ = _split3(wg2_ref[...])
    gate = _dot(g_hi, w_hi) + _dot(g_mid, w_hi) + _dot(g_hi, w_mid) + bg_ref[...]
    la_ref[...] = _log_sigmoid(gate) / GLA_GATE_NORM


def _mix_in(x, g, wk, wqt, wvt, wgla, wgr, wsm, bf, pk, pqt, wg2, bg, seq):
    t = x.shape[0]
    row = lambda w: pl.BlockSpec((TM, w), lambda i: (i, 0))
    tblk = lambda rows: pl.BlockSpec((TM // TQ, rows, TQ), lambda i: (i, 0, 0))
    return pl.pallas_call(
        functools.partial(_mix_in_kernel, tiles_per_seq=seq // TM),
        grid=(t // TM,),
        in_specs=[
            row(D_MODEL),
            _const_spec((1, D_MODEL)),
            _const_spec((D_MODEL, FOX_WIDTH)),
            _const_spec((FOX_WIDTH, D_MODEL)),
            _const_spec((FOX_WIDTH, D_MODEL)),
            _const_spec((D_MODEL, 2 * GLA_QK + GLA_WIDTH)),
            _const_spec((D_MODEL, GLA_WIDTH)),
            _const_spec((D_MODEL, LANES)),
            _const_spec((1, LANES)),
            _const_spec((LANES, FOX_WIDTH)),
            _const_spec((FOX_HEADS * AUG_ROWS, LANES)),
            _const_spec((LANES, GLA_QK)),
            _const_spec((1, GLA_QK)),
        ],
        out_specs=[row(FOX_PAD), tblk(FOX_PAD), tblk(FOX_HEADS * V_ROWS), row(2 * GLA_QK),
                   row(GLA_WIDTH), row(GLA_QK), row(GLA_WIDTH)],
        out_shape=[
            jax.ShapeDtypeStruct((t, FOX_PAD), BF16),
            jax.ShapeDtypeStruct((t // TQ, FOX_PAD, TQ), BF16),
            jax.ShapeDtypeStruct((t // TQ, FOX_HEADS * V_ROWS, TQ), BF16),
            jax.ShapeDtypeStruct((t, 2 * GLA_QK), BF16),
            jax.ShapeDtypeStruct((t, GLA_WIDTH), BF16),
            jax.ShapeDtypeStruct((t, GLA_QK), F32),
            jax.ShapeDtypeStruct((t, GLA_WIDTH), BF16),
        ],
        scratch_shapes=[pltpu.VMEM((1, LANES), F32)],
        compiler_params=pltpu.CompilerParams(
            dimension_semantics=("arbitrary",), vmem_limit_bytes=VMEM_LIMIT),
        name="mix_in",
    )(x, g, wk, wqt, wvt, wgla, wgr, wsm, bf, pk, pqt, wg2, bg)


def _fox_kernel(qt_ref, k_ref, vt_ref, g_ref, o_ref, m_ref, acc_ref, sa_ref, sb_ref):
    i = pl.program_id(1)
    m_ref[...] = jnp.full(m_ref.shape, -jnp.inf, F32)
    acc_ref[...] = jnp.zeros(acc_ref.shape, F32)
    key = lax.broadcasted_iota(jnp.int32, (TQ, TQ), 0)
    qry = lax.broadcasted_iota(jnp.int32, (TQ, TQ), 1)
    causal = key <= qry
    slots = [slice(h * LANES, (h + 1) * LANES) for h in range(FOX_HEADS)]
    vrows = [slice(h * V_ROWS, (h + 1) * V_ROWS) for h in range(FOX_HEADS)]
    both = (0, 1)

    def score(j, s_ref, g, h):
        start = pl.multiple_of(j * TQ, TQ)
        s_ref[g, h] = _dot(k_ref[pl.ds(start, TQ), slots[h]], qt_ref[g, slots[h], :])

    def accumulate(j, s_ref, g, h, diagonal):
        s = s_ref[g, h]
        if diagonal:
            s = jnp.where(causal, s, -jnp.inf)
        m_old = m_ref[g, h:h + 1, :]
        m_new = jnp.maximum(m_old, jnp.max(s, axis=0, keepdims=True))
        alpha = jnp.exp2(m_old - m_new)
        p = jnp.exp2(s - m_new).astype(BF16)
        m_ref[g, h:h + 1, :] = m_new
        acc_ref[g, h] = alpha * acc_ref[g, h] + _dot(vt_ref[j, vrows[h], :], p)

    def stage(cur, nxt):
        todo = [(g, h) for g in nxt[2] for h in range(FOX_HEADS)] if nxt else []
        units = [(g, h) for g in cur[2] for h in range(FOX_HEADS)] if cur else []
        for g, h in todo[:SCORE_LEAD]:
            score(nxt[0], nxt[1], g, h)
        todo = todo[SCORE_LEAD:]
        per_unit = -(-len(todo) // len(units)) if units else 0
        for g, h in units:
            accumulate(cur[0], cur[1], g, h, g == cur[3])
            for g2, h2 in todo[:per_unit]:
                score(nxt[0], nxt[1], g2, h2)
            todo = todo[per_unit:]
        for g, h in todo:
            score(nxt[0], nxt[1], g, h)

    stage(None, (0, sa_ref, both))

    def pair(t, carry):
        j = 2 * t
        stage((j, sa_ref, both, None), (j + 1, sb_ref, both))
        stage((j + 1, sb_ref, both, None), (j + 2, sa_ref, both))
        return carry

    lax.fori_loop(0, i, pair, 0)
    stage((2 * i, sa_ref, both, 0), (2 * i + 1, sb_ref, (1,)))
    stage((2 * i + 1, sb_ref, (1,), 1), None)

    for g in both:
        for h in range(FOX_HEADS):
            acc = acc_ref[g, h]
            o = acc[:FOX_HEAD_DIM, :] / acc[V_ONE_ROW:V_ONE_ROW + 1, :]
            ms = jnp.mean(o * o, axis=0, keepdims=True)
            rows = slice(h * FOX_HEAD_DIM, (h + 1) * FOX_HEAD_DIM)
            o_ref[rows, g * TQ:(g + 1) * TQ] = (o * lax.rsqrt(ms + RMS_EPS)
                                                * g_ref[rows, :]).astype(BF16)


def _fox(qt, k, vt, g, batch, seq):
    nq = seq // TQ
    steps = nq // 2
    return pl.pallas_call(
        _fox_kernel,
        grid=(batch, steps),
        in_specs=[
            pl.BlockSpec((2, FOX_PAD, TQ), lambda b, i: (b * steps + i, 0, 0)),
            pl.BlockSpec((seq, FOX_PAD), lambda b, i: (b, 0)),
            pl.BlockSpec((nq, FOX_HEADS * V_ROWS, TQ), lambda b, i: (b, 0, 0)),
            _const_spec((FOX_WIDTH, TQ)),
        ],
        out_specs=pl.BlockSpec((FOX_WIDTH, 2 * TQ), lambda b, i: (0, b * steps + i)),
        out_shape=jax.ShapeDtypeStruct((FOX_WIDTH, batch * seq), BF16),
        scratch_shapes=[pltpu.VMEM((2, FOX_HEADS, TQ), F32),
                        pltpu.VMEM((2, FOX_HEADS, V_ROWS, TQ), F32),
                        pltpu.VMEM((2, FOX_HEADS, TQ, TQ), F32),
                        pltpu.VMEM((2, FOX_HEADS, TQ, TQ), F32)],
        compiler_params=pltpu.CompilerParams(
            dimension_semantics=("arbitrary", "arbitrary"), vmem_limit_bytes=VMEM_LIMIT),
        name="fox",
    )(qt, k, vt, g)


def _gla_kernel(qk_ref, v_ref, la_ref, sr_ref, g_ref, o_ref, st_ref):
    @pl.when(pl.program_id(1) == 0)
    def _():
        st_ref[...] = jnp.zeros_like(st_ref)

    nchunk = TG // CHUNK
    seqs = range(GLA_SEQS)
    heads = range(GLA_HEADS)
    lane = lax.broadcasted_iota(jnp.int32, (1, LANES), 1)
    first = lane < GLA_KEY_DIM
    subs = [slice(u * GLA_SUB, (u + 1) * GLA_SUB) for u in range(TG // GLA_SUB)]
    row = lax.broadcasted_iota(jnp.int32, (GLA_SUB, GLA_SUB), 0)
    col = lax.broadcasted_iota(jnp.int32, (GLA_SUB, GLA_SUB), 1)
    causal = (row - col).astype(jnp.uint32) <= (row & (CHUNK - 1)).astype(jnp.uint32)
    tri = jnp.where(causal, 1.0, 0.0).astype(BF16)
    scale = GLA_KEY_DIM ** -0.5
    zero = jnp.zeros((TG, LANES), BF16)
    chunks = [slice(c * CHUNK, (c + 1) * CHUNK) for c in range(nchunk)]
    prs = [slice((h // 2) * LANES, (h // 2 + 1) * LANES) for h in heads]
    cols = [slice(h * GLA_VAL_DIM, (h + 1) * GLA_VAL_DIM) for h in heads]
    hmasks = [first if h % 2 == 0 else jnp.logical_not(first) for h in heads]

    srow = lax.broadcasted_iota(jnp.int32, (LANES, 2 * GLA_VAL_DIM), 0) // GLA_KEY_DIM
    scol = lax.broadcasted_iota(jnp.int32, (LANES, 2 * GLA_VAL_DIM), 1) // GLA_VAL_DIM
    own_block = srow == scol
    pairs = range(GLA_HEADS // 2)
    pcols = [slice(p * 2 * GLA_VAL_DIM, (p + 1) * 2 * GLA_VAL_DIM) for p in pairs]
    pk = [slice(p * LANES, (p + 1) * LANES) for p in pairs]

    b = [jnp.concatenate([_tri_cumsum(tri, la_ref[n, u, :]) for u in subs], axis=0)
         for n in seqs]
    qp, qd, kr, k_inv, decay_t = [], [], [], [], []
    for n in seqs:
        b_last3 = b[n].reshape(nchunk, CHUNK, GLA_QK)[:, CHUNK - 1:CHUNK, :]
        b_last = jnp.broadcast_to(b_last3, (nchunk, CHUNK, GLA_QK)).reshape(TG, GLA_QK)
        q = qk_ref[n, :, 0:GLA_QK].astype(F32)
        k = qk_ref[n, :, GLA_QK:2 * GLA_QK].astype(F32)
        q_dec = (q * scale * jnp.exp(b[n])).astype(BF16)
        kr.append((k * jnp.exp(b_last - b[n])).astype(BF16))
        k_inv.append((k * jnp.exp(-b[n])).astype(BF16))
        decay = jnp.exp(b_last3).reshape(nchunk, GLA_QK)
        decay_t.append(jnp.concatenate(
            [decay, jnp.zeros((LANES - nchunk, GLA_QK), F32)], axis=0).T)
        qp.append(q_dec)
        qd.append([jnp.where(hmasks[h], q_dec[:, prs[h]], zero) for h in heads])
    kv, att = [], []
    for n in seqs:
        kv.append([[_dot_tn(kr[n][r, pk[p]], v_ref[n, r, pcols[p]]) for r in chunks]
                   for p in pairs])
        att.append([[jnp.where(causal, _dot_nt(qd[n][h][u], k_inv[n][u, prs[h]]), 0.0).astype(BF16)
                     for u in subs] for h in heads])
    states = []
    for n in seqs:
        per_pair = []
        for p in pairs:
            st = st_ref[n, p]
            per_chunk = []
            for c in range(nchunk):
                per_chunk.append(jnp.where(own_block, st, 0.0).astype(BF16))
                st = st * decay_t[n][pk[p], c:c + 1] + kv[n][p][c]
            st_ref[n, p] = st
            per_pair.append(per_chunk)
        states.append(per_pair)
    for n in seqs:
        inter = [jnp.concatenate([_dot(qp[n][chunks[c], pk[p]], states[n][p][c])
                                  for c in range(nchunk)], axis=0) for p in pairs]
        for h in heads:
            half = slice((h % 2) * GLA_VAL_DIM, (h % 2 + 1) * GLA_VAL_DIM)
            intra = jnp.concatenate([_dot(att[n][h][ui], v_ref[n, u, cols[h]])
                                     for ui, u in enumerate(subs)], axis=0)
            o = intra + inter[h // 2][:, half]
            ms = jnp.mean(o * o, axis=-1, keepdims=True)
            y = o * lax.rsqrt(ms + RMS_EPS) * g_ref[:, cols[h]] * sr_ref[n, :, cols[h]].astype(F32)
            o_ref[n, :, cols[h]] = y.astype(BF16)


def _gla(gqk, gv, la, sr, g, batch, seq):
    blk = lambda w: pl.BlockSpec((GLA_SEQS, TG, w), lambda b, i: (b, i, 0))
    return pl.pallas_call(
        _gla_kernel,
        grid=(batch // GLA_SEQS, seq // TG),
        in_specs=[blk(2 * GLA_QK), blk(GLA_WIDTH), blk(GLA_QK), blk(GLA_WIDTH),
                  pl.BlockSpec((1, GLA_WIDTH), lambda b, i: (0, 0))],
        out_specs=blk(GLA_WIDTH),
        out_shape=jax.ShapeDtypeStruct((batch, seq, GLA_WIDTH), BF16),
        scratch_shapes=[pltpu.VMEM((GLA_SEQS, GLA_HEADS // 2, LANES, 2 * GLA_VAL_DIM), F32)],
        compiler_params=pltpu.CompilerParams(
            dimension_semantics=("arbitrary", "arbitrary"), vmem_limit_bytes=VMEM_LIMIT),
        name="gla",
    )(gqk, gv, la, sr, g)


def _mem_kv_kernel(m_ref, g_ref, w_ref, o_ref):
    h = _rms(m_ref[...], g_ref[...]).astype(BF16)
    o_ref[...] = _dot(h, w_ref[...]).astype(BF16)


def _mem_kv(mem, g, w, mem_len):
    t = mem.shape[0]
    width = w.shape[1]
    return pl.pallas_call(
        _mem_kv_kernel,
        grid=(t // mem_len,),
        in_specs=[pl.BlockSpec((mem_len, D_MODEL), lambda i: (i, 0)),
                  _const_spec((1, D_MODEL)), _const_spec((D_MODEL, width))],
        out_specs=pl.BlockSpec((mem_len, width), lambda i: (i, 0)),
        out_shape=jax.ShapeDtypeStruct((t, width), BF16),
        compiler_params=pltpu.CompilerParams(
            dimension_semantics=("arbitrary",), vmem_limit_bytes=VMEM_LIMIT),
        name="mem_kv",
    )(mem, g, w)


def _mix_out_mem_kernel(x_ref, oft_ref, og_ref, kv_ref, wout_ref, gpost_ref, gpre_ref,
                        wq_ref, wo_ref, gmpost_ref, o_ref):
    width = MEM_HEADS * MEM_HEAD_DIM
    groups = [slice(n * TMO // MIX_GROUPS, (n + 1) * TMO // MIX_GROUPS) for n in range(MIX_GROUPS)]
    y = [_dot_tn(oft_ref[:, r], wout_ref[0:FOX_WIDTH, :]) + _dot(og_ref[r, :], wout_ref[FOX_WIDTH:, :])
         for r in groups]
    x = [x_ref[r, :] + _rms(y[n], gpost_ref[...]) for n, r in enumerate(groups)]
    h = [_rms(xn, gpre_ref[...]).astype(BF16) for xn in x]
    q = [(_dot(hn, wq_ref[...]) * (MEM_HEAD_DIM ** -0.5)).astype(BF16) for hn in h]
    hcols = [slice(hd * MEM_HEAD_DIM, (hd + 1) * MEM_HEAD_DIM) for hd in range(MEM_HEADS)]
    vcols = [slice(width + hd * MEM_HEAD_DIM, width + (hd + 1) * MEM_HEAD_DIM)
             for hd in range(MEM_HEADS)]
    s = [[_dot_nt(qn[:, c], kv_ref[:, c]) for c in hcols] for qn in q]
    o = []
    for sn in s:
        outs = []
        for hd in range(MEM_HEADS):
            m = jnp.max(sn[hd], axis=-1, keepdims=True)
            p = jnp.exp(sn[hd] - m)
            p = p / jnp.sum(p, axis=-1, keepdims=True)
            outs.append(_dot(p.astype(BF16), kv_ref[:, vcols[hd]]).astype(BF16))
        o.append(jnp.concatenate(outs, axis=-1))
    y2 = [_dot(on, wo_ref[...]) for on in o]
    for n, r in enumerate(groups):
        o_ref[r, :] = x[n] + _rms(y2[n], gmpost_ref[...])


def _mix_out_mem(x, o_fox_t, o_gla, kv, wout, gpost, gpre, wq, wo, gmpost, seq, mem_len):
    t = x.shape[0]
    tiles_per_seq = seq // TMO
    row = lambda w: pl.BlockSpec((TMO, w), lambda i: (i, 0))
    return pl.pallas_call(
        _mix_out_mem_kernel,
        grid=(t // TMO,),
        in_specs=[
            row(D_MODEL),
            pl.BlockSpec((FOX_WIDTH, TMO), lambda i: (0, i)),
            row(GLA_WIDTH),
            pl.BlockSpec((mem_len, kv.shape[1]), lambda i: (i // tiles_per_seq, 0)),
            _const_spec((D_MODEL, D_MODEL)), _const_spec((1, D_MODEL)), _const_spec((1, D_MODEL)),
            _const_spec((D_MODEL, D_MODEL)), _const_spec((D_MODEL, D_MODEL)),
            _const_spec((1, D_MODEL)),
        ],
        out_specs=row(D_MODEL),
        out_shape=jax.ShapeDtypeStruct((t, D_MODEL), F32),
        compiler_params=pltpu.CompilerParams(
            dimension_semantics=("arbitrary",), vmem_limit_bytes=VMEM_LIMIT),
        name="mix_out_mem",
    )(x, o_fox_t, o_gla, kv, wout, gpost, gpre, wq, wo, gmpost)


def _pad_cols(w, width):
    return jnp.pad(w, ((0, 0), (0, width - w.shape[1])))


def _aug_base(h):
    return FOX_HEAD_DIM if h % 2 == 0 else 0


def _placement():
    pk = np.zeros((LANES, FOX_WIDTH), np.float32)
    pq = np.zeros((FOX_HEADS * AUG_ROWS, LANES), np.float32)
    for h in range(FOX_HEADS):
        kbase = (h // 2) * LANES + _aug_base(h)
        for t in range(C_TERMS):
            src = t * FOX_HEADS + h
            pq[h * AUG_ROWS + t, src] = 1.0
            pk[ONE_LANE, kbase + t] = 1.0
            pq[h * AUG_ROWS + C_TERMS + t, ONE_LANE] = 1.0
            pk[src, kbase + C_TERMS + t] = -1.0
    return jnp.asarray(pk, BF16), jnp.asarray(pq, BF16)


def kernel(x, mem, g_ff1_pre, w_ff1_gate, w_ff1_up, w_ff1_down, g_ff1_post, g_mix_pre, w_mix_in, b_fox_f, w_gla_g2, b_gla_g, g_fox_out, g_gla_out, w_mix_out, g_mix_post, g_mem_pre, g_mem_src, w_mem_q, w_mem_kv, w_mem_o, g_mem_post, g_ff2_pre, w_ff2_gate, w_ff2_up, w_ff2_down, g_ff2_post, g_final):
    batch, seq, d = x.shape
    mem_len = mem.shape[1]
    depth = w_ff1_gate.shape[0]
    xt = x.reshape(batch * seq, d)
    memt = mem.reshape(batch * mem_len, d)
    gfin = g_final.reshape(1, d)
    vec = lambda v: v.reshape(1, -1).astype(F32)
    pk, pqt = _placement()
    for l in range(depth):
        last = l == depth - 1
        xt = _ffn(xt, vec(g_ff1_pre[l]), w_ff1_gate[l].astype(BF16), w_ff1_up[l].astype(BF16),
                  w_ff1_down[l].astype(BF16), vec(g_ff1_post[l]), gfin, False)

        w_in = w_mix_in[l]
        wqt = (w_in[:, OFF_FQ:OFF_FK] * (LOG2E * FOX_HEAD_DIM ** -0.5)).T.astype(BF16)
        wk = w_in[:, OFF_FK:OFF_FV].astype(BF16)
        wvt = w_in[:, OFF_FV:OFF_FF].T.astype(BF16)
        wgla = w_in[:, OFF_GQ:OFF_GG].astype(BF16)
        wgr = w_in[:, OFF_GR:IN_WIDTH].astype(BF16)
        wsm = _pad_cols(jnp.concatenate([w_in[:, OFF_FF:OFF_GQ], w_in[:, OFF_GG:OFF_GR]], axis=1),
                        LANES).astype(BF16)
        bf = _pad_cols(vec(b_fox_f[l]), LANES)
        wg2 = jnp.pad(w_gla_g2[l].astype(F32),
                      ((FOX_HEADS, LANES - FOX_HEADS - GLA_GATE_RANK), (0, 0)))
        k, qt, vt, gqk, gv, la, sr = _mix_in(xt, vec(g_mix_pre[l]), wk, wqt, wvt, wgla, wgr, wsm,
                                             bf, pk, pqt, wg2, vec(b_gla_g[l]), seq)

        g_fox = jnp.broadcast_to(g_fox_out[l].astype(F32)[:, None], (FOX_WIDTH, TQ))
        o_fox_t = _fox(qt, k, vt, g_fox, batch, seq)
        bsd = lambda a: a.reshape(batch, seq, a.shape[-1])
        o_gla = _gla(bsd(gqk), bsd(gv), bsd(la), bsd(sr), vec(g_gla_out[l]), batch, seq)
        o_gla = o_gla.reshape(batch * seq, GLA_WIDTH)

        kv = _mem_kv(memt, vec(g_mem_src[l]), w_mem_kv[l].astype(BF16), mem_len)
        xt = _mix_out_mem(xt, o_fox_t, o_gla, kv, w_mix_out[l].astype(BF16), vec(g_mix_post[l]),
                          vec(g_mem_pre[l]), w_mem_q[l].astype(BF16), w_mem_o[l].astype(BF16),
                          vec(g_mem_post[l]), seq, mem_len)

        xt = _ffn(xt, vec(g_ff2_pre[l]), w_ff2_gate[l].astype(BF16), w_ff2_up[l].astype(BF16),
                  w_ff2_down[l].astype(BF16), vec(g_ff2_post[l]), gfin, last)
    return xt.reshape(batch, seq, d)
```

```python
import functools

import numpy as np
import jax
import jax.numpy as jnp
from jax import lax
from jax.experimental import pallas as pl
from jax.experimental.pallas import tpu as pltpu

F32 = jnp.float32
BF16 = jnp.bfloat16

D_MODEL = 1024
CHUNK = 64
FOX_WIDTH = 512
FOX_HEADS = 8
FOX_HEAD_DIM = 64
GLA_WIDTH = 512
GLA_HEADS = 4
GLA_VAL_DIM = 128
GLA_KEY_DIM = 64
GLA_QK = 256
GLA_GATE_RANK = 16
GLA_GATE_NORM = 16.0
MEM_HEADS = 4
MEM_HEAD_DIM = 256
D_FF = 2816
RMS_EPS = 1e-6

OFF_FQ = 0
OFF_FK = OFF_FQ + FOX_WIDTH
OFF_FV = OFF_FK + FOX_WIDTH
OFF_FF = OFF_FV + FOX_WIDTH
OFF_GQ = OFF_FF + FOX_HEADS
OFF_GK = OFF_GQ + GLA_QK
OFF_GV = OFF_GK + GLA_QK
OFF_GG = OFF_GV + GLA_WIDTH
OFF_GR = OFF_GG + GLA_GATE_RANK
IN_WIDTH = OFF_GR + GLA_WIDTH

LANES = 128
VMEM_LIMIT = 56 * 1024 * 1024

TM = 512
TMF = 1024
TQ = 256
TG = 512
GLA_SEQS = 4
GLA_SUB = 256
SCORE_LEAD = 1
FFN_GROUPS = 2
TMO = 1024
MIX_GROUPS = 2

FOX_PAD = FOX_HEADS * LANES
C_TERMS = 3
ONE_LANE = C_TERMS * FOX_HEADS
AUG_ROWS = 16
V_ONE_ROW = FOX_HEAD_DIM
V_ROWS = FOX_HEAD_DIM + AUG_ROWS
LOG2E = 1.4426950408889634


def _rms(x, g):
    return x * lax.rsqrt(jnp.mean(x * x, axis=-1, keepdims=True) + RMS_EPS) * g


def _dot(a, b):
    return jnp.dot(a, b, preferred_element_type=F32)


def _dot_nt(a, b):
    return lax.dot_general(a, b, (((1,), (1,)), ((), ())), preferred_element_type=F32)


def _dot_tn(a, b):
    return lax.dot_general(a, b, (((0,), (0,)), ((), ())), preferred_element_type=F32)


def _split3(x):
    hi = x.astype(BF16)
    r1 = x - hi.astype(F32)
    mid = r1.astype(BF16)
    lo = (r1 - mid.astype(F32)).astype(BF16)
    return hi, mid, lo


def _tri_cumsum(tri_bf16, x):
    hi, mid, lo = _split3(x)
    return _dot(tri_bf16, hi) + _dot(tri_bf16, mid) + _dot(tri_bf16, lo)


def _pack_terms(x, one_lane=None):
    hi, mid, lo = _split3(x)
    out = (hi.astype(F32) + pltpu.roll(mid.astype(F32), FOX_HEADS, 1)
           + pltpu.roll(lo.astype(F32), 2 * FOX_HEADS, 1))
    if one_lane is not None:
        lane = lax.broadcasted_iota(jnp.int32, (1, LANES), 1)
        out = out + jnp.where(lane == one_lane, 1.0, 0.0)
    return out.astype(BF16)


def _log_sigmoid(x):
    return jnp.minimum(x, 0.0) - jnp.log1p(jnp.exp(-jnp.abs(x)))


def _silu(x):
    return x * jax.nn.sigmoid(x)


def _const_spec(shape):
    return pl.BlockSpec(shape, lambda *_: (0,) * len(shape), pipeline_mode=pl.Buffered(1))


def _ffn_kernel(x_ref, gpre_ref, wg_ref, wu_ref, wd_ref, gpost_ref, gfin_ref, o_ref, *, final):
    groups = [slice(n * TMF // FFN_GROUPS, (n + 1) * TMF // FFN_GROUPS) for n in range(FFN_GROUPS)]
    x = [x_ref[r, :] for r in groups]
    h = [_rms(xn, gpre_ref[...]).astype(BF16) for xn in x]
    a = [(_silu(_dot(hn, wg_ref[...])) * _dot(hn, wu_ref[...])).astype(BF16) for hn in h]
    y = [_dot(an, wd_ref[...]) for an in a]
    for n, r in enumerate(groups):
        out = x[n] + 0.5 * _rms(y[n], gpost_ref[...])
        if final:
            out = _rms(out, gfin_ref[...])
        o_ref[r, :] = out


def _ffn(x, gpre, wg, wu, wd, gpost, gfin, final):
    t = x.shape[0]
    return pl.pallas_call(
        functools.partial(_ffn_kernel, final=final),
        grid=(t // TMF,),
        in_specs=[
            pl.BlockSpec((TMF, D_MODEL), lambda i: (i, 0)),
            _const_spec((1, D_MODEL)),
            _const_spec((D_MODEL, D_FF)),
            _const_spec((D_MODEL, D_FF)),
            _const_spec((D_FF, D_MODEL)),
            _const_spec((1, D_MODEL)),
            _const_spec((1, D_MODEL)),
        ],
        out_specs=pl.BlockSpec((TMF, D_MODEL), lambda i: (i, 0)),
        out_shape=jax.ShapeDtypeStruct((t, D_MODEL), F32),
        compiler_params=pltpu.CompilerParams(
            dimension_semantics=("arbitrary",), vmem_limit_bytes=VMEM_LIMIT),
        name="ffn_final" if final else "ffn",
    )(x, gpre, wg, wu, wd, gpost, gfin)


def _mix_in_kernel(x_ref, g_ref, wk_ref, wqt_ref, wvt_ref, wgla_ref, wgr_ref, wsm_ref,
                   bf_ref, pk_ref, pqt_ref, wg2_ref, bg_ref,
                   k_ref, qt_ref, vt_ref, gqk_ref, gv_ref, la_ref, sr_ref,
                   carry_ref, *, tiles_per_seq):
    i = pl.program_id(0)

    @pl.when(i % tiles_per_seq == 0)
    def _():
        carry_ref[...] = jnp.zeros_like(carry_ref)

    h = _rms(x_ref[...], g_ref[...]).astype(BF16)
    gl = _dot(h, wgla_ref[...])
    gqk_ref[...] = gl[:, :2 * GLA_QK].astype(BF16)
    gv_ref[...] = gl[:, 2 * GLA_QK:].astype(BF16)
    sr_ref[...] = _silu(_dot(h, wgr_ref[...])).astype(BF16)

    sm = _dot(h, wsm_ref[...])

    lane = lax.broadcasted_iota(jnp.int32, (1, LANES), 1)
    lf = jnp.where(lane < FOX_HEADS, _log_sigmoid(sm + bf_ref[...]), 0.0)
    row = lax.broadcasted_iota(jnp.int32, (TM, TM), 0)
    col = lax.broadcasted_iota(jnp.int32, (TM, TM), 1)
    tri = jnp.where(col <= row, 1.0, 0.0).astype(BF16)
    c3 = _dot(tri, _pack_terms(lf))
    c = jnp.where(lane < FOX_HEADS,
                  c3 + pltpu.roll(c3, LANES - FOX_HEADS, 1) + pltpu.roll(c3, LANES - 2 * FOX_HEADS, 1),
                  0.0) + carry_ref[...]
    carry_ref[...] = c[TM - 1:TM, :]
    packed = _pack_terms(c * LOG2E, one_lane=ONE_LANE)

    kp = _dot(h, wk_ref[...])
    kaug = _dot(packed, pk_ref[...])
    low = lane < FOX_HEAD_DIM
    for hd in range(FOX_HEADS):
        pair = slice((hd // 2) * LANES, (hd // 2 + 1) * LANES)
        slot = slice(hd * LANES, (hd + 1) * LANES)
        own = low if hd % 2 == 0 else jnp.logical_not(low)
        k_ref[:, slot] = jnp.where(own, kp[:, pair], kaug[:, pair]).astype(BF16)

    qt = _dot_nt(wqt_ref[...], h).astype(BF16)
    aug = _dot_nt(pqt_ref[...], packed).astype(BF16)
    vt = _dot_nt(wvt_ref[...], h).astype(BF16)
    one_rows = jnp.where(lax.broadcasted_iota(jnp.int32, (AUG_ROWS, TQ), 0) == 0,
                         1.0, 0.0).astype(BF16)
    zeros = jnp.zeros((LANES - FOX_HEAD_DIM - AUG_ROWS, TQ), BF16)
    for n in range(TM // TQ):
        cs = slice(n * TQ, (n + 1) * TQ)
        for hd in range(FOX_HEADS):
            q_h = qt[hd * FOX_HEAD_DIM:(hd + 1) * FOX_HEAD_DIM, cs]
            a_h = aug[hd * AUG_ROWS:(hd + 1) * AUG_ROWS, cs]
            base = hd * LANES
            pieces = (q_h, a_h, zeros) if hd % 2 == 0 else (a_h, zeros, q_h)
            r = base
            for piece in pieces:
                qt_ref[n, r:r + piece.shape[0], :] = piece
                r += piece.shape[0]
            vb = hd * V_ROWS
            vt_ref[n, vb:vb + FOX_HEAD_DIM, :] = vt[hd * FOX_HEAD_DIM:(hd + 1) * FOX_HEAD_DIM, cs]
            vt_ref[n, vb + FOX_HEAD_DIM:vb + V_ROWS, :] = one_rows

    g_hi, g_mid, _ = _split3(sm)
    w_hi, w_mid, _ = _split3(wg2_ref[...])
    gate = _dot(g_hi, w_hi) + _dot(g_mid, w_hi) + _dot(g_hi, w_mid) + bg_ref[...]
    la_ref[...] = _log_sigmoid(gate) / GLA_GATE_NORM


def _mix_in(x, g, wk, wqt, wvt, wgla, wgr, wsm, bf, pk, pqt, wg2, bg, seq):
    t = x.shape[0]
    row = lambda w: pl.BlockSpec((TM, w), lambda i: (i, 0))
    tblk = lambda rows: pl.BlockSpec((TM // TQ, rows, TQ), lambda i: (i, 0, 0))
    return pl.pallas_call(
        functools.partial(_mix_in_kernel, tiles_per_seq=seq // TM),
        grid=(t // TM,),
        in_specs=[
            row(D_MODEL),
            _const_spec((1, D_MODEL)),
            _const_spec((D_MODEL, FOX_WIDTH)),
            _const_spec((FOX_WIDTH, D_MODEL)),
            _const_spec((FOX_WIDTH, D_MODEL)),
            _const_spec((D_MODEL, 2 * GLA_QK + GLA_WIDTH)),
            _const_spec((D_MODEL, GLA_WIDTH)),
            _const_spec((D_MODEL, LANES)),
            _const_spec((1, LANES)),
            _const_spec((LANES, FOX_WIDTH)),
            _const_spec((FOX_HEADS * AUG_ROWS, LANES)),
            _const_spec((LANES, GLA_QK)),
            _const_spec((1, GLA_QK)),
        ],
        out_specs=[row(FOX_PAD), tblk(FOX_PAD), tblk(FOX_HEADS * V_ROWS), row(2 * GLA_QK),
                   row(GLA_WIDTH), row(GLA_QK), row(GLA_WIDTH)],
        out_shape=[
            jax.ShapeDtypeStruct((t, FOX_PAD), BF16),
            jax.ShapeDtypeStruct((t // TQ, FOX_PAD, TQ), BF16),
            jax.ShapeDtypeStruct((t // TQ, FOX_HEADS * V_ROWS, TQ), BF16),
            jax.ShapeDtypeStruct((t, 2 * GLA_QK), BF16),
            jax.ShapeDtypeStruct((t, GLA_WIDTH), BF16),
            jax.ShapeDtypeStruct((t, GLA_QK), F32),
            jax.ShapeDtypeStruct((t, GLA_WIDTH), BF16),
        ],
        scratch_shapes=[pltpu.VMEM((1, LANES), F32)],
        compiler_params=pltpu.CompilerParams(
            dimension_semantics=("arbitrary",), vmem_limit_bytes=VMEM_LIMIT),
        name="mix_in",
    )(x, g, wk, wqt, wvt, wgla, wgr, wsm, bf, pk, pqt, wg2, bg)


def _fox_kernel(qt_ref, k_ref, vt_ref, g_ref, o_ref, m_ref, acc_ref, sa_ref, sb_ref):
    i = pl.program_id(1)
    m_ref[...] = jnp.full(m_ref.shape, -jnp.inf, F32)
    acc_ref[...] = jnp.zeros(acc_ref.shape, F32)
    key = lax.broadcasted_iota(jnp.int32, (TQ, TQ), 0)
    qry = lax.broadcasted_iota(jnp.int32, (TQ, TQ), 1)
    causal = key <= qry
    slots = [slice(h * LANES, (h + 1) * LANES) for h in range(FOX_HEADS)]
    vrows = [slice(h * V_ROWS, (h + 1) * V_ROWS) for h in range(FOX_HEADS)]
    both = (0, 1)

    def score(j, s_ref, g, h):
        start = pl.multiple_of(j * TQ, TQ)
        s_ref[g, h] = _dot(k_ref[pl.ds(start, TQ), slots[h]], qt_ref[g, slots[h], :])

    def accumulate(j, s_ref, g, h, diagonal):
        s = s_ref[g, h]
        if diagonal:
            s = jnp.where(causal, s, -jnp.inf)
        m_old = m_ref[g, h:h + 1, :]
        m_new = jnp.maximum(m_old, jnp.max(s, axis=0, keepdims=True))
        alpha = jnp.exp2(m_old - m_new)
        p = jnp.exp2(s - m_new).astype(BF16)
        m_ref[g, h:h + 1, :] = m_new
        acc_ref[g, h] = alpha * acc_ref[g, h] + _dot(vt_ref[j, vrows[h], :], p)

    def stage(cur, nxt):
        todo = [(g, h) for g in nxt[2] for h in range(FOX_HEADS)] if nxt else []
        units = [(g, h) for g in cur[2] for h in range(FOX_HEADS)] if cur else []
        for g, h in todo[:SCORE_LEAD]:
            score(nxt[0], nxt[1], g, h)
        todo = todo[SCORE_LEAD:]
        per_unit = -(-len(todo) // len(units)) if units else 0
        for g, h in units:
            accumulate(cur[0], cur[1], g, h, g == cur[3])
            for g2, h2 in todo[:per_unit]:
                score(nxt[0], nxt[1], g2, h2)
            todo = todo[per_unit:]
        for g, h in todo:
            score(nxt[0], nxt[1], g, h)

    stage(None, (0, sa_ref, both))

    def pair(t, carry):
        j = 2 * t
        stage((j, sa_ref, both, None), (j + 1, sb_ref, both))
        stage((j + 1, sb_ref, both, None), (j + 2, sa_ref, both))
        return carry

    def quad(t, carry):
        pair(2 * t, carry)
        return pair(2 * t + 1, carry)

    lax.fori_loop(0, i // 2, quad, 0)

    @pl.when(i % 2 == 1)
    def _():
        pair(i - 1, 0)

    stage((2 * i, sa_ref, both, 0), (2 * i + 1, sb_ref, (1,)))
    stage((2 * i + 1, sb_ref, (1,), 1), None)

    for g in both:
        for h in range(FOX_HEADS):
            acc = acc_ref[g, h]
            o = acc[:FOX_HEAD_DIM, :] / acc[V_ONE_ROW:V_ONE_ROW + 1, :]
            ms = jnp.mean(o * o, axis=0, keepdims=True)
            rows = slice(h * FOX_HEAD_DIM, (h + 1) * FOX_HEAD_DIM)
            o_ref[rows, g * TQ:(g + 1) * TQ] = (o * lax.rsqrt(ms + RMS_EPS)
                                                * g_ref[rows, :]).astype(BF16)


def _fox(qt, k, vt, g, batch, seq):
    nq = seq // TQ
    steps = nq // 2
    return pl.pallas_call(
        _fox_kernel,
        grid=(batch, steps),
        in_specs=[
            pl.BlockSpec((2, FOX_PAD, TQ), lambda b, i: (b * steps + i, 0, 0)),
            pl.BlockSpec((seq, FOX_PAD), lambda b, i: (b, 0)),
            pl.BlockSpec((nq, FOX_HEADS * V_ROWS, TQ), lambda b, i: (b, 0, 0)),
            _const_spec((FOX_WIDTH, TQ)),
        ],
        out_specs=pl.BlockSpec((FOX_WIDTH, 2 * TQ), lambda b, i: (0, b * steps + i)),
        out_shape=jax.ShapeDtypeStruct((FOX_WIDTH, batch * seq), BF16),
        scratch_shapes=[pltpu.VMEM((2, FOX_HEADS, TQ), F32),
                        pltpu.VMEM((2, FOX_HEADS, V_ROWS, TQ), F32),
                        pltpu.VMEM((2, FOX_HEADS, TQ, TQ), F32),
                        pltpu.VMEM((2, FOX_HEADS, TQ, TQ), F32)],
        compiler_params=pltpu.CompilerParams(
            dimension_semantics=("arbitrary", "arbitrary"), vmem_limit_bytes=VMEM_LIMIT),
        name="fox",
    )(qt, k, vt, g)


def _gla_kernel(qk_ref, v_ref, la_ref, sr_ref, g_ref, o_ref, st_ref):
    @pl.when(pl.program_id(1) == 0)
    def _():
        st_ref[...] = jnp.zeros_like(st_ref)

    nchunk = TG // CHUNK
    seqs = range(GLA_SEQS)
    heads = range(GLA_HEADS)
    lane = lax.broadcasted_iota(jnp.int32, (1, LANES), 1)
    first = lane < GLA_KEY_DIM
    subs = [slice(u * GLA_SUB, (u + 1) * GLA_SUB) for u in range(TG // GLA_SUB)]
    row = lax.broadcasted_iota(jnp.int32, (GLA_SUB, GLA_SUB), 0)
    col = lax.broadcasted_iota(jnp.int32, (GLA_SUB, GLA_SUB), 1)
    causal = (row - col).astype(jnp.uint32) <= (row & (CHUNK - 1)).astype(jnp.uint32)
    tri = jnp.where(causal, 1.0, 0.0).astype(BF16)
    scale = GLA_KEY_DIM ** -0.5
    zero = jnp.zeros((TG, LANES), BF16)
    chunks = [slice(c * CHUNK, (c + 1) * CHUNK) for c in range(nchunk)]
    prs = [slice((h // 2) * LANES, (h // 2 + 1) * LANES) for h in heads]
    cols = [slice(h * GLA_VAL_DIM, (h + 1) * GLA_VAL_DIM) for h in heads]
    hmasks = [first if h % 2 == 0 else jnp.logical_not(first) for h in heads]

    srow = lax.broadcasted_iota(jnp.int32, (LANES, 2 * GLA_VAL_DIM), 0) // GLA_KEY_DIM
    scol = lax.broadcasted_iota(jnp.int32, (LANES, 2 * GLA_VAL_DIM), 1) // GLA_VAL_DIM
    own_block = srow == scol
    pairs = range(GLA_HEADS // 2)
    pcols = [slice(p * 2 * GLA_VAL_DIM, (p + 1) * 2 * GLA_VAL_DIM) for p in pairs]
    pk = [slice(p * LANES, (p + 1) * LANES) for p in pairs]

    b = [jnp.concatenate([_tri_cumsum(tri, la_ref[n, u, :]) for u in subs], axis=0)
         for n in seqs]
    qp, qd, kr, k_inv, decay_t = [], [], [], [], []
    for n in seqs:
        b_last3 = b[n].reshape(nchunk, CHUNK, GLA_QK)[:, CHUNK - 1:CHUNK, :]
        b_last = jnp.broadcast_to(b_last3, (nchunk, CHUNK, GLA_QK)).reshape(TG, GLA_QK)
        q = qk_ref[n, :, 0:GLA_QK].astype(F32)
        k = qk_ref[n, :, GLA_QK:2 * GLA_QK].astype(F32)
        q_dec = (q * scale * jnp.exp(b[n])).astype(BF16)
        kr.append((k * jnp.exp(b_last - b[n])).astype(BF16))
        k_inv.append((k * jnp.exp(-b[n])).astype(BF16))
        decay = jnp.exp(b_last3).reshape(nchunk, GLA_QK)
        decay_t.append(jnp.concatenate(
            [decay, jnp.zeros((LANES - nchunk, GLA_QK), F32)], axis=0).T)
        qp.append(q_dec)
        qd.append([jnp.where(hmasks[h], q_dec[:, prs[h]], zero) for h in heads])
    kv, att = [], []
    for n in seqs:
        kv.append([[_dot_tn(kr[n][r, pk[p]], v_ref[n, r, pcols[p]]) for r in chunks]
                   for p in pairs])
        att.append([[jnp.where(causal, _dot_nt(qd[n][h][u], k_inv[n][u, prs[h]]), 0.0).astype(BF16)
                     for u in subs] for h in heads])
    states = []
    for n in seqs:
        per_pair = []
        for p in pairs:
            st = st_ref[n, p]
            per_chunk = []
            for c in range(nchunk):
                per_chunk.append(jnp.where(own_block, st, 0.0).astype(BF16))
                st = st * decay_t[n][pk[p], c:c + 1] + kv[n][p][c]
            st_ref[n, p] = st
            per_pair.append(per_chunk)
        states.append(per_pair)
    for n in seqs:
        inter = [jnp.concatenate([_dot(qp[n][chunks[c], pk[p]], states[n][p][c])
                                  for c in range(nchunk)], axis=0) for p in pairs]
        for h in heads:
            half = slice((h % 2) * GLA_VAL_DIM, (h % 2 + 1) * GLA_VAL_DIM)
            intra = jnp.concatenate([_dot(att[n][h][ui], v_ref[n, u, cols[h]])
                                     for ui, u in enumerate(subs)], axis=0)
            o = intra + inter[h // 2][:, half]
            ms = jnp.mean(o * o, axis=-1, keepdims=True)
            y = o * lax.rsqrt(ms + RMS_EPS) * g_ref[:, cols[h]] * sr_ref[n, :, cols[h]].astype(F32)
            o_ref[n, :, cols[h]] = y.astype(BF16)


def _gla(gqk, gv, la, sr, g, batch, seq):
    blk = lambda w: pl.BlockSpec((GLA_SEQS, TG, w), lambda b, i: (b, i, 0))
    return pl.pallas_call(
        _gla_kernel,
        grid=(batch // GLA_SEQS, seq // TG),
        in_specs=[blk(2 * GLA_QK), blk(GLA_WIDTH), blk(GLA_QK), blk(GLA_WIDTH),
                  pl.BlockSpec((1, GLA_WIDTH), lambda b, i: (0, 0))],
        out_specs=blk(GLA_WIDTH),
        out_shape=jax.ShapeDtypeStruct((batch, seq, GLA_WIDTH), BF16),
        scratch_shapes=[pltpu.VMEM((GLA_SEQS, GLA_HEADS // 2, LANES, 2 * GLA_VAL_DIM), F32)],
        compiler_params=pltpu.CompilerParams(
            dimension_semantics=("arbitrary", "arbitrary"), vmem_limit_bytes=VMEM_LIMIT),
        name="gla",
    )(gqk, gv, la, sr, g)


def _mem_kv_kernel(m_ref, g_ref, w_ref, o_ref):
    h = _rms(m_ref[...], g_ref[...]).astype(BF16)
    o_ref[...] = _dot(h, w_ref[...]).astype(BF16)


def _mem_kv(mem, g, w, mem_len):
    t = mem.shape[0]
    width = w.shape[1]
    return pl.pallas_call(
        _mem_kv_kernel,
        grid=(t // mem_len,),
        in_specs=[pl.BlockSpec((mem_len, D_MODEL), lambda i: (i, 0)),
                  _const_spec((1, D_MODEL)), _const_spec((D_MODEL, width))],
        out_specs=pl.BlockSpec((mem_len, width), lambda i: (i, 0)),
        out_shape=jax.ShapeDtypeStruct((t, width), BF16),
        compiler_params=pltpu.CompilerParams(
            dimension_semantics=("arbitrary",), vmem_limit_bytes=VMEM_LIMIT),
        name="mem_kv",
    )(mem, g, w)


def _mix_out_mem_kernel(x_ref, oft_ref, og_ref, kv_ref, wout_ref, gpost_ref, gpre_ref,
                        wq_ref, wo_ref, gmpost_ref, o_ref):
    width = MEM_HEADS * MEM_HEAD_DIM
    groups = [slice(n * TMO // MIX_GROUPS, (n + 1) * TMO // MIX_GROUPS) for n in range(MIX_GROUPS)]
    y = [_dot_tn(oft_ref[:, r], wout_ref[0:FOX_WIDTH, :]) + _dot(og_ref[r, :], wout_ref[FOX_WIDTH:, :])
         for r in groups]
    x = [x_ref[r, :] + _rms(y[n], gpost_ref[...]) for n, r in enumerate(groups)]
    h = [_rms(xn, gpre_ref[...]).astype(BF16) for xn in x]
    q = [(_dot(hn, wq_ref[...]) * (MEM_HEAD_DIM ** -0.5)).astype(BF16) for hn in h]
    hcols = [slice(hd * MEM_HEAD_DIM, (hd + 1) * MEM_HEAD_DIM) for hd in range(MEM_HEADS)]
    vcols = [slice(width + hd * MEM_HEAD_DIM, width + (hd + 1) * MEM_HEAD_DIM)
             for hd in range(MEM_HEADS)]
    s = [[_dot_nt(qn[:, c], kv_ref[:, c]) for c in hcols] for qn in q]
    o = []
    for sn in s:
        outs = []
        for hd in range(MEM_HEADS):
            m = jnp.max(sn[hd], axis=-1, keepdims=True)
            p = jnp.exp(sn[hd] - m)
            p = p / jnp.sum(p, axis=-1, keepdims=True)
            outs.append(_dot(p.astype(BF16), kv_ref[:, vcols[hd]]).astype(BF16))
        o.append(jnp.concatenate(outs, axis=-1))
    y2 = [_dot(on, wo_ref[...]) for on in o]
    for n, r in enumerate(groups):
        o_ref[r, :] = x[n] + _rms(y2[n], gmpost_ref[...])


def _mix_out_mem(x, o_fox_t, o_gla, kv, wout, gpost, gpre, wq, wo, gmpost, seq, mem_len):
    t = x.shape[0]
    tiles_per_seq = seq // TMO
    row = lambda w: pl.BlockSpec((TMO, w), lambda i: (i, 0))
    return pl.pallas_call(
        _mix_out_mem_kernel,
        grid=(t // TMO,),
        in_specs=[
            row(D_MODEL),
            pl.BlockSpec((FOX_WIDTH, TMO), lambda i: (0, i)),
            row(GLA_WIDTH),
            pl.BlockSpec((mem_len, kv.shape[1]), lambda i: (i // tiles_per_seq, 0)),
            _const_spec((D_MODEL, D_MODEL)), _const_spec((1, D_MODEL)), _const_spec((1, D_MODEL)),
            _const_spec((D_MODEL, D_MODEL)), _const_spec((D_MODEL, D_MODEL)),
            _const_spec((1, D_MODEL)),
        ],
        out_specs=row(D_MODEL),
        out_shape=jax.ShapeDtypeStruct((t, D_MODEL), F32),
        compiler_params=pltpu.CompilerParams(
            dimension_semantics=("arbitrary",), vmem_limit_bytes=VMEM_LIMIT),
        name="mix_out_mem",
    )(x, o_fox_t, o_gla, kv, wout, gpost, gpre, wq, wo, gmpost)


def _pad_cols(w, width):
    return jnp.pad(w, ((0, 0), (0, width - w.shape[1])))


def _aug_base(h):
    return FOX_HEAD_DIM if h % 2 == 0 else 0


def _placement():
    pk = np.zeros((LANES, FOX_WIDTH), np.float32)
    pq = np.zeros((FOX_HEADS * AUG_ROWS, LANES), np.float32)
    for h in range(FOX_HEADS):
        kbase = (h // 2) * LANES + _aug_base(h)
        for t in range(C_TERMS):
            src = t * FOX_HEADS + h
            pq[h * AUG_ROWS + t, src] = 1.0
            pk[ONE_LANE, kbase + t] = 1.0
            pq[h * AUG_ROWS + C_TERMS + t, ONE_LANE] = 1.0
            pk[src, kbase + C_TERMS + t] = -1.0
    return jnp.asarray(pk, BF16), jnp.asarray(pq, BF16)


def kernel(x, mem, g_ff1_pre, w_ff1_gate, w_ff1_up, w_ff1_down, g_ff1_post, g_mix_pre, w_mix_in, b_fox_f, w_gla_g2, b_gla_g, g_fox_out, g_gla_out, w_mix_out, g_mix_post, g_mem_pre, g_mem_src, w_mem_q, w_mem_kv, w_mem_o, g_mem_post, g_ff2_pre, w_ff2_gate, w_ff2_up, w_ff2_down, g_ff2_post, g_final):
    batch, seq, d = x.shape
    mem_len = mem.shape[1]
    depth = w_ff1_gate.shape[0]
    xt = x.reshape(batch * seq, d)
    memt = mem.reshape(batch * mem_len, d)
    gfin = g_final.reshape(1, d)
    vec = lambda v: v.reshape(1, -1).astype(F32)
    pk, pqt = _placement()
    for l in range(depth):
        last = l == depth - 1
        xt = _ffn(xt, vec(g_ff1_pre[l]), w_ff1_gate[l].astype(BF16), w_ff1_up[l].astype(BF16),
                  w_ff1_down[l].astype(BF16), vec(g_ff1_post[l]), gfin, False)

        w_in = w_mix_in[l]
        wqt = (w_in[:, OFF_FQ:OFF_FK] * (LOG2E * FOX_HEAD_DIM ** -0.5)).T.astype(BF16)
        wk = w_in[:, OFF_FK:OFF_FV].astype(BF16)
        wvt = w_in[:, OFF_FV:OFF_FF].T.astype(BF16)
        wgla = w_in[:, OFF_GQ:OFF_GG].astype(BF16)
        wgr = w_in[:, OFF_GR:IN_WIDTH].astype(BF16)
        wsm = _pad_cols(jnp.concatenate([w_in[:, OFF_FF:OFF_GQ], w_in[:, OFF_GG:OFF_GR]], axis=1),
                        LANES).astype(BF16)
        bf = _pad_cols(vec(b_fox_f[l]), LANES)
        wg2 = jnp.pad(w_gla_g2[l].astype(F32),
                      ((FOX_HEADS, LANES - FOX_HEADS - GLA_GATE_RANK), (0, 0)))
        k, qt, vt, gqk, gv, la, sr = _mix_in(xt, vec(g_mix_pre[l]), wk, wqt, wvt, wgla, wgr, wsm,
                                             bf, pk, pqt, wg2, vec(b_gla_g[l]), seq)

        g_fox = jnp.broadcast_to(g_fox_out[l].astype(F32)[:, None], (FOX_WIDTH, TQ))
        o_fox_t = _fox(qt, k, vt, g_fox, batch, seq)
        bsd = lambda a: a.reshape(batch, seq, a.shape[-1])
        o_gla = _gla(bsd(gqk), bsd(gv), bsd(la), bsd(sr), vec(g_gla_out[l]), batch, seq)
        o_gla = o_gla.reshape(batch * seq, GLA_WIDTH)

        kv = _mem_kv(memt, vec(g_mem_src[l]), w_mem_kv[l].astype(BF16), mem_len)
        xt = _mix_out_mem(xt, o_fox_t, o_gla, kv, w_mix_out[l].astype(BF16), vec(g_mix_post[l]),
                          vec(g_mem_pre[l]), w_mem_q[l].astype(BF16), w_mem_o[l].astype(BF16),
                          vec(g_mem_post[l]), seq, mem_len)

        xt = _ffn(xt, vec(g_ff2_pre[l]), w_ff2_gate[l].astype(BF16), w_ff2_up[l].astype(BF16),
                  w_ff2_down[l].astype(BF16), vec(g_ff2_post[l]), gfin, last)
    return xt.reshape(batch, seq, d)
```

```python
import functools

import numpy as np
import jax
import jax.numpy as jnp
from jax import lax
from jax.experimental import pallas as pl
from jax.experimental.pallas import tpu as pltpu

F32 = jnp.float32
BF16 = jnp.bfloat16

D_MODEL = 1024
CHUNK = 64
FOX_WIDTH = 512
FOX_HEADS = 8
FOX_HEAD_DIM = 64
GLA_WIDTH = 512
GLA_HEADS = 4
GLA_VAL_DIM = 128
GLA_KEY_DIM = 64
GLA_QK = 256
GLA_GATE_RANK = 16
GLA_GATE_NORM = 16.0
MEM_HEADS = 4
MEM_HEAD_DIM = 256
D_FF = 2816
RMS_EPS = 1e-6

OFF_FQ = 0
OFF_FK = OFF_FQ + FOX_WIDTH
OFF_FV = OFF_FK + FOX_WIDTH
OFF_FF = OFF_FV + FOX_WIDTH
OFF_GQ = OFF_FF + FOX_HEADS
OFF_GK = OFF_GQ + GLA_QK
OFF_GV = OFF_GK + GLA_QK
OFF_GG = OFF_GV + GLA_WIDTH
OFF_GR = OFF_GG + GLA_GATE_RANK
IN_WIDTH = OFF_GR + GLA_WIDTH

LANES = 128
VMEM_LIMIT = 56 * 1024 * 1024

TM = 512
TMF = 1024
TQ = 256
TG = 512
GLA_SEQS = 4
GLA_SUB = 256
SCORE_LEAD = 1
FFN_GROUPS = 4
TMO = 1024
MIX_GROUPS = 2

FOX_PAD = FOX_HEADS * LANES
C_TERMS = 3
ONE_LANE = C_TERMS * FOX_HEADS
AUG_ROWS = 16
V_ONE_ROW = FOX_HEAD_DIM
V_ROWS = FOX_HEAD_DIM + AUG_ROWS
LOG2E = 1.4426950408889634


def _rms(x, g):
    return x * lax.rsqrt(jnp.mean(x * x, axis=-1, keepdims=True) + RMS_EPS) * g


def _dot(a, b):
    return jnp.dot(a, b, preferred_element_type=F32)


def _dot_nt(a, b):
    return lax.dot_general(a, b, (((1,), (1,)), ((), ())), preferred_element_type=F32)


def _dot_tn(a, b):
    return lax.dot_general(a, b, (((0,), (0,)), ((), ())), preferred_element_type=F32)


def _split3(x):
    hi = x.astype(BF16)
    r1 = x - hi.astype(F32)
    mid = r1.astype(BF16)
    lo = (r1 - mid.astype(F32)).astype(BF16)
    return hi, mid, lo


def _tri_cumsum(tri_bf16, x):
    hi, mid, lo = _split3(x)
    return _dot(tri_bf16, hi) + _dot(tri_bf16, mid) + _dot(tri_bf16, lo)


def _pack_terms(x, one_lane=None):
    hi, mid, lo = _split3(x)
    out = (hi.astype(F32) + pltpu.roll(mid.astype(F32), FOX_HEADS, 1)
           + pltpu.roll(lo.astype(F32), 2 * FOX_HEADS, 1))
    if one_lane is not None:
        lane = lax.broadcasted_iota(jnp.int32, (1, LANES), 1)
        out = out + jnp.where(lane == one_lane, 1.0, 0.0)
    return out.astype(BF16)


def _log_sigmoid(x):
    return jnp.minimum(x, 0.0) - jnp.log1p(jnp.exp(-jnp.abs(x)))


def _silu(x):
    return x * jax.nn.sigmoid(x)


def _const_spec(shape):
    return pl.BlockSpec(shape, lambda *_: (0,) * len(shape), pipeline_mode=pl.Buffered(1))


def _ffn_kernel(x_ref, gpre_ref, wg_ref, wu_ref, wd_ref, gpost_ref, gfin_ref, o_ref, *, final):
    groups = [slice(n * TMF // FFN_GROUPS, (n + 1) * TMF // FFN_GROUPS) for n in range(FFN_GROUPS)]
    x = [x_ref[r, :] for r in groups]
    h = [_rms(xn, gpre_ref[...]).astype(BF16) for xn in x]
    a = [(_silu(_dot(hn, wg_ref[...])) * _dot(hn, wu_ref[...])).astype(BF16) for hn in h]
    y = [_dot(an, wd_ref[...]) for an in a]
    for n, r in enumerate(groups):
        out = x[n] + 0.5 * _rms(y[n], gpost_ref[...])
        if final:
            out = _rms(out, gfin_ref[...])
        o_ref[r, :] = out


def _ffn(x, gpre, wg, wu, wd, gpost, gfin, final):
    t = x.shape[0]
    return pl.pallas_call(
        functools.partial(_ffn_kernel, final=final),
        grid=(t // TMF,),
        in_specs=[
            pl.BlockSpec((TMF, D_MODEL), lambda i: (i, 0)),
            _const_spec((1, D_MODEL)),
            _const_spec((D_MODEL, D_FF)),
            _const_spec((D_MODEL, D_FF)),
            _const_spec((D_FF, D_MODEL)),
            _const_spec((1, D_MODEL)),
            _const_spec((1, D_MODEL)),
        ],
        out_specs=pl.BlockSpec((TMF, D_MODEL), lambda i: (i, 0)),
        out_shape=jax.ShapeDtypeStruct((t, D_MODEL), F32),
        compiler_params=pltpu.CompilerParams(
            dimension_semantics=("arbitrary",), vmem_limit_bytes=VMEM_LIMIT),
        name="ffn_final" if final else "ffn",
    )(x, gpre, wg, wu, wd, gpost, gfin)


def _mix_in_kernel(x_ref, g_ref, wk_ref, wqt_ref, wvt_ref, wgla_ref, wgr_ref, wsm_ref,
                   bf_ref, pk_ref, pqt_ref, wg2_ref, bg_ref,
                   k_ref, qt_ref, vt_ref, gqk_ref, gv_ref, la_ref, sr_ref,
                   carry_ref, *, tiles_per_seq):
    i = pl.program_id(0)

    @pl.when(i % tiles_per_seq == 0)
    def _():
        carry_ref[...] = jnp.zeros_like(carry_ref)

    h = _rms(x_ref[...], g_ref[...]).astype(BF16)
    gl = _dot(h, wgla_ref[...])
    gqk_ref[...] = gl[:, :2 * GLA_QK].astype(BF16)
    gv_ref[...] = gl[:, 2 * GLA_QK:].astype(BF16)
    sr_ref[...] = _silu(_dot(h, wgr_ref[...])).astype(BF16)

    sm = _dot(h, wsm_ref[...])

    lane = lax.broadcasted_iota(jnp.int32, (1, LANES), 1)
    lf = jnp.where(lane < FOX_HEADS, _log_sigmoid(sm + bf_ref[...]), 0.0)
    row = lax.broadcasted_iota(jnp.int32, (TM, TM), 0)
    col = lax.broadcasted_iota(jnp.int32, (TM, TM), 1)
    tri = jnp.where(col <= row, 1.0, 0.0).astype(BF16)
    c3 = _dot(tri, _pack_terms(lf))
    c = jnp.where(lane < FOX_HEADS,
                  c3 + pltpu.roll(c3, LANES - FOX_HEADS, 1) + pltpu.roll(c3, LANES - 2 * FOX_HEADS, 1),
                  0.0) + carry_ref[...]
    carry_ref[...] = c[TM - 1:TM, :]
    packed = _pack_terms(c * LOG2E, one_lane=ONE_LANE)

    kp = _dot(h, wk_ref[...])
    kaug = _dot(packed, pk_ref[...])
    low = lane < FOX_HEAD_DIM
    for hd in range(FOX_HEADS):
        pair = slice((hd // 2) * LANES, (hd // 2 + 1) * LANES)
        slot = slice(hd * LANES, (hd + 1) * LANES)
        own = low if hd % 2 == 0 else jnp.logical_not(low)
        k_ref[:, slot] = jnp.where(own, kp[:, pair], kaug[:, pair]).astype(BF16)

    qt = _dot_nt(wqt_ref[...], h).astype(BF16)
    aug = _dot_nt(pqt_ref[...], packed).astype(BF16)
    vt = _dot_nt(wvt_ref[...], h).astype(BF16)
    one_rows = jnp.where(lax.broadcasted_iota(jnp.int32, (AUG_ROWS, TQ), 0) == 0,
                         1.0, 0.0).astype(BF16)
    zeros = jnp.zeros((LANES - FOX_HEAD_DIM - AUG_ROWS, TQ), BF16)
    for n in range(TM // TQ):
        cs = slice(n * TQ, (n + 1) * TQ)
        for hd in range(FOX_HEADS):
            q_h = qt[hd * FOX_HEAD_DIM:(hd + 1) * FOX_HEAD_DIM, cs]
            a_h = aug[hd * AUG_ROWS:(hd + 1) * AUG_ROWS, cs]
            base = hd * LANES
            pieces = (q_h, a_h, zeros) if hd % 2 == 0 else (a_h, zeros, q_h)
            r = base
            for piece in pieces:
                qt_ref[n, r:r + piece.shape[0], :] = piece
                r += piece.shape[0]
            vb = hd * V_ROWS
            vt_ref[n, vb:vb + FOX_HEAD_DIM, :] = vt[hd * FOX_HEAD_DIM:(hd + 1) * FOX_HEAD_DIM, cs]
            vt_ref[n, vb + FOX_HEAD_DIM:vb + V_ROWS, :] = one_rows

    g_hi, g_mid, _ = _split3(sm)
    w_hi, w_mid, _ = _split3(wg2_ref[...])
    gate = _dot(g_hi, w_hi) + _dot(g_mid, w_hi) + _dot(g_hi, w_mid) + bg_ref[...]
    la_ref[...] = _log_sigmoid(gate) / GLA_GATE_NORM


def _mix_in(x, g, wk, wqt, wvt, wgla, wgr, wsm, bf, pk, pqt, wg2, bg, seq):
    t = x.shape[0]
    row = lambda w: pl.BlockSpec((TM, w), lambda i: (i, 0))
    tblk = lambda rows: pl.BlockSpec((TM // TQ, rows, TQ), lambda i: (i, 0, 0))
    return pl.pallas_call(
        functools.partial(_mix_in_kernel, tiles_per_seq=seq // TM),
        grid=(t // TM,),
        in_specs=[
            row(D_MODEL),
            _const_spec((1, D_MODEL)),
            _const_spec((D_MODEL, FOX_WIDTH)),
            _const_spec((FOX_WIDTH, D_MODEL)),
            _const_spec((FOX_WIDTH, D_MODEL)),
            _const_spec((D_MODEL, 2 * GLA_QK + GLA_WIDTH)),
            _const_spec((D_MODEL, GLA_WIDTH)),
            _const_spec((D_MODEL, LANES)),
            _const_spec((1, LANES)),
            _const_spec((LANES, FOX_WIDTH)),
            _const_spec((FOX_HEADS * AUG_ROWS, LANES)),
            _const_spec((LANES, GLA_QK)),
            _const_spec((1, GLA_QK)),
        ],
        out_specs=[row(FOX_PAD), tblk(FOX_PAD), tblk(FOX_HEADS * V_ROWS), row(2 * GLA_QK),
                   row(GLA_WIDTH), row(GLA_QK), row(GLA_WIDTH)],
        out_shape=[
            jax.ShapeDtypeStruct((t, FOX_PAD), BF16),
            jax.ShapeDtypeStruct((t // TQ, FOX_PAD, TQ), BF16),
            jax.ShapeDtypeStruct((t // TQ, FOX_HEADS * V_ROWS, TQ), BF16),
            jax.ShapeDtypeStruct((t, 2 * GLA_QK), BF16),
            jax.ShapeDtypeStruct((t, GLA_WIDTH), BF16),
            jax.ShapeDtypeStruct((t, GLA_QK), F32),
            jax.ShapeDtypeStruct((t, GLA_WIDTH), BF16),
        ],
        scratch_shapes=[pltpu.VMEM((1, LANES), F32)],
        compiler_params=pltpu.CompilerParams(
            dimension_semantics=("arbitrary",), vmem_limit_bytes=VMEM_LIMIT),
        name="mix_in",
    )(x, g, wk, wqt, wvt, wgla, wgr, wsm, bf, pk, pqt, wg2, bg)


def _fox_kernel(qt_ref, k_ref, vt_ref, g_ref, o_ref, m_ref, acc_ref, sa_ref, sb_ref):
    i = pl.program_id(1)
    m_ref[...] = jnp.full(m_ref.shape, -jnp.inf, F32)
    acc_ref[...] = jnp.zeros(acc_ref.shape, F32)
    key = lax.broadcasted_iota(jnp.int32, (TQ, TQ), 0)
    qry = lax.broadcasted_iota(jnp.int32, (TQ, TQ), 1)
    causal = key <= qry
    slots = [slice(h * LANES, (h + 1) * LANES) for h in range(FOX_HEADS)]
    vrows = [slice(h * V_ROWS, (h + 1) * V_ROWS) for h in range(FOX_HEADS)]
    both = (0, 1)

    def score(j, s_ref, g, h):
        start = pl.multiple_of(j * TQ, TQ)
        s_ref[g, h] = _dot(k_ref[pl.ds(start, TQ), slots[h]], qt_ref[g, slots[h], :])

    def accumulate(j, s_ref, g, h, diagonal):
        s = s_ref[g, h]
        if diagonal:
            s = jnp.where(causal, s, -jnp.inf)
        m_old = m_ref[g, h:h + 1, :]
        m_new = jnp.maximum(m_old, jnp.max(s, axis=0, keepdims=True))
        alpha = jnp.exp2(m_old - m_new)
        p = jnp.exp2(s - m_new).astype(BF16)
        m_ref[g, h:h + 1, :] = m_new
        acc_ref[g, h] = alpha * acc_ref[g, h] + _dot(vt_ref[j, vrows[h], :], p)

    def stage(cur, nxt):
        todo = [(g, h) for g in nxt[2] for h in range(FOX_HEADS)] if nxt else []
        units = [(g, h) for g in cur[2] for h in range(FOX_HEADS)] if cur else []
        for g, h in todo[:SCORE_LEAD]:
            score(nxt[0], nxt[1], g, h)
        todo = todo[SCORE_LEAD:]
        per_unit = -(-len(todo) // len(units)) if units else 0
        for g, h in units:
            accumulate(cur[0], cur[1], g, h, g == cur[3])
            for g2, h2 in todo[:per_unit]:
                score(nxt[0], nxt[1], g2, h2)
            todo = todo[per_unit:]
        for g, h in todo:
            score(nxt[0], nxt[1], g, h)

    stage(None, (0, sa_ref, both))

    def pair(t, carry):
        j = 2 * t
        stage((j, sa_ref, both, None), (j + 1, sb_ref, both))
        stage((j + 1, sb_ref, both, None), (j + 2, sa_ref, both))
        return carry

    def quad(t, carry):
        pair(2 * t, carry)
        return pair(2 * t + 1, carry)

    lax.fori_loop(0, i // 2, quad, 0)

    @pl.when(i % 2 == 1)
    def _():
        pair(i - 1, 0)

    stage((2 * i, sa_ref, both, 0), (2 * i + 1, sb_ref, (1,)))
    stage((2 * i + 1, sb_ref, (1,), 1), None)

    for g in both:
        for h in range(FOX_HEADS):
            acc = acc_ref[g, h]
            o = acc[:FOX_HEAD_DIM, :] / acc[V_ONE_ROW:V_ONE_ROW + 1, :]
            ms = jnp.mean(o * o, axis=0, keepdims=True)
            rows = slice(h * FOX_HEAD_DIM, (h + 1) * FOX_HEAD_DIM)
            o_ref[rows, g * TQ:(g + 1) * TQ] = (o * lax.rsqrt(ms + RMS_EPS)
                                                * g_ref[rows, :]).astype(BF16)


def _fox(qt, k, vt, g, batch, seq):
    nq = seq // TQ
    steps = nq // 2
    return pl.pallas_call(
        _fox_kernel,
        grid=(batch, steps),
        in_specs=[
            pl.BlockSpec((2, FOX_PAD, TQ), lambda b, i: (b * steps + i, 0, 0)),
            pl.BlockSpec((seq, FOX_PAD), lambda b, i: (b, 0)),
            pl.BlockSpec((nq, FOX_HEADS * V_ROWS, TQ), lambda b, i: (b, 0, 0)),
            _const_spec((FOX_WIDTH, TQ)),
        ],
        out_specs=pl.BlockSpec((FOX_WIDTH, 2 * TQ), lambda b, i: (0, b * steps + i)),
        out_shape=jax.ShapeDtypeStruct((FOX_WIDTH, batch * seq), BF16),
        scratch_shapes=[pltpu.VMEM((2, FOX_HEADS, TQ), F32),
                        pltpu.VMEM((2, FOX_HEADS, V_ROWS, TQ), F32),
                        pltpu.VMEM((2, FOX_HEADS, TQ, TQ), F32),
                        pltpu.VMEM((2, FOX_HEADS, TQ, TQ), F32)],
        compiler_params=pltpu.CompilerParams(
            dimension_semantics=("arbitrary", "arbitrary"), vmem_limit_bytes=VMEM_LIMIT),
        name="fox",
    )(qt, k, vt, g)


def _gla_kernel(qk_ref, v_ref, la_ref, sr_ref, g_ref, o_ref, st_ref):
    @pl.when(pl.program_id(1) == 0)
    def _():
        st_ref[...] = jnp.zeros_like(st_ref)

    nchunk = TG // CHUNK
    seqs = range(GLA_SEQS)
    heads = range(GLA_HEADS)
    lane = lax.broadcasted_iota(jnp.int32, (1, LANES), 1)
    first = lane < GLA_KEY_DIM
    subs = [slice(u * GLA_SUB, (u + 1) * GLA_SUB) for u in range(TG // GLA_SUB)]
    row = lax.broadcasted_iota(jnp.int32, (GLA_SUB, GLA_SUB), 0)
    col = lax.broadcasted_iota(jnp.int32, (GLA_SUB, GLA_SUB), 1)
    causal = (row - col).astype(jnp.uint32) <= (row & (CHUNK - 1)).astype(jnp.uint32)
    tri = jnp.where(causal, 1.0, 0.0).astype(BF16)
    scale = GLA_KEY_DIM ** -0.5
    zero = jnp.zeros((TG, LANES), BF16)
    chunks = [slice(c * CHUNK, (c + 1) * CHUNK) for c in range(nchunk)]
    prs = [slice((h // 2) * LANES, (h // 2 + 1) * LANES) for h in heads]
    cols = [slice(h * GLA_VAL_DIM, (h + 1) * GLA_VAL_DIM) for h in heads]
    hmasks = [first if h % 2 == 0 else jnp.logical_not(first) for h in heads]

    srow = lax.broadcasted_iota(jnp.int32, (LANES, 2 * GLA_VAL_DIM), 0) // GLA_KEY_DIM
    scol = lax.broadcasted_iota(jnp.int32, (LANES, 2 * GLA_VAL_DIM), 1) // GLA_VAL_DIM
    own_block = srow == scol
    pairs = range(GLA_HEADS // 2)
    pcols = [slice(p * 2 * GLA_VAL_DIM, (p + 1) * 2 * GLA_VAL_DIM) for p in pairs]
    pk = [slice(p * LANES, (p + 1) * LANES) for p in pairs]

    b = [jnp.concatenate([_tri_cumsum(tri, la_ref[n, u, :]) for u in subs], axis=0)
         for n in seqs]
    qp, qd, kr, k_inv, decay_t = [], [], [], [], []
    for n in seqs:
        b_last3 = b[n].reshape(nchunk, CHUNK, GLA_QK)[:, CHUNK - 1:CHUNK, :]
        b_last = jnp.broadcast_to(b_last3, (nchunk, CHUNK, GLA_QK)).reshape(TG, GLA_QK)
        q = qk_ref[n, :, 0:GLA_QK].astype(F32)
        k = qk_ref[n, :, GLA_QK:2 * GLA_QK].astype(F32)
        q_dec = (q * scale * jnp.exp(b[n])).astype(BF16)
        kr.append((k * jnp.exp(b_last - b[n])).astype(BF16))
        k_inv.append((k * jnp.exp(-b[n])).astype(BF16))
        decay = jnp.exp(b_last3).reshape(nchunk, GLA_QK)
        decay_t.append(jnp.concatenate(
            [decay, jnp.zeros((LANES - nchunk, GLA_QK), F32)], axis=0).T)
        qp.append(q_dec)
        qd.append([jnp.where(hmasks[h], q_dec[:, prs[h]], zero) for h in heads])
    kv, att = [], []
    for n in seqs:
        kv.append([[_dot_tn(kr[n][r, pk[p]], v_ref[n, r, pcols[p]]) for r in chunks]
                   for p in pairs])
        att.append([[jnp.where(causal, _dot_nt(qd[n][h][u], k_inv[n][u, prs[h]]), 0.0).astype(BF16)
                     for u in subs] for h in heads])
    states = []
    for n in seqs:
        per_pair = []
        for p in pairs:
            st = st_ref[n, p]
            per_chunk = []
            for c in range(nchunk):
                per_chunk.append(jnp.where(own_block, st, 0.0).astype(BF16))
                st = st * decay_t[n][pk[p], c:c + 1] + kv[n][p][c]
            st_ref[n, p] = st
            per_pair.append(per_chunk)
        states.append(per_pair)
    for n in seqs:
        inter = [jnp.concatenate([_dot(qp[n][chunks[c], pk[p]], states[n][p][c])
                                  for c in range(nchunk)], axis=0) for p in pairs]
        for h in heads:
            half = slice((h % 2) * GLA_VAL_DIM, (h % 2 + 1) * GLA_VAL_DIM)
            intra = jnp.concatenate([_dot(att[n][h][ui], v_ref[n, u, cols[h]])
                                     for ui, u in enumerate(subs)], axis=0)
            o = intra + inter[h // 2][:, half]
            ms = jnp.mean(o * o, axis=-1, keepdims=True)
            y = o * lax.rsqrt(ms + RMS_EPS) * g_ref[:, cols[h]] * sr_ref[n, :, cols[h]].astype(F32)
            o_ref[n, :, cols[h]] = y.astype(BF16)


def _gla(gqk, gv, la, sr, g, batch, seq):
    blk = lambda w: pl.BlockSpec((GLA_SEQS, TG, w), lambda b, i: (b, i, 0))
    return pl.pallas_call(
        _gla_kernel,
        grid=(batch // GLA_SEQS, seq // TG),
        in_specs=[blk(2 * GLA_QK), blk(GLA_WIDTH), blk(GLA_QK), blk(GLA_WIDTH),
                  pl.BlockSpec((1, GLA_WIDTH), lambda b, i: (0, 0))],
        out_specs=blk(GLA_WIDTH),
        out_shape=jax.ShapeDtypeStruct((batch, seq, GLA_WIDTH), BF16),
        scratch_shapes=[pltpu.VMEM((GLA_SEQS, GLA_HEADS // 2, LANES, 2 * GLA_VAL_DIM), F32)],
        compiler_params=pltpu.CompilerParams(
            dimension_semantics=("arbitrary", "arbitrary"), vmem_limit_bytes=VMEM_LIMIT),
        name="gla",
    )(gqk, gv, la, sr, g)


def _mem_kv_kernel(m_ref, g_ref, w_ref, o_ref):
    h = _rms(m_ref[...], g_ref[...]).astype(BF16)
    o_ref[...] = _dot(h, w_ref[...]).astype(BF16)


def _mem_kv(mem, g, w, mem_len):
    t = mem.shape[0]
    width = w.shape[1]
    return pl.pallas_call(
        _mem_kv_kernel,
        grid=(t // mem_len,),
        in_specs=[pl.BlockSpec((mem_len, D_MODEL), lambda i: (i, 0)),
                  _const_spec((1, D_MODEL)), _const_spec((D_MODEL, width))],
        out_specs=pl.BlockSpec((mem_len, width), lambda i: (i, 0)),
        out_shape=jax.ShapeDtypeStruct((t, width), BF16),
        compiler_params=pltpu.CompilerParams(
            dimension_semantics=("arbitrary",), vmem_limit_bytes=VMEM_LIMIT),
        name="mem_kv",
    )(mem, g, w)


def _mix_out_mem_kernel(x_ref, oft_ref, og_ref, kv_ref, wout_ref, gpost_ref, gpre_ref,
                        wq_ref, wo_ref, gmpost_ref, o_ref):
    width = MEM_HEADS * MEM_HEAD_DIM
    groups = [slice(n * TMO // MIX_GROUPS, (n + 1) * TMO // MIX_GROUPS) for n in range(MIX_GROUPS)]
    y = [_dot_tn(oft_ref[:, r], wout_ref[0:FOX_WIDTH, :]) + _dot(og_ref[r, :], wout_ref[FOX_WIDTH:, :])
         for r in groups]
    x = [x_ref[r, :] + _rms(y[n], gpost_ref[...]) for n, r in enumerate(groups)]
    h = [_rms(xn, gpre_ref[...]).astype(BF16) for xn in x]
    q = [(_dot(hn, wq_ref[...]) * (MEM_HEAD_DIM ** -0.5)).astype(BF16) for hn in h]
    hcols = [slice(hd * MEM_HEAD_DIM, (hd + 1) * MEM_HEAD_DIM) for hd in range(MEM_HEADS)]
    vcols = [slice(width + hd * MEM_HEAD_DIM, width + (hd + 1) * MEM_HEAD_DIM)
             for hd in range(MEM_HEADS)]
    s = [[_dot_nt(qn[:, c], kv_ref[:, c]) for c in hcols] for qn in q]
    o = []
    for sn in s:
        outs = []
        for hd in range(MEM_HEADS):
            m = jnp.max(sn[hd], axis=-1, keepdims=True)
            p = jnp.exp(sn[hd] - m)
            p = p / jnp.sum(p, axis=-1, keepdims=True)
            outs.append(_dot(p.astype(BF16), kv_ref[:, vcols[hd]]).astype(BF16))
        o.append(jnp.concatenate(outs, axis=-1))
    y2 = [_dot(on, wo_ref[...]) for on in o]
    for n, r in enumerate(groups):
        o_ref[r, :] = x[n] + _rms(y2[n], gmpost_ref[...])


def _mix_out_mem(x, o_fox_t, o_gla, kv, wout, gpost, gpre, wq, wo, gmpost, seq, mem_len):
    t = x.shape[0]
    tiles_per_seq = seq // TMO
    row = lambda w: pl.BlockSpec((TMO, w), lambda i: (i, 0))
    return pl.pallas_call(
        _mix_out_mem_kernel,
        grid=(t // TMO,),
        in_specs=[
            row(D_MODEL),
            pl.BlockSpec((FOX_WIDTH, TMO), lambda i: (0, i)),
            row(GLA_WIDTH),
            pl.BlockSpec((mem_len, kv.shape[1]), lambda i: (i // tiles_per_seq, 0)),
            _const_spec((D_MODEL, D_MODEL)), _const_spec((1, D_MODEL)), _const_spec((1, D_MODEL)),
            _const_spec((D_MODEL, D_MODEL)), _const_spec((D_MODEL, D_MODEL)),
            _const_spec((1, D_MODEL)),
        ],
        out_specs=row(D_MODEL),
        out_shape=jax.ShapeDtypeStruct((t, D_MODEL), F32),
        compiler_params=pltpu.CompilerParams(
            dimension_semantics=("arbitrary",), vmem_limit_bytes=VMEM_LIMIT),
        name="mix_out_mem",
    )(x, o_fox_t, o_gla, kv, wout, gpost, gpre, wq, wo, gmpost)


def _pad_cols(w, width):
    return jnp.pad(w, ((0, 0), (0, width - w.shape[1])))


def _aug_base(h):
    return FOX_HEAD_DIM if h % 2 == 0 else 0


def _placement():
    pk = np.zeros((LANES, FOX_WIDTH), np.float32)
    pq = np.zeros((FOX_HEADS * AUG_ROWS, LANES), np.float32)
    for h in range(FOX_HEADS):
        kbase = (h // 2) * LANES + _aug_base(h)
        for t in range(C_TERMS):
            src = t * FOX_HEADS + h
            pq[h * AUG_ROWS + t, src] = 1.0
            pk[ONE_LANE, kbase + t] = 1.0
            pq[h * AUG_ROWS + C_TERMS + t, ONE_LANE] = 1.0
            pk[src, kbase + C_TERMS + t] = -1.0
    return jnp.asarray(pk, BF16), jnp.asarray(pq, BF16)


def kernel(x, mem, g_ff1_pre, w_ff1_gate, w_ff1_up, w_ff1_down, g_ff1_post, g_mix_pre, w_mix_in, b_fox_f, w_gla_g2, b_gla_g, g_fox_out, g_gla_out, w_mix_out, g_mix_post, g_mem_pre, g_mem_src, w_mem_q, w_mem_kv, w_mem_o, g_mem_post, g_ff2_pre, w_ff2_gate, w_ff2_up, w_ff2_down, g_ff2_post, g_final):
    batch, seq, d = x.shape
    mem_len = mem.shape[1]
    depth = w_ff1_gate.shape[0]
    xt = x.reshape(batch * seq, d)
    memt = mem.reshape(batch * mem_len, d)
    gfin = g_final.reshape(1, d)
    vec = lambda v: v.reshape(1, -1).astype(F32)
    pk, pqt = _placement()
    for l in range(depth):
        last = l == depth - 1
        xt = _ffn(xt, vec(g_ff1_pre[l]), w_ff1_gate[l].astype(BF16), w_ff1_up[l].astype(BF16),
                  w_ff1_down[l].astype(BF16), vec(g_ff1_post[l]), gfin, False)

        w_in = w_mix_in[l]
        wqt = (w_in[:, OFF_FQ:OFF_FK] * (LOG2E * FOX_HEAD_DIM ** -0.5)).T.astype(BF16)
        wk = w_in[:, OFF_FK:OFF_FV].astype(BF16)
        wvt = w_in[:, OFF_FV:OFF_FF].T.astype(BF16)
        wgla = w_in[:, OFF_GQ:OFF_GG].astype(BF16)
        wgr = w_in[:, OFF_GR:IN_WIDTH].astype(BF16)
        wsm = _pad_cols(jnp.concatenate([w_in[:, OFF_FF:OFF_GQ], w_in[:, OFF_GG:OFF_GR]], axis=1),
                        LANES).astype(BF16)
        bf = _pad_cols(vec(b_fox_f[l]), LANES)
        wg2 = jnp.pad(w_gla_g2[l].astype(F32),
                      ((FOX_HEADS, LANES - FOX_HEADS - GLA_GATE_RANK), (0, 0)))
        k, qt, vt, gqk, gv, la, sr = _mix_in(xt, vec(g_mix_pre[l]), wk, wqt, wvt, wgla, wgr, wsm,
                                             bf, pk, pqt, wg2, vec(b_gla_g[l]), seq)

        g_fox = jnp.broadcast_to(g_fox_out[l].astype(F32)[:, None], (FOX_WIDTH, TQ))
        o_fox_t = _fox(qt, k, vt, g_fox, batch, seq)
        bsd = lambda a: a.reshape(batch, seq, a.shape[-1])
        o_gla = _gla(bsd(gqk), bsd(gv), bsd(la), bsd(sr), vec(g_gla_out[l]), batch, seq)
        o_gla = o_gla.reshape(batch * seq, GLA_WIDTH)

        kv = _mem_kv(memt, vec(g_mem_src[l]), w_mem_kv[l].astype(BF16), mem_len)
        xt = _mix_out_mem(xt, o_fox_t, o_gla, kv, w_mix_out[l].astype(BF16), vec(g_mix_post[l]),
                          vec(g_mem_pre[l]), w_mem_q[l].astype(BF16), w_mem_o[l].astype(BF16),
                          vec(g_mem_post[l]), seq, mem_len)

        xt = _ffn(xt, vec(g_ff2_pre[l]), w_ff2_gate[l].astype(BF16), w_ff2_up[l].astype(BF16),
                  w_ff2_down[l].astype(BF16), vec(g_ff2_post[l]), gfin, last)
    return xt.reshape(batch, seq, d)
```

```python
import functools

import numpy as np
import jax
import jax.numpy as jnp
from jax import lax
from jax.experimental import pallas as pl
from jax.experimental.pallas import tpu as pltpu

F32 = jnp.float32
BF16 = jnp.bfloat16

D_MODEL = 1024
CHUNK = 64
FOX_WIDTH = 512
FOX_HEADS = 8
FOX_HEAD_DIM = 64
GLA_WIDTH = 512
GLA_HEADS = 4
GLA_VAL_DIM = 128
GLA_KEY_DIM = 64
GLA_QK = 256
GLA_GATE_RANK = 16
GLA_GATE_NORM = 16.0
MEM_HEADS = 4
MEM_HEAD_DIM = 256
D_FF = 2816
RMS_EPS = 1e-6

OFF_FQ = 0
OFF_FK = OFF_FQ + FOX_WIDTH
OFF_FV = OFF_FK + FOX_WIDTH
OFF_FF = OFF_FV + FOX_WIDTH
OFF_GQ = OFF_FF + FOX_HEADS
OFF_GK = OFF_GQ + GLA_QK
OFF_GV = OFF_GK + GLA_QK
OFF_GG = OFF_GV + GLA_WIDTH
OFF_GR = OFF_GG + GLA_GATE_RANK
IN_WIDTH = OFF_GR + GLA_WIDTH

LANES = 128
VMEM_LIMIT = 56 * 1024 * 1024

TM = 512
TMF = 1024
TQ = 256
TG = 512
GLA_SEQS = 4
GLA_SUB = 256
SCORE_LEAD = 1
FFN_GROUPS = 8
TMO = 1024
MIX_GROUPS = 4

FOX_PAD = FOX_HEADS * LANES
C_TERMS = 3
ONE_LANE = C_TERMS * FOX_HEADS
AUG_ROWS = 16
V_ONE_ROW = FOX_HEAD_DIM
V_ROWS = FOX_HEAD_DIM + AUG_ROWS
LOG2E = 1.4426950408889634


def _rms(x, g):
    return x * lax.rsqrt(jnp.mean(x * x, axis=-1, keepdims=True) + RMS_EPS) * g


def _dot(a, b):
    return jnp.dot(a, b, preferred_element_type=F32)


def _dot_nt(a, b):
    return lax.dot_general(a, b, (((1,), (1,)), ((), ())), preferred_element_type=F32)


def _dot_tn(a, b):
    return lax.dot_general(a, b, (((0,), (0,)), ((), ())), preferred_element_type=F32)


def _split3(x):
    hi = x.astype(BF16)
    r1 = x - hi.astype(F32)
    mid = r1.astype(BF16)
    lo = (r1 - mid.astype(F32)).astype(BF16)
    return hi, mid, lo


def _tri_cumsum(tri_bf16, x):
    hi, mid, lo = _split3(x)
    return _dot(tri_bf16, hi) + _dot(tri_bf16, mid) + _dot(tri_bf16, lo)


def _pack_terms(x, one_lane=None):
    hi, mid, lo = _split3(x)
    out = (hi.astype(F32) + pltpu.roll(mid.astype(F32), FOX_HEADS, 1)
           + pltpu.roll(lo.astype(F32), 2 * FOX_HEADS, 1))
    if one_lane is not None:
        lane = lax.broadcasted_iota(jnp.int32, (1, LANES), 1)
        out = out + jnp.where(lane == one_lane, 1.0, 0.0)
    return out.astype(BF16)


def _log_sigmoid(x):
    return jnp.minimum(x, 0.0) - jnp.log1p(jnp.exp(-jnp.abs(x)))


def _silu(x):
    return x * jax.nn.sigmoid(x)


def _const_spec(shape):
    return pl.BlockSpec(shape, lambda *_: (0,) * len(shape), pipeline_mode=pl.Buffered(1))


def _ffn_kernel(x_ref, gpre_ref, wg_ref, wu_ref, wd_ref, gpost_ref, gfin_ref, o_ref, *, final):
    groups = [slice(n * TMF // FFN_GROUPS, (n + 1) * TMF // FFN_GROUPS) for n in range(FFN_GROUPS)]
    x = [x_ref[r, :] for r in groups]
    h = [_rms(xn, gpre_ref[...]).astype(BF16) for xn in x]
    a = [(_silu(_dot(hn, wg_ref[...])) * _dot(hn, wu_ref[...])).astype(BF16) for hn in h]
    y = [_dot(an, wd_ref[...]) for an in a]
    for n, r in enumerate(groups):
        out = x[n] + 0.5 * _rms(y[n], gpost_ref[...])
        if final:
            out = _rms(out, gfin_ref[...])
        o_ref[r, :] = out


def _ffn(x, gpre, wg, wu, wd, gpost, gfin, final):
    t = x.shape[0]
    return pl.pallas_call(
        functools.partial(_ffn_kernel, final=final),
        grid=(t // TMF,),
        in_specs=[
            pl.BlockSpec((TMF, D_MODEL), lambda i: (i, 0)),
            _const_spec((1, D_MODEL)),
            _const_spec((D_MODEL, D_FF)),
            _const_spec((D_MODEL, D_FF)),
            _const_spec((D_FF, D_MODEL)),
            _const_spec((1, D_MODEL)),
            _const_spec((1, D_MODEL)),
        ],
        out_specs=pl.BlockSpec((TMF, D_MODEL), lambda i: (i, 0)),
        out_shape=jax.ShapeDtypeStruct((t, D_MODEL), F32),
        compiler_params=pltpu.CompilerParams(
            dimension_semantics=("arbitrary",), vmem_limit_bytes=VMEM_LIMIT),
        name="ffn_final" if final else "ffn",
    )(x, gpre, wg, wu, wd, gpost, gfin)


def _mix_in_kernel(x_ref, g_ref, wk_ref, wqt_ref, wvt_ref, wgla_ref, wgr_ref, wsm_ref,
                   bf_ref, pk_ref, pqt_ref, wg2_ref, bg_ref,
                   k_ref, qt_ref, vt_ref, gqk_ref, gv_ref, la_ref, sr_ref,
                   carry_ref, *, tiles_per_seq):
    i = pl.program_id(0)

    @pl.when(i % tiles_per_seq == 0)
    def _():
        carry_ref[...] = jnp.zeros_like(carry_ref)

    h = _rms(x_ref[...], g_ref[...]).astype(BF16)
    gl = _dot(h, wgla_ref[...])
    gqk_ref[...] = gl[:, :2 * GLA_QK].astype(BF16)
    gv_ref[...] = gl[:, 2 * GLA_QK:].astype(BF16)
    sr_ref[...] = _silu(_dot(h, wgr_ref[...])).astype(BF16)

    sm = _dot(h, wsm_ref[...])

    lane = lax.broadcasted_iota(jnp.int32, (1, LANES), 1)
    lf = jnp.where(lane < FOX_HEADS, _log_sigmoid(sm + bf_ref[...]), 0.0)
    row = lax.broadcasted_iota(jnp.int32, (TM, TM), 0)
    col = lax.broadcasted_iota(jnp.int32, (TM, TM), 1)
    tri = jnp.where(col <= row, 1.0, 0.0).astype(BF16)
    c3 = _dot(tri, _pack_terms(lf))
    c = jnp.where(lane < FOX_HEADS,
                  c3 + pltpu.roll(c3, LANES - FOX_HEADS, 1) + pltpu.roll(c3, LANES - 2 * FOX_HEADS, 1),
                  0.0) + carry_ref[...]
    carry_ref[...] = c[TM - 1:TM, :]
    packed = _pack_terms(c * LOG2E, one_lane=ONE_LANE)

    kp = _dot(h, wk_ref[...])
    kaug = _dot(packed, pk_ref[...])
    low = lane < FOX_HEAD_DIM
    for hd in range(FOX_HEADS):
        pair = slice((hd // 2) * LANES, (hd // 2 + 1) * LANES)
        slot = slice(hd * LANES, (hd + 1) * LANES)
        own = low if hd % 2 == 0 else jnp.logical_not(low)
        k_ref[:, slot] = jnp.where(own, kp[:, pair], kaug[:, pair]).astype(BF16)

    qt = _dot_nt(wqt_ref[...], h).astype(BF16)
    aug = _dot_nt(pqt_ref[...], packed).astype(BF16)
    vt = _dot_nt(wvt_ref[...], h).astype(BF16)
    one_rows = jnp.where(lax.broadcasted_iota(jnp.int32, (AUG_ROWS, TQ), 0) == 0,
                         1.0, 0.0).astype(BF16)
    zeros = jnp.zeros((LANES - FOX_HEAD_DIM - AUG_ROWS, TQ), BF16)
    for n in range(TM // TQ):
        cs = slice(n * TQ, (n + 1) * TQ)
        for hd in range(FOX_HEADS):
            q_h = qt[hd * FOX_HEAD_DIM:(hd + 1) * FOX_HEAD_DIM, cs]
            a_h = aug[hd * AUG_ROWS:(hd + 1) * AUG_ROWS, cs]
            base = hd * LANES
            pieces = (q_h, a_h, zeros) if hd % 2 == 0 else (a_h, zeros, q_h)
            r = base
            for piece in pieces:
                qt_ref[n, r:r + piece.shape[0], :] = piece
                r += piece.shape[0]
            vb = hd * V_ROWS
            vt_ref[n, vb:vb + FOX_HEAD_DIM, :] = vt[hd * FOX_HEAD_DIM:(hd + 1) * FOX_HEAD_DIM, cs]
            vt_ref[n, vb + FOX_HEAD_DIM:vb + V_ROWS, :] = one_rows

    g_hi, g_mid, _ = _split3(sm)
    w_hi, w_mid, _ = _split3(wg2_ref[...])
    gate = _dot(g_hi, w_hi) + _dot(g_mid, w_hi) + _dot(g_hi, w_mid) + bg_ref[...]
    la_ref[...] = _log_sigmoid(gate) / GLA_GATE_NORM


def _mix_in(x, g, wk, wqt, wvt, wgla, wgr, wsm, bf, pk, pqt, wg2, bg, seq):
    t = x.shape[0]
    row = lambda w: pl.BlockSpec((TM, w), lambda i: (i, 0))
    tblk = lambda rows: pl.BlockSpec((TM // TQ, rows, TQ), lambda i: (i, 0, 0))
    return pl.pallas_call(
        functools.partial(_mix_in_kernel, tiles_per_seq=seq // TM),
        grid=(t // TM,),
        in_specs=[
            row(D_MODEL),
            _const_spec((1, D_MODEL)),
            _const_spec((D_MODEL, FOX_WIDTH)),
            _const_spec((FOX_WIDTH, D_MODEL)),
            _const_spec((FOX_WIDTH, D_MODEL)),
            _const_spec((D_MODEL, 2 * GLA_QK + GLA_WIDTH)),
            _const_spec((D_MODEL, GLA_WIDTH)),
            _const_spec((D_MODEL, LANES)),
            _const_spec((1, LANES)),
            _const_spec((LANES, FOX_WIDTH)),
            _const_spec((FOX_HEADS * AUG_ROWS, LANES)),
            _const_spec((LANES, GLA_QK)),
            _const_spec((1, GLA_QK)),
        ],
        out_specs=[row(FOX_PAD), tblk(FOX_PAD), tblk(FOX_HEADS * V_ROWS), row(2 * GLA_QK),
                   row(GLA_WIDTH), row(GLA_QK), row(GLA_WIDTH)],
        out_shape=[
            jax.ShapeDtypeStruct((t, FOX_PAD), BF16),
            jax.ShapeDtypeStruct((t // TQ, FOX_PAD, TQ), BF16),
            jax.ShapeDtypeStruct((t // TQ, FOX_HEADS * V_ROWS, TQ), BF16),
            jax.ShapeDtypeStruct((t, 2 * GLA_QK), BF16),
            jax.ShapeDtypeStruct((t, GLA_WIDTH), BF16),
            jax.ShapeDtypeStruct((t, GLA_QK), F32),
            jax.ShapeDtypeStruct((t, GLA_WIDTH), BF16),
        ],
        scratch_shapes=[pltpu.VMEM((1, LANES), F32)],
        compiler_params=pltpu.CompilerParams(
            dimension_semantics=("arbitrary",), vmem_limit_bytes=VMEM_LIMIT),
        name="mix_in",
    )(x, g, wk, wqt, wvt, wgla, wgr, wsm, bf, pk, pqt, wg2, bg)


def _fox_kernel(qt_ref, k_ref, vt_ref, g_ref, o_ref, m_ref, acc_ref, sa_ref, sb_ref):
    i = pl.program_id(1)
    m_ref[...] = jnp.full(m_ref.shape, -jnp.inf, F32)
    acc_ref[...] = jnp.zeros(acc_ref.shape, F32)
    key = lax.broadcasted_iota(jnp.int32, (TQ, TQ), 0)
    qry = lax.broadcasted_iota(jnp.int32, (TQ, TQ), 1)
    causal = key <= qry
    slots = [slice(h * LANES, (h + 1) * LANES) for h in range(FOX_HEADS)]
    vrows = [slice(h * V_ROWS, (h + 1) * V_ROWS) for h in range(FOX_HEADS)]
    both = (0, 1)

    def score(j, s_ref, g, h):
        start = pl.multiple_of(j * TQ, TQ)
        s_ref[g, h] = _dot(k_ref[pl.ds(start, TQ), slots[h]], qt_ref[g, slots[h], :])

    def accumulate(j, s_ref, g, h, diagonal):
        s = s_ref[g, h]
        if diagonal:
            s = jnp.where(causal, s, -jnp.inf)
        m_old = m_ref[g, h:h + 1, :]
        m_new = jnp.maximum(m_old, jnp.max(s, axis=0, keepdims=True))
        alpha = jnp.exp2(m_old - m_new)
        p = jnp.exp2(s - m_new).astype(BF16)
        m_ref[g, h:h + 1, :] = m_new
        acc_ref[g, h] = alpha * acc_ref[g, h] + _dot(vt_ref[j, vrows[h], :], p)

    def stage(cur, nxt):
        todo = [(g, h) for g in nxt[2] for h in range(FOX_HEADS)] if nxt else []
        units = [(g, h) for g in cur[2] for h in range(FOX_HEADS)] if cur else []
        for g, h in todo[:SCORE_LEAD]:
            score(nxt[0], nxt[1], g, h)
        todo = todo[SCORE_LEAD:]
        per_unit = -(-len(todo) // len(units)) if units else 0
        for g, h in units:
            accumulate(cur[0], cur[1], g, h, g == cur[3])
            for g2, h2 in todo[:per_unit]:
                score(nxt[0], nxt[1], g2, h2)
            todo = todo[per_unit:]
        for g, h in todo:
            score(nxt[0], nxt[1], g, h)

    stage(None, (0, sa_ref, both))

    def pair(t, carry):
        j = 2 * t
        stage((j, sa_ref, both, None), (j + 1, sb_ref, both))
        stage((j + 1, sb_ref, both, None), (j + 2, sa_ref, both))
        return carry

    def quad(t, carry):
        pair(2 * t, carry)
        return pair(2 * t + 1, carry)

    lax.fori_loop(0, i // 2, quad, 0)

    @pl.when(i % 2 == 1)
    def _():
        pair(i - 1, 0)

    stage((2 * i, sa_ref, both, 0), (2 * i + 1, sb_ref, (1,)))
    stage((2 * i + 1, sb_ref, (1,), 1), None)

    for g in both:
        for h in range(FOX_HEADS):
            acc = acc_ref[g, h]
            o = acc[:FOX_HEAD_DIM, :] / acc[V_ONE_ROW:V_ONE_ROW + 1, :]
            ms = jnp.mean(o * o, axis=0, keepdims=True)
            rows = slice(h * FOX_HEAD_DIM, (h + 1) * FOX_HEAD_DIM)
            o_ref[rows, g * TQ:(g + 1) * TQ] = (o * lax.rsqrt(ms + RMS_EPS)
                                                * g_ref[rows, :]).astype(BF16)


def _fox(qt, k, vt, g, batch, seq):
    nq = seq // TQ
    steps = nq // 2
    return pl.pallas_call(
        _fox_kernel,
        grid=(batch, steps),
        in_specs=[
            pl.BlockSpec((2, FOX_PAD, TQ), lambda b, i: (b * steps + i, 0, 0)),
            pl.BlockSpec((seq, FOX_PAD), lambda b, i: (b, 0)),
            pl.BlockSpec((nq, FOX_HEADS * V_ROWS, TQ), lambda b, i: (b, 0, 0)),
            _const_spec((FOX_WIDTH, TQ)),
        ],
        out_specs=pl.BlockSpec((FOX_WIDTH, 2 * TQ), lambda b, i: (0, b * steps + i)),
        out_shape=jax.ShapeDtypeStruct((FOX_WIDTH, batch * seq), BF16),
        scratch_shapes=[pltpu.VMEM((2, FOX_HEADS, TQ), F32),
                        pltpu.VMEM((2, FOX_HEADS, V_ROWS, TQ), F32),
                        pltpu.VMEM((2, FOX_HEADS, TQ, TQ), F32),
                        pltpu.VMEM((2, FOX_HEADS, TQ, TQ), F32)],
        compiler_params=pltpu.CompilerParams(
            dimension_semantics=("arbitrary", "arbitrary"), vmem_limit_bytes=VMEM_LIMIT),
        name="fox",
    )(qt, k, vt, g)


def _gla_kernel(qk_ref, v_ref, la_ref, sr_ref, g_ref, o_ref, st_ref):
    @pl.when(pl.program_id(1) == 0)
    def _():
        st_ref[...] = jnp.zeros_like(st_ref)

    nchunk = TG // CHUNK
    seqs = range(GLA_SEQS)
    heads = range(GLA_HEADS)
    lane = lax.broadcasted_iota(jnp.int32, (1, LANES), 1)
    first = lane < GLA_KEY_DIM
    subs = [slice(u * GLA_SUB, (u + 1) * GLA_SUB) for u in range(TG // GLA_SUB)]
    row = lax.broadcasted_iota(jnp.int32, (GLA_SUB, GLA_SUB), 0)
    col = lax.broadcasted_iota(jnp.int32, (GLA_SUB, GLA_SUB), 1)
    causal = (row - col).astype(jnp.uint32) <= (row & (CHUNK - 1)).astype(jnp.uint32)
    tri = jnp.where(causal, 1.0, 0.0).astype(BF16)
    scale = GLA_KEY_DIM ** -0.5
    zero = jnp.zeros((TG, LANES), BF16)
    chunks = [slice(c * CHUNK, (c + 1) * CHUNK) for c in range(nchunk)]
    prs = [slice((h // 2) * LANES, (h // 2 + 1) * LANES) for h in heads]
    cols = [slice(h * GLA_VAL_DIM, (h + 1) * GLA_VAL_DIM) for h in heads]
    hmasks = [first if h % 2 == 0 else jnp.logical_not(first) for h in heads]

    srow = lax.broadcasted_iota(jnp.int32, (LANES, 2 * GLA_VAL_DIM), 0) // GLA_KEY_DIM
    scol = lax.broadcasted_iota(jnp.int32, (LANES, 2 * GLA_VAL_DIM), 1) // GLA_VAL_DIM
    own_block = srow == scol
    pairs = range(GLA_HEADS // 2)
    pcols = [slice(p * 2 * GLA_VAL_DIM, (p + 1) * 2 * GLA_VAL_DIM) for p in pairs]
    pk = [slice(p * LANES, (p + 1) * LANES) for p in pairs]

    b = [jnp.concatenate([_tri_cumsum(tri, la_ref[n, u, :]) for u in subs], axis=0)
         for n in seqs]
    qp, qd, kr, k_inv, decay_t = [], [], [], [], []
    for n in seqs:
        b_last3 = b[n].reshape(nchunk, CHUNK, GLA_QK)[:, CHUNK - 1:CHUNK, :]
        b_last = jnp.broadcast_to(b_last3, (nchunk, CHUNK, GLA_QK)).reshape(TG, GLA_QK)
        q = qk_ref[n, :, 0:GLA_QK].astype(F32)
        k = qk_ref[n, :, GLA_QK:2 * GLA_QK].astype(F32)
        q_dec = (q * scale * jnp.exp(b[n])).astype(BF16)
        kr.append((k * jnp.exp(b_last - b[n])).astype(BF16))
        k_inv.append((k * jnp.exp(-b[n])).astype(BF16))
        decay = jnp.exp(b_last3).reshape(nchunk, GLA_QK)
        decay_t.append(jnp.concatenate(
            [decay, jnp.zeros((LANES - nchunk, GLA_QK), F32)], axis=0).T)
        qp.append(q_dec)
        qd.append([jnp.where(hmasks[h], q_dec[:, prs[h]], zero) for h in heads])
    kv, att = [], []
    for n in seqs:
        kv.append([[_dot_tn(kr[n][r, pk[p]], v_ref[n, r, pcols[p]]) for r in chunks]
                   for p in pairs])
        att.append([[jnp.where(causal, _dot_nt(qd[n][h][u], k_inv[n][u, prs[h]]), 0.0).astype(BF16)
                     for u in subs] for h in heads])
    states = []
    for n in seqs:
        per_pair = []
        for p in pairs:
            st = st_ref[n, p]
            per_chunk = []
            for c in range(nchunk):
                per_chunk.append(jnp.where(own_block, st, 0.0).astype(BF16))
                st = st * decay_t[n][pk[p], c:c + 1] + kv[n][p][c]
            st_ref[n, p] = st
            per_pair.append(per_chunk)
        states.append(per_pair)
    for n in seqs:
        inter = [jnp.concatenate([_dot(qp[n][chunks[c], pk[p]], states[n][p][c])
                                  for c in range(nchunk)], axis=0) for p in pairs]
        for h in heads:
            half = slice((h % 2) * GLA_VAL_DIM, (h % 2 + 1) * GLA_VAL_DIM)
            intra = jnp.concatenate([_dot(att[n][h][ui], v_ref[n, u, cols[h]])
                                     for ui, u in enumerate(subs)], axis=0)
            o = intra + inter[h // 2][:, half]
            ms = jnp.mean(o * o, axis=-1, keepdims=True)
            y = o * lax.rsqrt(ms + RMS_EPS) * g_ref[:, cols[h]] * sr_ref[n, :, cols[h]].astype(F32)
            o_ref[n, :, cols[h]] = y.astype(BF16)


def _gla(gqk, gv, la, sr, g, batch, seq):
    blk = lambda w: pl.BlockSpec((GLA_SEQS, TG, w), lambda b, i: (b, i, 0))
    return pl.pallas_call(
        _gla_kernel,
        grid=(batch // GLA_SEQS, seq // TG),
        in_specs=[blk(2 * GLA_QK), blk(GLA_WIDTH), blk(GLA_QK), blk(GLA_WIDTH),
                  pl.BlockSpec((1, GLA_WIDTH), lambda b, i: (0, 0))],
        out_specs=blk(GLA_WIDTH),
        out_shape=jax.ShapeDtypeStruct((batch, seq, GLA_WIDTH), BF16),
        scratch_shapes=[pltpu.VMEM((GLA_SEQS, GLA_HEADS // 2, LANES, 2 * GLA_VAL_DIM), F32)],
        compiler_params=pltpu.CompilerParams(
            dimension_semantics=("arbitrary", "arbitrary"), vmem_limit_bytes=VMEM_LIMIT),
        name="gla",
    )(gqk, gv, la, sr, g)


def _mem_kv_kernel(m_ref, g_ref, w_ref, o_ref):
    h = _rms(m_ref[...], g_ref[...]).astype(BF16)
    o_ref[...] = _dot(h, w_ref[...]).astype(BF16)


def _mem_kv(mem, g, w, mem_len):
    t = mem.shape[0]
    width = w.shape[1]
    return pl.pallas_call(
        _mem_kv_kernel,
        grid=(t // mem_len,),
        in_specs=[pl.BlockSpec((mem_len, D_MODEL), lambda i: (i, 0)),
                  _const_spec((1, D_MODEL)), _const_spec((D_MODEL, width))],
        out_specs=pl.BlockSpec((mem_len, width), lambda i: (i, 0)),
        out_shape=jax.ShapeDtypeStruct((t, width), BF16),
        compiler_params=pltpu.CompilerParams(
            dimension_semantics=("arbitrary",), vmem_limit_bytes=VMEM_LIMIT),
        name="mem_kv",
    )(mem, g, w)


def _mix_out_mem_kernel(x_ref, oft_ref, og_ref, kv_ref, wout_ref, gpost_ref, gpre_ref,
                        wq_ref, wo_ref, gmpost_ref, o_ref):
    width = MEM_HEADS * MEM_HEAD_DIM
    groups = [slice(n * TMO // MIX_GROUPS, (n + 1) * TMO // MIX_GROUPS) for n in range(MIX_GROUPS)]
    y = [_dot_tn(oft_ref[:, r], wout_ref[0:FOX_WIDTH, :]) + _dot(og_ref[r, :], wout_ref[FOX_WIDTH:, :])
         for r in groups]
    x = [x_ref[r, :] + _rms(y[n], gpost_ref[...]) for n, r in enumerate(groups)]
    h = [_rms(xn, gpre_ref[...]).astype(BF16) for xn in x]
    q = [(_dot(hn, wq_ref[...]) * (MEM_HEAD_DIM ** -0.5)).astype(BF16) for hn in h]
    hcols = [slice(hd * MEM_HEAD_DIM, (hd + 1) * MEM_HEAD_DIM) for hd in range(MEM_HEADS)]
    vcols = [slice(width + hd * MEM_HEAD_DIM, width + (hd + 1) * MEM_HEAD_DIM)
             for hd in range(MEM_HEADS)]
    s = [[_dot_nt(qn[:, c], kv_ref[:, c]) for c in hcols] for qn in q]
    o = []
    for sn in s:
        outs = []
        for hd in range(MEM_HEADS):
            m = jnp.max(sn[hd], axis=-1, keepdims=True)
            p = jnp.exp(sn[hd] - m)
            p = p / jnp.sum(p, axis=-1, keepdims=True)
            outs.append(_dot(p.astype(BF16), kv_ref[:, vcols[hd]]).astype(BF16))
        o.append(jnp.concatenate(outs, axis=-1))
    y2 = [_dot(on, wo_ref[...]) for on in o]
    for n, r in enumerate(groups):
        o_ref[r, :] = x[n] + _rms(y2[n], gmpost_ref[...])


def _mix_out_mem(x, o_fox_t, o_gla, kv, wout, gpost, gpre, wq, wo, gmpost, seq, mem_len):
    t = x.shape[0]
    tiles_per_seq = seq // TMO
    row = lambda w: pl.BlockSpec((TMO, w), lambda i: (i, 0))
    return pl.pallas_call(
        _mix_out_mem_kernel,
        grid=(t // TMO,),
        in_specs=[
            row(D_MODEL),
            pl.BlockSpec((FOX_WIDTH, TMO), lambda i: (0, i)),
            row(GLA_WIDTH),
            pl.BlockSpec((mem_len, kv.shape[1]), lambda i: (i // tiles_per_seq, 0)),
            _const_spec((D_MODEL, D_MODEL)), _const_spec((1, D_MODEL)), _const_spec((1, D_MODEL)),
            _const_spec((D_MODEL, D_MODEL)), _const_spec((D_MODEL, D_MODEL)),
            _const_spec((1, D_MODEL)),
        ],
        out_specs=row(D_MODEL),
        out_shape=jax.ShapeDtypeStruct((t, D_MODEL), F32),
        compiler_params=pltpu.CompilerParams(
            dimension_semantics=("arbitrary",), vmem_limit_bytes=VMEM_LIMIT),
        name="mix_out_mem",
    )(x, o_fox_t, o_gla, kv, wout, gpost, gpre, wq, wo, gmpost)


def _pad_cols(w, width):
    return jnp.pad(w, ((0, 0), (0, width - w.shape[1])))


def _aug_base(h):
    return FOX_HEAD_DIM if h % 2 == 0 else 0


def _placement():
    pk = np.zeros((LANES, FOX_WIDTH), np.float32)
    pq = np.zeros((FOX_HEADS * AUG_ROWS, LANES), np.float32)
    for h in range(FOX_HEADS):
        kbase = (h // 2) * LANES + _aug_base(h)
        for t in range(C_TERMS):
            src = t * FOX_HEADS + h
            pq[h * AUG_ROWS + t, src] = 1.0
            pk[ONE_LANE, kbase + t] = 1.0
            pq[h * AUG_ROWS + C_TERMS + t, ONE_LANE] = 1.0
            pk[src, kbase + C_TERMS + t] = -1.0
    return jnp.asarray(pk, BF16), jnp.asarray(pq, BF16)


def kernel(x, mem, g_ff1_pre, w_ff1_gate, w_ff1_up, w_ff1_down, g_ff1_post, g_mix_pre, w_mix_in, b_fox_f, w_gla_g2, b_gla_g, g_fox_out, g_gla_out, w_mix_out, g_mix_post, g_mem_pre, g_mem_src, w_mem_q, w_mem_kv, w_mem_o, g_mem_post, g_ff2_pre, w_ff2_gate, w_ff2_up, w_ff2_down, g_ff2_post, g_final):
    batch, seq, d = x.shape
    mem_len = mem.shape[1]
    depth = w_ff1_gate.shape[0]
    xt = x.reshape(batch * seq, d)
    memt = mem.reshape(batch * mem_len, d)
    gfin = g_final.reshape(1, d)
    vec = lambda v: v.reshape(1, -1).astype(F32)
    pk, pqt = _placement()
    for l in range(depth):
        last = l == depth - 1
        xt = _ffn(xt, vec(g_ff1_pre[l]), w_ff1_gate[l].astype(BF16), w_ff1_up[l].astype(BF16),
                  w_ff1_down[l].astype(BF16), vec(g_ff1_post[l]), gfin, False)

        w_in = w_mix_in[l]
        wqt = (w_in[:, OFF_FQ:OFF_FK] * (LOG2E * FOX_HEAD_DIM ** -0.5)).T.astype(BF16)
        wk = w_in[:, OFF_FK:OFF_FV].astype(BF16)
        wvt = w_in[:, OFF_FV:OFF_FF].T.astype(BF16)
        wgla = w_in[:, OFF_GQ:OFF_GG].astype(BF16)
        wgr = w_in[:, OFF_GR:IN_WIDTH].astype(BF16)
        wsm = _pad_cols(jnp.concatenate([w_in[:, OFF_FF:OFF_GQ], w_in[:, OFF_GG:OFF_GR]], axis=1),
                        LANES).astype(BF16)
        bf = _pad_cols(vec(b_fox_f[l]), LANES)
        wg2 = jnp.pad(w_gla_g2[l].astype(F32),
                      ((FOX_HEADS, LANES - FOX_HEADS - GLA_GATE_RANK), (0, 0)))
        k, qt, vt, gqk, gv, la, sr = _mix_in(xt, vec(g_mix_pre[l]), wk, wqt, wvt, wgla, wgr, wsm,
                                             bf, pk, pqt, wg2, vec(b_gla_g[l]), seq)

        g_fox = jnp.broadcast_to(g_fox_out[l].astype(F32)[:, None], (FOX_WIDTH, TQ))
        o_fox_t = _fox(qt, k, vt, g_fox, batch, seq)
        bsd = lambda a: a.reshape(batch, seq, a.shape[-1])
        o_gla = _gla(bsd(gqk), bsd(gv), bsd(la), bsd(sr), vec(g_gla_out[l]), batch, seq)
        o_gla = o_gla.reshape(batch * seq, GLA_WIDTH)

        kv = _mem_kv(memt, vec(g_mem_src[l]), w_mem_kv[l].astype(BF16), mem_len)
        xt = _mix_out_mem(xt, o_fox_t, o_gla, kv, w_mix_out[l].astype(BF16), vec(g_mix_post[l]),
                          vec(g_mem_pre[l]), w_mem_q[l].astype(BF16), w_mem_o[l].astype(BF16),
                          vec(g_mem_post[l]), seq, mem_len)

        xt = _ffn(xt, vec(g_ff2_pre[l]), w_ff2_gate[l].astype(BF16), w_ff2_up[l].astype(BF16),
                  w_ff2_down[l].astype(BF16), vec(g_ff2_post[l]), gfin, last)
    return xt.reshape(batch, seq, d)
```

```python
import functools

import numpy as np
import jax
import jax.numpy as jnp
from jax import lax
from jax.experimental import pallas as pl
from jax.experimental.pallas import tpu as pltpu

F32 = jnp.float32
BF16 = jnp.bfloat16

D_MODEL = 1024
CHUNK = 64
FOX_WIDTH = 512
FOX_HEADS = 8
FOX_HEAD_DIM = 64
GLA_WIDTH = 512
GLA_HEADS = 4
GLA_VAL_DIM = 128
GLA_KEY_DIM = 64
GLA_QK = 256
GLA_GATE_RANK = 16
GLA_GATE_NORM = 16.0
MEM_HEADS = 4
MEM_HEAD_DIM = 256
D_FF = 2816
RMS_EPS = 1e-6

OFF_FQ = 0
OFF_FK = OFF_FQ + FOX_WIDTH
OFF_FV = OFF_FK + FOX_WIDTH
OFF_FF = OFF_FV + FOX_WIDTH
OFF_GQ = OFF_FF + FOX_HEADS
OFF_GK = OFF_GQ + GLA_QK
OFF_GV = OFF_GK + GLA_QK
OFF_GG = OFF_GV + GLA_WIDTH
OFF_GR = OFF_GG + GLA_GATE_RANK
IN_WIDTH = OFF_GR + GLA_WIDTH

LANES = 128
VMEM_LIMIT = 56 * 1024 * 1024

TM = 256
MIXIN_GROUPS = 4
TMI = TM * MIXIN_GROUPS
TMF = 1024
TQ = 256
TG = 512
GLA_SEQS = 4
GLA_SUB = 256
SCORE_LEAD = 1
FFN_GROUPS = 8
TMO = 1024
MIX_GROUPS = 4

FOX_PAD = FOX_HEADS * LANES
C_TERMS = 3
ONE_LANE = C_TERMS * FOX_HEADS
AUG_ROWS = 16
V_ONE_ROW = FOX_HEAD_DIM
V_ROWS = FOX_HEAD_DIM + AUG_ROWS
LOG2E = 1.4426950408889634


def _rms(x, g):
    return x * lax.rsqrt(jnp.mean(x * x, axis=-1, keepdims=True) + RMS_EPS) * g


def _dot(a, b):
    return jnp.dot(a, b, preferred_element_type=F32)


def _dot_nt(a, b):
    return lax.dot_general(a, b, (((1,), (1,)), ((), ())), preferred_element_type=F32)


def _dot_tn(a, b):
    return lax.dot_general(a, b, (((0,), (0,)), ((), ())), preferred_element_type=F32)


def _split3(x):
    hi = x.astype(BF16)
    r1 = x - hi.astype(F32)
    mid = r1.astype(BF16)
    lo = (r1 - mid.astype(F32)).astype(BF16)
    return hi, mid, lo


def _tri_cumsum(tri_bf16, x):
    hi, mid, lo = _split3(x)
    return _dot(tri_bf16, hi) + _dot(tri_bf16, mid) + _dot(tri_bf16, lo)


def _pack_terms(x, one_lane=None):
    hi, mid, lo = _split3(x)
    out = (hi.astype(F32) + pltpu.roll(mid.astype(F32), FOX_HEADS, 1)
           + pltpu.roll(lo.astype(F32), 2 * FOX_HEADS, 1))
    if one_lane is not None:
        lane = lax.broadcasted_iota(jnp.int32, (1, LANES), 1)
        out = out + jnp.where(lane == one_lane, 1.0, 0.0)
    return out.astype(BF16)


def _log_sigmoid(x):
    return jnp.minimum(x, 0.0) - jnp.log1p(jnp.exp(-jnp.abs(x)))


def _silu(x):
    return x * jax.nn.sigmoid(x)


def _const_spec(shape):
    return pl.BlockSpec(shape, lambda *_: (0,) * len(shape), pipeline_mode=pl.Buffered(1))


def _ffn_kernel(x_ref, gpre_ref, wg_ref, wu_ref, wd_ref, gpost_ref, gfin_ref, o_ref, *, final):
    groups = [slice(n * TMF // FFN_GROUPS, (n + 1) * TMF // FFN_GROUPS) for n in range(FFN_GROUPS)]
    x = [x_ref[r, :] for r in groups]
    h = [_rms(xn, gpre_ref[...]).astype(BF16) for xn in x]
    a = [(_silu(_dot(hn, wg_ref[...])) * _dot(hn, wu_ref[...])).astype(BF16) for hn in h]
    y = [_dot(an, wd_ref[...]) for an in a]
    for n, r in enumerate(groups):
        out = x[n] + 0.5 * _rms(y[n], gpost_ref[...])
        if final:
            out = _rms(out, gfin_ref[...])
        o_ref[r, :] = out


def _ffn(x, gpre, wg, wu, wd, gpost, gfin, final):
    t = x.shape[0]
    return pl.pallas_call(
        functools.partial(_ffn_kernel, final=final),
        grid=(t // TMF,),
        in_specs=[
            pl.BlockSpec((TMF, D_MODEL), lambda i: (i, 0)),
            _const_spec((1, D_MODEL)),
            _const_spec((D_MODEL, D_FF)),
            _const_spec((D_MODEL, D_FF)),
            _const_spec((D_FF, D_MODEL)),
            _const_spec((1, D_MODEL)),
            _const_spec((1, D_MODEL)),
        ],
        out_specs=pl.BlockSpec((TMF, D_MODEL), lambda i: (i, 0)),
        out_shape=jax.ShapeDtypeStruct((t, D_MODEL), F32),
        compiler_params=pltpu.CompilerParams(
            dimension_semantics=("arbitrary",), vmem_limit_bytes=VMEM_LIMIT),
        name="ffn_final" if final else "ffn",
    )(x, gpre, wg, wu, wd, gpost, gfin)


def _mix_in_kernel(x_ref, g_ref, wk_ref, wqt_ref, wvt_ref, wgla_ref, wgr_ref, wsm_ref,
                   bf_ref, pk_ref, pqt_ref, wg2_ref, bg_ref,
                   k_ref, qt_ref, vt_ref, gqk_ref, gv_ref, la_ref, sr_ref,
                   carry_ref, *, tiles_per_seq):
    i = pl.program_id(0)

    @pl.when(i % tiles_per_seq == 0)
    def _():
        carry_ref[...] = jnp.zeros_like(carry_ref)

    gs = range(MIXIN_GROUPS)
    groups = [slice(n * TM, (n + 1) * TM) for n in gs]
    lane = lax.broadcasted_iota(jnp.int32, (1, LANES), 1)
    low = lane < FOX_HEAD_DIM
    row = lax.broadcasted_iota(jnp.int32, (TM, TM), 0)
    col = lax.broadcasted_iota(jnp.int32, (TM, TM), 1)
    tri = jnp.where(col <= row, 1.0, 0.0).astype(BF16)

    h = [_rms(x_ref[r, :], g_ref[...]).astype(BF16) for r in groups]
    gl = [_dot(hn, wgla_ref[...]) for hn in h]
    for n, r in enumerate(groups):
        gqk_ref[r, :] = gl[n][:, :2 * GLA_QK].astype(BF16)
        gv_ref[r, :] = gl[n][:, 2 * GLA_QK:].astype(BF16)
    gr = [_dot(hn, wgr_ref[...]) for hn in h]
    for n, r in enumerate(groups):
        sr_ref[r, :] = _silu(gr[n]).astype(BF16)

    sm = [_dot(hn, wsm_ref[...]) for hn in h]

    carry = carry_ref[...]
    packed = []
    for n in gs:
        lf = jnp.where(lane < FOX_HEADS, _log_sigmoid(sm[n] + bf_ref[...]), 0.0)
        c3 = _dot(tri, _pack_terms(lf))
        c = jnp.where(lane < FOX_HEADS,
                      c3 + pltpu.roll(c3, LANES - FOX_HEADS, 1)
                      + pltpu.roll(c3, LANES - 2 * FOX_HEADS, 1), 0.0) + carry
        carry = c[TM - 1:TM, :]
        packed.append(_pack_terms(c * LOG2E, one_lane=ONE_LANE))
    carry_ref[...] = carry

    kp = [_dot(hn, wk_ref[...]) for hn in h]
    kaug = [_dot(pn, pk_ref[...]) for pn in packed]
    for n, r in enumerate(groups):
        for hd in range(FOX_HEADS):
            pair = slice((hd // 2) * LANES, (hd // 2 + 1) * LANES)
            slot = slice(hd * LANES, (hd + 1) * LANES)
            own = low if hd % 2 == 0 else jnp.logical_not(low)
            k_ref[r, slot] = jnp.where(own, kp[n][:, pair], kaug[n][:, pair]).astype(BF16)

    qt = [_dot_nt(wqt_ref[...], hn).astype(BF16) for hn in h]
    aug = [_dot_nt(pqt_ref[...], pn).astype(BF16) for pn in packed]
    vt = [_dot_nt(wvt_ref[...], hn).astype(BF16) for hn in h]
    one_rows = jnp.where(lax.broadcasted_iota(jnp.int32, (AUG_ROWS, TQ), 0) == 0,
                         1.0, 0.0).astype(BF16)
    zeros = jnp.zeros((LANES - FOX_HEAD_DIM - AUG_ROWS, TQ), BF16)
    for n in gs:
        for b in range(TM // TQ):
            cs = slice(b * TQ, (b + 1) * TQ)
            blk = n * (TM // TQ) + b
            for hd in range(FOX_HEADS):
                q_h = qt[n][hd * FOX_HEAD_DIM:(hd + 1) * FOX_HEAD_DIM, cs]
                a_h = aug[n][hd * AUG_ROWS:(hd + 1) * AUG_ROWS, cs]
                pieces = (q_h, a_h, zeros) if hd % 2 == 0 else (a_h, zeros, q_h)
                p0 = hd * LANES
                for piece in pieces:
                    qt_ref[blk, p0:p0 + piece.shape[0], :] = piece
                    p0 += piece.shape[0]
                vb = hd * V_ROWS
                vt_ref[blk, vb:vb + FOX_HEAD_DIM, :] = vt[n][hd * FOX_HEAD_DIM:
                                                             (hd + 1) * FOX_HEAD_DIM, cs]
                vt_ref[blk, vb + FOX_HEAD_DIM:vb + V_ROWS, :] = one_rows

    w_hi, w_mid, _ = _split3(wg2_ref[...])
    for n, r in enumerate(groups):
        g_hi, g_mid, _ = _split3(sm[n])
        gate = _dot(g_hi, w_hi) + _dot(g_mid, w_hi) + _dot(g_hi, w_mid) + bg_ref[...]
        la_ref[r, :] = _log_sigmoid(gate) / GLA_GATE_NORM


def _mix_in(x, g, wk, wqt, wvt, wgla, wgr, wsm, bf, pk, pqt, wg2, bg, seq):
    t = x.shape[0]
    row = lambda w: pl.BlockSpec((TMI, w), lambda i: (i, 0))
    tblk = lambda rows: pl.BlockSpec((TMI // TQ, rows, TQ), lambda i: (i, 0, 0))
    return pl.pallas_call(
        functools.partial(_mix_in_kernel, tiles_per_seq=seq // TMI),
        grid=(t // TMI,),
        in_specs=[
            row(D_MODEL),
            _const_spec((1, D_MODEL)),
            _const_spec((D_MODEL, FOX_WIDTH)),
            _const_spec((FOX_WIDTH, D_MODEL)),
            _const_spec((FOX_WIDTH, D_MODEL)),
            _const_spec((D_MODEL, 2 * GLA_QK + GLA_WIDTH)),
            _const_spec((D_MODEL, GLA_WIDTH)),
            _const_spec((D_MODEL, LANES)),
            _const_spec((1, LANES)),
            _const_spec((LANES, FOX_WIDTH)),
            _const_spec((FOX_HEADS * AUG_ROWS, LANES)),
            _const_spec((LANES, GLA_QK)),
            _const_spec((1, GLA_QK)),
        ],
        out_specs=[row(FOX_PAD), tblk(FOX_PAD), tblk(FOX_HEADS * V_ROWS), row(2 * GLA_QK),
                   row(GLA_WIDTH), row(GLA_QK), row(GLA_WIDTH)],
        out_shape=[
            jax.ShapeDtypeStruct((t, FOX_PAD), BF16),
            jax.ShapeDtypeStruct((t // TQ, FOX_PAD, TQ), BF16),
            jax.ShapeDtypeStruct((t // TQ, FOX_HEADS * V_ROWS, TQ), BF16),
            jax.ShapeDtypeStruct((t, 2 * GLA_QK), BF16),
            jax.ShapeDtypeStruct((t, GLA_WIDTH), BF16),
            jax.ShapeDtypeStruct((t, GLA_QK), F32),
            jax.ShapeDtypeStruct((t, GLA_WIDTH), BF16),
        ],
        scratch_shapes=[pltpu.VMEM((1, LANES), F32)],
        compiler_params=pltpu.CompilerParams(
            dimension_semantics=("arbitrary",), vmem_limit_bytes=VMEM_LIMIT),
        name="mix_in",
    )(x, g, wk, wqt, wvt, wgla, wgr, wsm, bf, pk, pqt, wg2, bg)


def _fox_kernel(qt_ref, k_ref, vt_ref, g_ref, o_ref, m_ref, acc_ref, sa_ref, sb_ref):
    i = pl.program_id(1)
    m_ref[...] = jnp.full(m_ref.shape, -jnp.inf, F32)
    acc_ref[...] = jnp.zeros(acc_ref.shape, F32)
    key = lax.broadcasted_iota(jnp.int32, (TQ, TQ), 0)
    qry = lax.broadcasted_iota(jnp.int32, (TQ, TQ), 1)
    causal = key <= qry
    slots = [slice(h * LANES, (h + 1) * LANES) for h in range(FOX_HEADS)]
    vrows = [slice(h * V_ROWS, (h + 1) * V_ROWS) for h in range(FOX_HEADS)]
    both = (0, 1)

    def score(j, s_ref, g, h):
        start = pl.multiple_of(j * TQ, TQ)
        s_ref[g, h] = _dot(k_ref[pl.ds(start, TQ), slots[h]], qt_ref[g, slots[h], :])

    def accumulate(j, s_ref, g, h, diagonal):
        s = s_ref[g, h]
        if diagonal:
            s = jnp.where(causal, s, -jnp.inf)
        m_old = m_ref[g, h:h + 1, :]
        m_new = jnp.maximum(m_old, jnp.max(s, axis=0, keepdims=True))
        alpha = jnp.exp2(m_old - m_new)
        p = jnp.exp2(s - m_new).astype(BF16)
        m_ref[g, h:h + 1, :] = m_new
        acc_ref[g, h] = alpha * acc_ref[g, h] + _dot(vt_ref[j, vrows[h], :], p)

    def stage(cur, nxt):
        todo = [(g, h) for g in nxt[2] for h in range(FOX_HEADS)] if nxt else []
        units = [(g, h) for g in cur[2] for h in range(FOX_HEADS)] if cur else []
        for g, h in todo[:SCORE_LEAD]:
            score(nxt[0], nxt[1], g, h)
        todo = todo[SCORE_LEAD:]
        per_unit = -(-len(todo) // len(units)) if units else 0
        for g, h in units:
            accumulate(cur[0], cur[1], g, h, g == cur[3])
            for g2, h2 in todo[:per_unit]:
                score(nxt[0], nxt[1], g2, h2)
            todo = todo[per_unit:]
        for g, h in todo:
            score(nxt[0], nxt[1], g, h)

    stage(None, (0, sa_ref, both))

    def pair(t, carry):
        j = 2 * t
        stage((j, sa_ref, both, None), (j + 1, sb_ref, both))
        stage((j + 1, sb_ref, both, None), (j + 2, sa_ref, both))
        return carry

    def quad(t, carry):
        pair(2 * t, carry)
        return pair(2 * t + 1, carry)

    lax.fori_loop(0, i // 2, quad, 0)

    @pl.when(i % 2 == 1)
    def _():
        pair(i - 1, 0)

    stage((2 * i, sa_ref, both, 0), (2 * i + 1, sb_ref, (1,)))
    stage((2 * i + 1, sb_ref, (1,), 1), None)

    for g in both:
        for h in range(FOX_HEADS):
            acc = acc_ref[g, h]
            o = acc[:FOX_HEAD_DIM, :] / acc[V_ONE_ROW:V_ONE_ROW + 1, :]
            ms = jnp.mean(o * o, axis=0, keepdims=True)
            rows = slice(h * FOX_HEAD_DIM, (h + 1) * FOX_HEAD_DIM)
            o_ref[rows, g * TQ:(g + 1) * TQ] = (o * lax.rsqrt(ms + RMS_EPS)
                                                * g_ref[rows, :]).astype(BF16)


def _fox(qt, k, vt, g, batch, seq):
    nq = seq // TQ
    steps = nq // 2
    return pl.pallas_call(
        _fox_kernel,
        grid=(batch, steps),
        in_specs=[
            pl.BlockSpec((2, FOX_PAD, TQ), lambda b, i: (b * steps + i, 0, 0)),
            pl.BlockSpec((seq, FOX_PAD), lambda b, i: (b, 0)),
            pl.BlockSpec((nq, FOX_HEADS * V_ROWS, TQ), lambda b, i: (b, 0, 0)),
            _const_spec((FOX_WIDTH, TQ)),
        ],
        out_specs=pl.BlockSpec((FOX_WIDTH, 2 * TQ), lambda b, i: (0, b * steps + i)),
        out_shape=jax.ShapeDtypeStruct((FOX_WIDTH, batch * seq), BF16),
        scratch_shapes=[pltpu.VMEM((2, FOX_HEADS, TQ), F32),
                        pltpu.VMEM((2, FOX_HEADS, V_ROWS, TQ), F32),
                        pltpu.VMEM((2, FOX_HEADS, TQ, TQ), F32),
                        pltpu.VMEM((2, FOX_HEADS, TQ, TQ), F32)],
        compiler_params=pltpu.CompilerParams(
            dimension_semantics=("arbitrary", "arbitrary"), vmem_limit_bytes=VMEM_LIMIT),
        name="fox",
    )(qt, k, vt, g)


def _gla_kernel(qk_ref, v_ref, la_ref, sr_ref, g_ref, o_ref, st_ref):
    @pl.when(pl.program_id(1) == 0)
    def _():
        st_ref[...] = jnp.zeros_like(st_ref)

    nchunk = TG // CHUNK
    seqs = range(GLA_SEQS)
    heads = range(GLA_HEADS)
    lane = lax.broadcasted_iota(jnp.int32, (1, LANES), 1)
    first = lane < GLA_KEY_DIM
    subs = [slice(u * GLA_SUB, (u + 1) * GLA_SUB) for u in range(TG // GLA_SUB)]
    row = lax.broadcasted_iota(jnp.int32, (GLA_SUB, GLA_SUB), 0)
    col = lax.broadcasted_iota(jnp.int32, (GLA_SUB, GLA_SUB), 1)
    causal = (row - col).astype(jnp.uint32) <= (row & (CHUNK - 1)).astype(jnp.uint32)
    tri = jnp.where(causal, 1.0, 0.0).astype(BF16)
    scale = GLA_KEY_DIM ** -0.5
    zero = jnp.zeros((TG, LANES), BF16)
    chunks = [slice(c * CHUNK, (c + 1) * CHUNK) for c in range(nchunk)]
    prs = [slice((h // 2) * LANES, (h // 2 + 1) * LANES) for h in heads]
    cols = [slice(h * GLA_VAL_DIM, (h + 1) * GLA_VAL_DIM) for h in heads]
    hmasks = [first if h % 2 == 0 else jnp.logical_not(first) for h in heads]

    srow = lax.broadcasted_iota(jnp.int32, (LANES, 2 * GLA_VAL_DIM), 0) // GLA_KEY_DIM
    scol = lax.broadcasted_iota(jnp.int32, (LANES, 2 * GLA_VAL_DIM), 1) // GLA_VAL_DIM
    own_block = srow == scol
    pairs = range(GLA_HEADS // 2)
    pcols = [slice(p * 2 * GLA_VAL_DIM, (p + 1) * 2 * GLA_VAL_DIM) for p in pairs]
    pk = [slice(p * LANES, (p + 1) * LANES) for p in pairs]

    b = [jnp.concatenate([_tri_cumsum(tri, la_ref[n, u, :]) for u in subs], axis=0)
         for n in seqs]
    qp, qd, kr, k_inv, decay_t = [], [], [], [], []
    for n in seqs:
        b_last3 = b[n].reshape(nchunk, CHUNK, GLA_QK)[:, CHUNK - 1:CHUNK, :]
        b_last = jnp.broadcast_to(b_last3, (nchunk, CHUNK, GLA_QK)).reshape(TG, GLA_QK)
        q = qk_ref[n, :, 0:GLA_QK].astype(F32)
        k = qk_ref[n, :, GLA_QK:2 * GLA_QK].astype(F32)
        q_dec = (q * scale * jnp.exp(b[n])).astype(BF16)
        kr.append((k * jnp.exp(b_last - b[n])).astype(BF16))
        k_inv.append((k * jnp.exp(-b[n])).astype(BF16))
        decay = jnp.exp(b_last3).reshape(nchunk, GLA_QK)
        decay_t.append(jnp.concatenate(
            [decay, jnp.zeros((LANES - nchunk, GLA_QK), F32)], axis=0).T)
        qp.append(q_dec)
        qd.append([jnp.where(hmasks[h], q_dec[:, prs[h]], zero) for h in heads])
    kv, att = [], []
    for n in seqs:
        kv.append([[_dot_tn(kr[n][r, pk[p]], v_ref[n, r, pcols[p]]) for r in chunks]
                   for p in pairs])
        att.append([[jnp.where(causal, _dot_nt(qd[n][h][u], k_inv[n][u, prs[h]]), 0.0).astype(BF16)
                     for u in subs] for h in heads])
    states = []
    for n in seqs:
        per_pair = []
        for p in pairs:
            st = st_ref[n, p]
            per_chunk = []
            for c in range(nchunk):
                per_chunk.append(jnp.where(own_block, st, 0.0).astype(BF16))
                st = st * decay_t[n][pk[p], c:c + 1] + kv[n][p][c]
            st_ref[n, p] = st
            per_pair.append(per_chunk)
        states.append(per_pair)
    for n in seqs:
        inter = [jnp.concatenate([_dot(qp[n][chunks[c], pk[p]], states[n][p][c])
                                  for c in range(nchunk)], axis=0) for p in pairs]
        for h in heads:
            half = slice((h % 2) * GLA_VAL_DIM, (h % 2 + 1) * GLA_VAL_DIM)
            intra = jnp.concatenate([_dot(att[n][h][ui], v_ref[n, u, cols[h]])
                                     for ui, u in enumerate(subs)], axis=0)
            o = intra + inter[h // 2][:, half]
            ms = jnp.mean(o * o, axis=-1, keepdims=True)
            y = o * lax.rsqrt(ms + RMS_EPS) * g_ref[:, cols[h]] * sr_ref[n, :, cols[h]].astype(F32)
            o_ref[n, :, cols[h]] = y.astype(BF16)


def _gla(gqk, gv, la, sr, g, batch, seq):
    blk = lambda w: pl.BlockSpec((GLA_SEQS, TG, w), lambda b, i: (b, i, 0))
    return pl.pallas_call(
        _gla_kernel,
        grid=(batch // GLA_SEQS, seq // TG),
        in_specs=[blk(2 * GLA_QK), blk(GLA_WIDTH), blk(GLA_QK), blk(GLA_WIDTH),
                  pl.BlockSpec((1, GLA_WIDTH), lambda b, i: (0, 0))],
        out_specs=blk(GLA_WIDTH),
        out_shape=jax.ShapeDtypeStruct((batch, seq, GLA_WIDTH), BF16),
        scratch_shapes=[pltpu.VMEM((GLA_SEQS, GLA_HEADS // 2, LANES, 2 * GLA_VAL_DIM), F32)],
        compiler_params=pltpu.CompilerParams(
            dimension_semantics=("arbitrary", "arbitrary"), vmem_limit_bytes=VMEM_LIMIT),
        name="gla",
    )(gqk, gv, la, sr, g)


def _mem_kv_kernel(m_ref, g_ref, w_ref, o_ref):
    h = _rms(m_ref[...], g_ref[...]).astype(BF16)
    o_ref[...] = _dot(h, w_ref[...]).astype(BF16)


def _mem_kv(mem, g, w, mem_len):
    t = mem.shape[0]
    width = w.shape[1]
    return pl.pallas_call(
        _mem_kv_kernel,
        grid=(t // mem_len,),
        in_specs=[pl.BlockSpec((mem_len, D_MODEL), lambda i: (i, 0)),
                  _const_spec((1, D_MODEL)), _const_spec((D_MODEL, width))],
        out_specs=pl.BlockSpec((mem_len, width), lambda i: (i, 0)),
        out_shape=jax.ShapeDtypeStruct((t, width), BF16),
        compiler_params=pltpu.CompilerParams(
            dimension_semantics=("arbitrary",), vmem_limit_bytes=VMEM_LIMIT),
        name="mem_kv",
    )(mem, g, w)


def _mix_out_mem_kernel(x_ref, oft_ref, og_ref, kv_ref, wout_ref, gpost_ref, gpre_ref,
                        wq_ref, wo_ref, gmpost_ref, o_ref):
    width = MEM_HEADS * MEM_HEAD_DIM
    groups = [slice(n * TMO // MIX_GROUPS, (n + 1) * TMO // MIX_GROUPS) for n in range(MIX_GROUPS)]
    y = [_dot_tn(oft_ref[:, r], wout_ref[0:FOX_WIDTH, :]) + _dot(og_ref[r, :], wout_ref[FOX_WIDTH:, :])
         for r in groups]
    x = [x_ref[r, :] + _rms(y[n], gpost_ref[...]) for n, r in enumerate(groups)]
    h = [_rms(xn, gpre_ref[...]).astype(BF16) for xn in x]
    q = [(_dot(hn, wq_ref[...]) * (MEM_HEAD_DIM ** -0.5)).astype(BF16) for hn in h]
    hcols = [slice(hd * MEM_HEAD_DIM, (hd + 1) * MEM_HEAD_DIM) for hd in range(MEM_HEADS)]
    vcols = [slice(width + hd * MEM_HEAD_DIM, width + (hd + 1) * MEM_HEAD_DIM)
             for hd in range(MEM_HEADS)]
    s = [[_dot_nt(qn[:, c], kv_ref[:, c]) for c in hcols] for qn in q]
    o = []
    for sn in s:
        outs = []
        for hd in range(MEM_HEADS):
            m = jnp.max(sn[hd], axis=-1, keepdims=True)
            p = jnp.exp(sn[hd] - m)
            p = p / jnp.sum(p, axis=-1, keepdims=True)
            outs.append(_dot(p.astype(BF16), kv_ref[:, vcols[hd]]).astype(BF16))
        o.append(jnp.concatenate(outs, axis=-1))
    y2 = [_dot(on, wo_ref[...]) for on in o]
    for n, r in enumerate(groups):
        o_ref[r, :] = x[n] + _rms(y2[n], gmpost_ref[...])


def _mix_out_mem(x, o_fox_t, o_gla, kv, wout, gpost, gpre, wq, wo, gmpost, seq, mem_len):
    t = x.shape[0]
    tiles_per_seq = seq // TMO
    row = lambda w: pl.BlockSpec((TMO, w), lambda i: (i, 0))
    return pl.pallas_call(
        _mix_out_mem_kernel,
        grid=(t // TMO,),
        in_specs=[
            row(D_MODEL),
            pl.BlockSpec((FOX_WIDTH, TMO), lambda i: (0, i)),
            row(GLA_WIDTH),
            pl.BlockSpec((mem_len, kv.shape[1]), lambda i: (i // tiles_per_seq, 0)),
            _const_spec((D_MODEL, D_MODEL)), _const_spec((1, D_MODEL)), _const_spec((1, D_MODEL)),
            _const_spec((D_MODEL, D_MODEL)), _const_spec((D_MODEL, D_MODEL)),
            _const_spec((1, D_MODEL)),
        ],
        out_specs=row(D_MODEL),
        out_shape=jax.ShapeDtypeStruct((t, D_MODEL), F32),
        compiler_params=pltpu.CompilerParams(
            dimension_semantics=("arbitrary",), vmem_limit_bytes=VMEM_LIMIT),
        name="mix_out_mem",
    )(x, o_fox_t, o_gla, kv, wout, gpost, gpre, wq, wo, gmpost)


def _pad_cols(w, width):
    return jnp.pad(w, ((0, 0), (0, width - w.shape[1])))


def _aug_base(h):
    return FOX_HEAD_DIM if h % 2 == 0 else 0


def _placement():
    pk = np.zeros((LANES, FOX_WIDTH), np.float32)
    pq = np.zeros((FOX_HEADS * AUG_ROWS, LANES), np.float32)
    for h in range(FOX_HEADS):
        kbase = (h // 2) * LANES + _aug_base(h)
        for t in range(C_TERMS):
            src = t * FOX_HEADS + h
            pq[h * AUG_ROWS + t, src] = 1.0
            pk[ONE_LANE, kbase + t] = 1.0
            pq[h * AUG_ROWS + C_TERMS + t, ONE_LANE] = 1.0
            pk[src, kbase + C_TERMS + t] = -1.0
    return jnp.asarray(pk, BF16), jnp.asarray(pq, BF16)


def kernel(x, mem, g_ff1_pre, w_ff1_gate, w_ff1_up, w_ff1_down, g_ff1_post, g_mix_pre, w_mix_in, b_fox_f, w_gla_g2, b_gla_g, g_fox_out, g_gla_out, w_mix_out, g_mix_post, g_mem_pre, g_mem_src, w_mem_q, w_mem_kv, w_mem_o, g_mem_post, g_ff2_pre, w_ff2_gate, w_ff2_up, w_ff2_down, g_ff2_post, g_final):
    batch, seq, d = x.shape
    mem_len = mem.shape[1]
    depth = w_ff1_gate.shape[0]
    xt = x.reshape(batch * seq, d)
    memt = mem.reshape(batch * mem_len, d)
    gfin = g_final.reshape(1, d)
    vec = lambda v: v.reshape(1, -1).astype(F32)
    pk, pqt = _placement()
    for l in range(depth):
        last = l == depth - 1
        xt = _ffn(xt, vec(g_ff1_pre[l]), w_ff1_gate[l].astype(BF16), w_ff1_up[l].astype(BF16),
                  w_ff1_down[l].astype(BF16), vec(g_ff1_post[l]), gfin, False)

        w_in = w_mix_in[l]
        wqt = (w_in[:, OFF_FQ:OFF_FK] * (LOG2E * FOX_HEAD_DIM ** -0.5)).T.astype(BF16)
        wk = w_in[:, OFF_FK:OFF_FV].astype(BF16)
        wvt = w_in[:, OFF_FV:OFF_FF].T.astype(BF16)
        wgla = w_in[:, OFF_GQ:OFF_GG].astype(BF16)
        wgr = w_in[:, OFF_GR:IN_WIDTH].astype(BF16)
        wsm = _pad_cols(jnp.concatenate([w_in[:, OFF_FF:OFF_GQ], w_in[:, OFF_GG:OFF_GR]], axis=1),
                        LANES).astype(BF16)
        bf = _pad_cols(vec(b_fox_f[l]), LANES)
        wg2 = jnp.pad(w_gla_g2[l].astype(F32),
                      ((FOX_HEADS, LANES - FOX_HEADS - GLA_GATE_RANK), (0, 0)))
        k, qt, vt, gqk, gv, la, sr = _mix_in(xt, vec(g_mix_pre[l]), wk, wqt, wvt, wgla, wgr, wsm,
                                             bf, pk, pqt, wg2, vec(b_gla_g[l]), seq)

        g_fox = jnp.broadcast_to(g_fox_out[l].astype(F32)[:, None], (FOX_WIDTH, TQ))
        o_fox_t = _fox(qt, k, vt, g_fox, batch, seq)
        bsd = lambda a: a.reshape(batch, seq, a.shape[-1])
        o_gla = _gla(bsd(gqk), bsd(gv), bsd(la), bsd(sr), vec(g_gla_out[l]), batch, seq)
        o_gla = o_gla.reshape(batch * seq, GLA_WIDTH)

        kv = _mem_kv(memt, vec(g_mem_src[l]), w_mem_kv[l].astype(BF16), mem_len)
        xt = _mix_out_mem(xt, o_fox_t, o_gla, kv, w_mix_out[l].astype(BF16), vec(g_mix_post[l]),
                          vec(g_mem_pre[l]), w_mem_q[l].astype(BF16), w_mem_o[l].astype(BF16),
                          vec(g_mem_post[l]), seq, mem_len)

        xt = _ffn(xt, vec(g_ff2_pre[l]), w_ff2_gate[l].astype(BF16), w_ff2_up[l].astype(BF16),
                  w_ff2_down[l].astype(BF16), vec(g_ff2_post[l]), gfin, last)
    return xt.reshape(batch, seq, d)
```

```python
import functools

import numpy as np
import jax
import jax.numpy as jnp
from jax import lax
from jax.experimental import pallas as pl
from jax.experimental.pallas import tpu as pltpu

F32 = jnp.float32
BF16 = jnp.bfloat16

D_MODEL = 1024
CHUNK = 64
FOX_WIDTH = 512
FOX_HEADS = 8
FOX_HEAD_DIM = 64
GLA_WIDTH = 512
GLA_HEADS = 4
GLA_VAL_DIM = 128
GLA_KEY_DIM = 64
GLA_QK = 256
GLA_GATE_RANK = 16
GLA_GATE_NORM = 16.0
MEM_HEADS = 4
MEM_HEAD_DIM = 256
D_FF = 2816
RMS_EPS = 1e-6

OFF_FQ = 0
OFF_FK = OFF_FQ + FOX_WIDTH
OFF_FV = OFF_FK + FOX_WIDTH
OFF_FF = OFF_FV + FOX_WIDTH
OFF_GQ = OFF_FF + FOX_HEADS
OFF_GK = OFF_GQ + GLA_QK
OFF_GV = OFF_GK + GLA_QK
OFF_GG = OFF_GV + GLA_WIDTH
OFF_GR = OFF_GG + GLA_GATE_RANK
IN_WIDTH = OFF_GR + GLA_WIDTH

LANES = 128
VMEM_LIMIT = 56 * 1024 * 1024

TM = 256
MIXIN_GROUPS = 4
TMI = TM * MIXIN_GROUPS
TMF = 1024
TQ = 256
TG = 512
GLA_SEQS = 4
GLA_SUB = 256
SCORE_LEAD = 1
FFN_GROUPS = 8
TMO = 1024
MIX_GROUPS = 4

FOX_PAD = FOX_HEADS * LANES
C_TERMS = 3
ONE_LANE = C_TERMS * FOX_HEADS
AUG_ROWS = 16
V_ONE_ROW = FOX_HEAD_DIM
V_ROWS = FOX_HEAD_DIM + AUG_ROWS
LOG2E = 1.4426950408889634


def _rms(x, g):
    return x * lax.rsqrt(jnp.mean(x * x, axis=-1, keepdims=True) + RMS_EPS) * g


def _dot(a, b):
    return jnp.dot(a, b, preferred_element_type=F32)


def _dot_nt(a, b):
    return lax.dot_general(a, b, (((1,), (1,)), ((), ())), preferred_element_type=F32)


def _dot_tn(a, b):
    return lax.dot_general(a, b, (((0,), (0,)), ((), ())), preferred_element_type=F32)


def _split3(x):
    hi = x.astype(BF16)
    r1 = x - hi.astype(F32)
    mid = r1.astype(BF16)
    lo = (r1 - mid.astype(F32)).astype(BF16)
    return hi, mid, lo


def _tri_cumsum(tri_bf16, x):
    hi, mid, lo = _split3(x)
    return _dot(tri_bf16, hi) + _dot(tri_bf16, mid) + _dot(tri_bf16, lo)


def _pack_terms(x, one_lane=None):
    hi, mid, lo = _split3(x)
    out = (hi.astype(F32) + pltpu.roll(mid.astype(F32), FOX_HEADS, 1)
           + pltpu.roll(lo.astype(F32), 2 * FOX_HEADS, 1))
    if one_lane is not None:
        lane = lax.broadcasted_iota(jnp.int32, (1, LANES), 1)
        out = out + jnp.where(lane == one_lane, 1.0, 0.0)
    return out.astype(BF16)


def _log_sigmoid(x):
    return jnp.minimum(x, 0.0) - jnp.log1p(jnp.exp(-jnp.abs(x)))


def _silu(x):
    return x * jax.nn.sigmoid(x)


def _const_spec(shape):
    return pl.BlockSpec(shape, lambda *_: (0,) * len(shape), pipeline_mode=pl.Buffered(1))


def _ffn_kernel(x_ref, gpre_ref, wg_ref, wu_ref, wd_ref, gpost_ref, gfin_ref, o_ref, *, final):
    groups = [slice(n * TMF // FFN_GROUPS, (n + 1) * TMF // FFN_GROUPS) for n in range(FFN_GROUPS)]
    x = [x_ref[r, :] for r in groups]
    h = [_rms(xn, gpre_ref[...]).astype(BF16) for xn in x]
    a = [(_silu(_dot(hn, wg_ref[...])) * _dot(hn, wu_ref[...])).astype(BF16) for hn in h]
    y = [_dot(an, wd_ref[...]) for an in a]
    for n, r in enumerate(groups):
        out = x[n] + 0.5 * _rms(y[n], gpost_ref[...])
        if final:
            out = _rms(out, gfin_ref[...])
        o_ref[r, :] = out


def _ffn(x, gpre, wg, wu, wd, gpost, gfin, final):
    t = x.shape[0]
    return pl.pallas_call(
        functools.partial(_ffn_kernel, final=final),
        grid=(t // TMF,),
        in_specs=[
            pl.BlockSpec((TMF, D_MODEL), lambda i: (i, 0)),
            _const_spec((1, D_MODEL)),
            _const_spec((D_MODEL, D_FF)),
            _const_spec((D_MODEL, D_FF)),
            _const_spec((D_FF, D_MODEL)),
            _const_spec((1, D_MODEL)),
            _const_spec((1, D_MODEL)),
        ],
        out_specs=pl.BlockSpec((TMF, D_MODEL), lambda i: (i, 0)),
        out_shape=jax.ShapeDtypeStruct((t, D_MODEL), F32),
        compiler_params=pltpu.CompilerParams(
            dimension_semantics=("arbitrary",), vmem_limit_bytes=VMEM_LIMIT),
        name="ffn_final" if final else "ffn",
    )(x, gpre, wg, wu, wd, gpost, gfin)


def _mix_in_kernel(x_ref, g_ref, wk_ref, wqt_ref, wvt_ref, wgla_ref, wgr_ref, wsm_ref,
                   bf_ref, pk_ref, pqt_ref, wg2_ref, bg_ref,
                   k_ref, qt_ref, vt_ref, gqk_ref, gv_ref, la_ref, sr_ref,
                   carry_ref, *, tiles_per_seq):
    i = pl.program_id(0)

    @pl.when(i % tiles_per_seq == 0)
    def _():
        carry_ref[...] = jnp.zeros_like(carry_ref)

    gs = range(MIXIN_GROUPS)
    groups = [slice(n * TM, (n + 1) * TM) for n in gs]
    lane = lax.broadcasted_iota(jnp.int32, (1, LANES), 1)
    low = lane < FOX_HEAD_DIM
    row = lax.broadcasted_iota(jnp.int32, (TM, TM), 0)
    col = lax.broadcasted_iota(jnp.int32, (TM, TM), 1)
    tri = jnp.where(col <= row, 1.0, 0.0).astype(BF16)

    h = [_rms(x_ref[r, :], g_ref[...]).astype(BF16) for r in groups]
    gl = [_dot(hn, wgla_ref[...]) for hn in h]
    for n, r in enumerate(groups):
        gqk_ref[r, :] = gl[n][:, :2 * GLA_QK].astype(BF16)
        gv_ref[r, :] = gl[n][:, 2 * GLA_QK:].astype(BF16)
    gr = [_dot(hn, wgr_ref[...]) for hn in h]
    for n, r in enumerate(groups):
        sr_ref[r, :] = _silu(gr[n]).astype(BF16)

    sm = [_dot(hn, wsm_ref[...]) for hn in h]

    carry = carry_ref[...]
    packed = []
    for n in gs:
        lf = jnp.where(lane < FOX_HEADS, _log_sigmoid(sm[n] + bf_ref[...]), 0.0)
        c3 = _dot(tri, _pack_terms(lf))
        c = jnp.where(lane < FOX_HEADS,
                      c3 + pltpu.roll(c3, LANES - FOX_HEADS, 1)
                      + pltpu.roll(c3, LANES - 2 * FOX_HEADS, 1), 0.0) + carry
        carry = c[TM - 1:TM, :]
        packed.append(_pack_terms(c * LOG2E, one_lane=ONE_LANE))
    carry_ref[...] = carry

    kp = [_dot(hn, wk_ref[...]) for hn in h]
    kaug = [_dot(pn, pk_ref[...]) for pn in packed]
    for n, r in enumerate(groups):
        for hd in range(FOX_HEADS):
            pair = slice((hd // 2) * LANES, (hd // 2 + 1) * LANES)
            slot = slice(hd * LANES, (hd + 1) * LANES)
            own = low if hd % 2 == 0 else jnp.logical_not(low)
            k_ref[r, slot] = jnp.where(own, kp[n][:, pair], kaug[n][:, pair]).astype(BF16)

    qt = [_dot_nt(wqt_ref[...], hn).astype(BF16) for hn in h]
    aug = [_dot_nt(pqt_ref[...], pn).astype(BF16) for pn in packed]
    vt = [_dot_nt(wvt_ref[...], hn).astype(BF16) for hn in h]
    one_rows = jnp.where(lax.broadcasted_iota(jnp.int32, (AUG_ROWS, TQ), 0) == 0,
                         1.0, 0.0).astype(BF16)
    zeros = jnp.zeros((LANES - FOX_HEAD_DIM - AUG_ROWS, TQ), BF16)
    for n in gs:
        for b in range(TM // TQ):
            cs = slice(b * TQ, (b + 1) * TQ)
            blk = n * (TM // TQ) + b
            for hd in range(FOX_HEADS):
                q_h = qt[n][hd * FOX_HEAD_DIM:(hd + 1) * FOX_HEAD_DIM, cs]
                a_h = aug[n][hd * AUG_ROWS:(hd + 1) * AUG_ROWS, cs]
                pieces = (q_h, a_h, zeros) if hd % 2 == 0 else (a_h, zeros, q_h)
                p0 = hd * LANES
                for piece in pieces:
                    qt_ref[blk, p0:p0 + piece.shape[0], :] = piece
                    p0 += piece.shape[0]
                vb = hd * V_ROWS
                vt_ref[blk, vb:vb + FOX_HEAD_DIM, :] = vt[n][hd * FOX_HEAD_DIM:
                                                             (hd + 1) * FOX_HEAD_DIM, cs]
                vt_ref[blk, vb + FOX_HEAD_DIM:vb + V_ROWS, :] = one_rows

    w_hi, w_mid, _ = _split3(wg2_ref[...])
    for n, r in enumerate(groups):
        g_hi, g_mid, _ = _split3(sm[n])
        gate = _dot(g_hi, w_hi) + _dot(g_mid, w_hi) + _dot(g_hi, w_mid) + bg_ref[...]
        la_ref[r, :] = _log_sigmoid(gate) / GLA_GATE_NORM


def _mix_in(x, g, wk, wqt, wvt, wgla, wgr, wsm, bf, pk, pqt, wg2, bg, seq):
    t = x.shape[0]
    row = lambda w: pl.BlockSpec((TMI, w), lambda i: (i, 0))
    tblk = lambda rows: pl.BlockSpec((TMI // TQ, rows, TQ), lambda i: (i, 0, 0))
    return pl.pallas_call(
        functools.partial(_mix_in_kernel, tiles_per_seq=seq // TMI),
        grid=(t // TMI,),
        in_specs=[
            row(D_MODEL),
            _const_spec((1, D_MODEL)),
            _const_spec((D_MODEL, FOX_WIDTH)),
            _const_spec((FOX_WIDTH, D_MODEL)),
            _const_spec((FOX_WIDTH, D_MODEL)),
            _const_spec((D_MODEL, 2 * GLA_QK + GLA_WIDTH)),
            _const_spec((D_MODEL, GLA_WIDTH)),
            _const_spec((D_MODEL, LANES)),
            _const_spec((1, LANES)),
            _const_spec((LANES, FOX_WIDTH)),
            _const_spec((FOX_HEADS * AUG_ROWS, LANES)),
            _const_spec((LANES, GLA_QK)),
            _const_spec((1, GLA_QK)),
        ],
        out_specs=[row(FOX_PAD), tblk(FOX_PAD), tblk(FOX_HEADS * V_ROWS), row(2 * GLA_QK),
                   row(GLA_WIDTH), row(GLA_QK), row(GLA_WIDTH)],
        out_shape=[
            jax.ShapeDtypeStruct((t, FOX_PAD), BF16),
            jax.ShapeDtypeStruct((t // TQ, FOX_PAD, TQ), BF16),
            jax.ShapeDtypeStruct((t // TQ, FOX_HEADS * V_ROWS, TQ), BF16),
            jax.ShapeDtypeStruct((t, 2 * GLA_QK), BF16),
            jax.ShapeDtypeStruct((t, GLA_WIDTH), BF16),
            jax.ShapeDtypeStruct((t, GLA_QK), F32),
            jax.ShapeDtypeStruct((t, GLA_WIDTH), BF16),
        ],
        scratch_shapes=[pltpu.VMEM((1, LANES), F32)],
        compiler_params=pltpu.CompilerParams(
            dimension_semantics=("arbitrary",), vmem_limit_bytes=VMEM_LIMIT),
        name="mix_in",
    )(x, g, wk, wqt, wvt, wgla, wgr, wsm, bf, pk, pqt, wg2, bg)


def _fox_kernel(qt_ref, k_ref, vt_ref, g_ref, o_ref, m_ref, acc_ref, sa_ref, sb_ref):
    i = pl.program_id(1)
    m_ref[...] = jnp.full(m_ref.shape, -jnp.inf, F32)
    acc_ref[...] = jnp.zeros(acc_ref.shape, F32)
    key = lax.broadcasted_iota(jnp.int32, (TQ, TQ), 0)
    qry = lax.broadcasted_iota(jnp.int32, (TQ, TQ), 1)
    causal = key <= qry
    slots = [slice(h * LANES, (h + 1) * LANES) for h in range(FOX_HEADS)]
    vrows = [slice(h * V_ROWS, (h + 1) * V_ROWS) for h in range(FOX_HEADS)]
    both = (0, 1)

    def score(j, s_ref, g, h):
        start = pl.multiple_of(j * TQ, TQ)
        s_ref[g, h] = _dot(k_ref[pl.ds(start, TQ), slots[h]], qt_ref[g, slots[h], :])

    def accumulate(j, s_ref, g, h, diagonal):
        ps, alphas = [], []
        for q0 in range(0, TQ, LANES):
            qs = slice(q0, q0 + LANES)
            s = s_ref[g, h, :, qs]
            if diagonal:
                s = jnp.where(causal[:, qs], s, -jnp.inf)
            m_old = m_ref[g, h:h + 1, qs]
            m_new = jnp.maximum(m_old, jnp.max(s, axis=0, keepdims=True))
            alphas.append(jnp.exp2(m_old - m_new))
            ps.append(jnp.exp2(s - m_new).astype(BF16))
            m_ref[g, h:h + 1, qs] = m_new
        p = jnp.concatenate(ps, axis=1)
        alpha = jnp.concatenate(alphas, axis=1)
        acc_ref[g, h] = alpha * acc_ref[g, h] + _dot(vt_ref[j, vrows[h], :], p)

    def stage(cur, nxt):
        todo = [(g, h) for g in nxt[2] for h in range(FOX_HEADS)] if nxt else []
        units = [(g, h) for g in cur[2] for h in range(FOX_HEADS)] if cur else []
        for g, h in todo[:SCORE_LEAD]:
            score(nxt[0], nxt[1], g, h)
        todo = todo[SCORE_LEAD:]
        per_unit = -(-len(todo) // len(units)) if units else 0
        for g, h in units:
            accumulate(cur[0], cur[1], g, h, g == cur[3])
            for g2, h2 in todo[:per_unit]:
                score(nxt[0], nxt[1], g2, h2)
            todo = todo[per_unit:]
        for g, h in todo:
            score(nxt[0], nxt[1], g, h)

    stage(None, (0, sa_ref, both))

    def pair(t, carry):
        j = 2 * t
        stage((j, sa_ref, both, None), (j + 1, sb_ref, both))
        stage((j + 1, sb_ref, both, None), (j + 2, sa_ref, both))
        return carry

    def quad(t, carry):
        pair(2 * t, carry)
        return pair(2 * t + 1, carry)

    lax.fori_loop(0, i // 2, quad, 0)

    @pl.when(i % 2 == 1)
    def _():
        pair(i - 1, 0)

    stage((2 * i, sa_ref, both, 0), (2 * i + 1, sb_ref, (1,)))
    stage((2 * i + 1, sb_ref, (1,), 1), None)

    for g in both:
        for h in range(FOX_HEADS):
            acc = acc_ref[g, h]
            o = acc[:FOX_HEAD_DIM, :] / acc[V_ONE_ROW:V_ONE_ROW + 1, :]
            ms = jnp.mean(o * o, axis=0, keepdims=True)
            rows = slice(h * FOX_HEAD_DIM, (h + 1) * FOX_HEAD_DIM)
            o_ref[rows, g * TQ:(g + 1) * TQ] = (o * lax.rsqrt(ms + RMS_EPS)
                                                * g_ref[rows, :]).astype(BF16)


def _fox(qt, k, vt, g, batch, seq):
    nq = seq // TQ
    steps = nq // 2
    return pl.pallas_call(
        _fox_kernel,
        grid=(batch, steps),
        in_specs=[
            pl.BlockSpec((2, FOX_PAD, TQ), lambda b, i: (b * steps + i, 0, 0)),
            pl.BlockSpec((seq, FOX_PAD), lambda b, i: (b, 0)),
            pl.BlockSpec((nq, FOX_HEADS * V_ROWS, TQ), lambda b, i: (b, 0, 0)),
            _const_spec((FOX_WIDTH, TQ)),
        ],
        out_specs=pl.BlockSpec((FOX_WIDTH, 2 * TQ), lambda b, i: (0, b * steps + i)),
        out_shape=jax.ShapeDtypeStruct((FOX_WIDTH, batch * seq), BF16),
        scratch_shapes=[pltpu.VMEM((2, FOX_HEADS, TQ), F32),
                        pltpu.VMEM((2, FOX_HEADS, V_ROWS, TQ), F32),
                        pltpu.VMEM((2, FOX_HEADS, TQ, TQ), F32),
                        pltpu.VMEM((2, FOX_HEADS, TQ, TQ), F32)],
        compiler_params=pltpu.CompilerParams(
            dimension_semantics=("arbitrary", "arbitrary"), vmem_limit_bytes=VMEM_LIMIT),
        name="fox",
    )(qt, k, vt, g)


def _gla_kernel(qk_ref, v_ref, la_ref, sr_ref, g_ref, o_ref, st_ref):
    @pl.when(pl.program_id(1) == 0)
    def _():
        st_ref[...] = jnp.zeros_like(st_ref)

    nchunk = TG // CHUNK
    seqs = range(GLA_SEQS)
    heads = range(GLA_HEADS)
    lane = lax.broadcasted_iota(jnp.int32, (1, LANES), 1)
    first = lane < GLA_KEY_DIM
    subs = [slice(u * GLA_SUB, (u + 1) * GLA_SUB) for u in range(TG // GLA_SUB)]
    row = lax.broadcasted_iota(jnp.int32, (GLA_SUB, GLA_SUB), 0)
    col = lax.broadcasted_iota(jnp.int32, (GLA_SUB, GLA_SUB), 1)
    causal = (row - col).astype(jnp.uint32) <= (row & (CHUNK - 1)).astype(jnp.uint32)
    tri = jnp.where(causal, 1.0, 0.0).astype(BF16)
    scale = GLA_KEY_DIM ** -0.5
    zero = jnp.zeros((TG, LANES), BF16)
    chunks = [slice(c * CHUNK, (c + 1) * CHUNK) for c in range(nchunk)]
    prs = [slice((h // 2) * LANES, (h // 2 + 1) * LANES) for h in heads]
    cols = [slice(h * GLA_VAL_DIM, (h + 1) * GLA_VAL_DIM) for h in heads]
    hmasks = [first if h % 2 == 0 else jnp.logical_not(first) for h in heads]

    srow = lax.broadcasted_iota(jnp.int32, (LANES, 2 * GLA_VAL_DIM), 0) // GLA_KEY_DIM
    scol = lax.broadcasted_iota(jnp.int32, (LANES, 2 * GLA_VAL_DIM), 1) // GLA_VAL_DIM
    own_block = srow == scol
    pairs = range(GLA_HEADS // 2)
    pcols = [slice(p * 2 * GLA_VAL_DIM, (p + 1) * 2 * GLA_VAL_DIM) for p in pairs]
    pk = [slice(p * LANES, (p + 1) * LANES) for p in pairs]

    b = [jnp.concatenate([_tri_cumsum(tri, la_ref[n, u, :]) for u in subs], axis=0)
         for n in seqs]
    qp, qd, kr, k_inv, decay_t = [], [], [], [], []
    for n in seqs:
        b_last3 = b[n].reshape(nchunk, CHUNK, GLA_QK)[:, CHUNK - 1:CHUNK, :]
        b_last = jnp.broadcast_to(b_last3, (nchunk, CHUNK, GLA_QK)).reshape(TG, GLA_QK)
        q = qk_ref[n, :, 0:GLA_QK].astype(F32)
        k = qk_ref[n, :, GLA_QK:2 * GLA_QK].astype(F32)
        q_dec = (q * scale * jnp.exp(b[n])).astype(BF16)
        kr.append((k * jnp.exp(b_last - b[n])).astype(BF16))
        k_inv.append((k * jnp.exp(-b[n])).astype(BF16))
        decay = jnp.exp(b_last3).reshape(nchunk, GLA_QK)
        decay_t.append(jnp.concatenate(
            [decay, jnp.zeros((LANES - nchunk, GLA_QK), F32)], axis=0).T)
        qp.append(q_dec)
        qd.append([jnp.where(hmasks[h], q_dec[:, prs[h]], zero) for h in heads])
    kv, att = [], []
    for n in seqs:
        kv.append([[_dot_tn(kr[n][r, pk[p]], v_ref[n, r, pcols[p]]) for r in chunks]
                   for p in pairs])
        att.append([[jnp.where(causal, _dot_nt(qd[n][h][u], k_inv[n][u, prs[h]]), 0.0).astype(BF16)
                     for u in subs] for h in heads])
    states = []
    for n in seqs:
        per_pair = []
        for p in pairs:
            st = st_ref[n, p]
            per_chunk = []
            for c in range(nchunk):
                per_chunk.append(jnp.where(own_block, st, 0.0).astype(BF16))
                st = st * decay_t[n][pk[p], c:c + 1] + kv[n][p][c]
            st_ref[n, p] = st
            per_pair.append(per_chunk)
        states.append(per_pair)
    for n in seqs:
        inter = [jnp.concatenate([_dot(qp[n][chunks[c], pk[p]], states[n][p][c])
                                  for c in range(nchunk)], axis=0) for p in pairs]
        for h in heads:
            half = slice((h % 2) * GLA_VAL_DIM, (h % 2 + 1) * GLA_VAL_DIM)
            intra = jnp.concatenate([_dot(att[n][h][ui], v_ref[n, u, cols[h]])
                                     for ui, u in enumerate(subs)], axis=0)
            o = intra + inter[h // 2][:, half]
            ms = jnp.mean(o * o, axis=-1, keepdims=True)
            y = o * lax.rsqrt(ms + RMS_EPS) * g_ref[:, cols[h]] * sr_ref[n, :, cols[h]].astype(F32)
            o_ref[n, :, cols[h]] = y.astype(BF16)


def _gla(gqk, gv, la, sr, g, batch, seq):
    blk = lambda w: pl.BlockSpec((GLA_SEQS, TG, w), lambda b, i: (b, i, 0))
    return pl.pallas_call(
        _gla_kernel,
        grid=(batch // GLA_SEQS, seq // TG),
        in_specs=[blk(2 * GLA_QK), blk(GLA_WIDTH), blk(GLA_QK), blk(GLA_WIDTH),
                  pl.BlockSpec((1, GLA_WIDTH), lambda b, i: (0, 0))],
        out_specs=blk(GLA_WIDTH),
        out_shape=jax.ShapeDtypeStruct((batch, seq, GLA_WIDTH), BF16),
        scratch_shapes=[pltpu.VMEM((GLA_SEQS, GLA_HEADS // 2, LANES, 2 * GLA_VAL_DIM), F32)],
        compiler_params=pltpu.CompilerParams(
            dimension_semantics=("arbitrary", "arbitrary"), vmem_limit_bytes=VMEM_LIMIT),
        name="gla",
    )(gqk, gv, la, sr, g)


def _mem_kv_kernel(m_ref, g_ref, w_ref, o_ref):
    h = _rms(m_ref[...], g_ref[...]).astype(BF16)
    o_ref[...] = _dot(h, w_ref[...]).astype(BF16)


def _mem_kv(mem, g, w, mem_len):
    t = mem.shape[0]
    width = w.shape[1]
    return pl.pallas_call(
        _mem_kv_kernel,
        grid=(t // mem_len,),
        in_specs=[pl.BlockSpec((mem_len, D_MODEL), lambda i: (i, 0)),
                  _const_spec((1, D_MODEL)), _const_spec((D_MODEL, width))],
        out_specs=pl.BlockSpec((mem_len, width), lambda i: (i, 0)),
        out_shape=jax.ShapeDtypeStruct((t, width), BF16),
        compiler_params=pltpu.CompilerParams(
            dimension_semantics=("arbitrary",), vmem_limit_bytes=VMEM_LIMIT),
        name="mem_kv",
    )(mem, g, w)


def _mix_out_mem_kernel(x_ref, oft_ref, og_ref, kv_ref, wout_ref, gpost_ref, gpre_ref,
                        wq_ref, wo_ref, gmpost_ref, o_ref):
    width = MEM_HEADS * MEM_HEAD_DIM
    groups = [slice(n * TMO // MIX_GROUPS, (n + 1) * TMO // MIX_GROUPS) for n in range(MIX_GROUPS)]
    y = [_dot_tn(oft_ref[:, r], wout_ref[0:FOX_WIDTH, :]) + _dot(og_ref[r, :], wout_ref[FOX_WIDTH:, :])
         for r in groups]
    x = [x_ref[r, :] + _rms(y[n], gpost_ref[...]) for n, r in enumerate(groups)]
    h = [_rms(xn, gpre_ref[...]).astype(BF16) for xn in x]
    q = [(_dot(hn, wq_ref[...]) * (MEM_HEAD_DIM ** -0.5)).astype(BF16) for hn in h]
    hcols = [slice(hd * MEM_HEAD_DIM, (hd + 1) * MEM_HEAD_DIM) for hd in range(MEM_HEADS)]
    vcols = [slice(width + hd * MEM_HEAD_DIM, width + (hd + 1) * MEM_HEAD_DIM)
             for hd in range(MEM_HEADS)]
    s = [[_dot_nt(qn[:, c], kv_ref[:, c]) for c in hcols] for qn in q]
    o = []
    for sn in s:
        outs = []
        for hd in range(MEM_HEADS):
            m = jnp.max(sn[hd], axis=-1, keepdims=True)
            p = jnp.exp(sn[hd] - m)
            p = p / jnp.sum(p, axis=-1, keepdims=True)
            outs.append(_dot(p.astype(BF16), kv_ref[:, vcols[hd]]).astype(BF16))
        o.append(jnp.concatenate(outs, axis=-1))
    y2 = [_dot(on, wo_ref[...]) for on in o]
    for n, r in enumerate(groups):
        o_ref[r, :] = x[n] + _rms(y2[n], gmpost_ref[...])


def _mix_out_mem(x, o_fox_t, o_gla, kv, wout, gpost, gpre, wq, wo, gmpost, seq, mem_len):
    t = x.shape[0]
    tiles_per_seq = seq // TMO
    row = lambda w: pl.BlockSpec((TMO, w), lambda i: (i, 0))
    return pl.pallas_call(
        _mix_out_mem_kernel,
        grid=(t // TMO,),
        in_specs=[
            row(D_MODEL),
            pl.BlockSpec((FOX_WIDTH, TMO), lambda i: (0, i)),
            row(GLA_WIDTH),
            pl.BlockSpec((mem_len, kv.shape[1]), lambda i: (i // tiles_per_seq, 0)),
            _const_spec((D_MODEL, D_MODEL)), _const_spec((1, D_MODEL)), _const_spec((1, D_MODEL)),
            _const_spec((D_MODEL, D_MODEL)), _const_spec((D_MODEL, D_MODEL)),
            _const_spec((1, D_MODEL)),
        ],
        out_specs=row(D_MODEL),
        out_shape=jax.ShapeDtypeStruct((t, D_MODEL), F32),
        compiler_params=pltpu.CompilerParams(
            dimension_semantics=("arbitrary",), vmem_limit_bytes=VMEM_LIMIT),
        name="mix_out_mem",
    )(x, o_fox_t, o_gla, kv, wout, gpost, gpre, wq, wo, gmpost)


def _pad_cols(w, width):
    return jnp.pad(w, ((0, 0), (0, width - w.shape[1])))


def _aug_base(h):
    return FOX_HEAD_DIM if h % 2 == 0 else 0


def _placement():
    pk = np.zeros((LANES, FOX_WIDTH), np.float32)
    pq = np.zeros((FOX_HEADS * AUG_ROWS, LANES), np.float32)
    for h in range(FOX_HEADS):
        kbase = (h // 2) * LANES + _aug_base(h)
        for t in range(C_TERMS):
            src = t * FOX_HEADS + h
            pq[h * AUG_ROWS + t, src] = 1.0
            pk[ONE_LANE, kbase + t] = 1.0
            pq[h * AUG_ROWS + C_TERMS + t, ONE_LANE] = 1.0
            pk[src, kbase + C_TERMS + t] = -1.0
    return jnp.asarray(pk, BF16), jnp.asarray(pq, BF16)


def kernel(x, mem, g_ff1_pre, w_ff1_gate, w_ff1_up, w_ff1_down, g_ff1_post, g_mix_pre, w_mix_in, b_fox_f, w_gla_g2, b_gla_g, g_fox_out, g_gla_out, w_mix_out, g_mix_post, g_mem_pre, g_mem_src, w_mem_q, w_mem_kv, w_mem_o, g_mem_post, g_ff2_pre, w_ff2_gate, w_ff2_up, w_ff2_down, g_ff2_post, g_final):
    batch, seq, d = x.shape
    mem_len = mem.shape[1]
    depth = w_ff1_gate.shape[0]
    xt = x.reshape(batch * seq, d)
    memt = mem.reshape(batch * mem_len, d)
    gfin = g_final.reshape(1, d)
    vec = lambda v: v.reshape(1, -1).astype(F32)
    pk, pqt = _placement()
    for l in range(depth):
        last = l == depth - 1
        xt = _ffn(xt, vec(g_ff1_pre[l]), w_ff1_gate[l].astype(BF16), w_ff1_up[l].astype(BF16),
                  w_ff1_down[l].astype(BF16), vec(g_ff1_post[l]), gfin, False)

        w_in = w_mix_in[l]
        wqt = (w_in[:, OFF_FQ:OFF_FK] * (LOG2E * FOX_HEAD_DIM ** -0.5)).T.astype(BF16)
        wk = w_in[:, OFF_FK:OFF_FV].astype(BF16)
        wvt = w_in[:, OFF_FV:OFF_FF].T.astype(BF16)
        wgla = w_in[:, OFF_GQ:OFF_GG].astype(BF16)
        wgr = w_in[:, OFF_GR:IN_WIDTH].astype(BF16)
        wsm = _pad_cols(jnp.concatenate([w_in[:, OFF_FF:OFF_GQ], w_in[:, OFF_GG:OFF_GR]], axis=1),
                        LANES).astype(BF16)
        bf = _pad_cols(vec(b_fox_f[l]), LANES)
        wg2 = jnp.pad(w_gla_g2[l].astype(F32),
                      ((FOX_HEADS, LANES - FOX_HEADS - GLA_GATE_RANK), (0, 0)))
        k, qt, vt, gqk, gv, la, sr = _mix_in(xt, vec(g_mix_pre[l]), wk, wqt, wvt, wgla, wgr, wsm,
                                             bf, pk, pqt, wg2, vec(b_gla_g[l]), seq)

        g_fox = jnp.broadcast_to(g_fox_out[l].astype(F32)[:, None], (FOX_WIDTH, TQ))
        o_fox_t = _fox(qt, k, vt, g_fox, batch, seq)
        bsd = lambda a: a.reshape(batch, seq, a.shape[-1])
        o_gla = _gla(bsd(gqk), bsd(gv), bsd(la), bsd(sr), vec(g_gla_out[l]), batch, seq)
        o_gla = o_gla.reshape(batch * seq, GLA_WIDTH)

        kv = _mem_kv(memt, vec(g_mem_src[l]), w_mem_kv[l].astype(BF16), mem_len)
        xt = _mix_out_mem(xt, o_fox_t, o_gla, kv, w_mix_out[l].astype(BF16), vec(g_mix_post[l]),
                          vec(g_mem_pre[l]), w_mem_q[l].astype(BF16), w_mem_o[l].astype(BF16),
                          vec(g_mem_post[l]), seq, mem_len)

        xt = _ffn(xt, vec(g_ff2_pre[l]), w_ff2_gate[l].astype(BF16), w_ff2_up[l].astype(BF16),
                  w_ff2_down[l].astype(BF16), vec(g_ff2_post[l]), gfin, last)
    return xt.reshape(batch, seq, d)
```
